```python
import math
import jax, jax.numpy as jnp
from jax import lax
import numpy as np

D_MODEL = 1024
BATCH = 16
SEQ = 2048
DEPTH = 4

HEAD_DIM = 64
N_HEADS = D_MODEL // HEAD_DIM
N_KV_HEADS = 4
GROUP = N_HEADS // N_KV_HEADS
HQ = N_HEADS * HEAD_DIM
HKV = N_KV_HEADS * HEAD_DIM
N_A_LAYERS = DEPTH // 2
N_B_LAYERS = DEPTH - N_A_LAYERS
SWA_WINDOW = 128
Q_BLOCK = 128
N_BRANCH = 3
CMP_LEN = 32
CMP_STRIDE = 16
CMP_HIDDEN = 256
SEL_BLOCK = 64
SEL_TOP = 8
SEL_FORCE_LOCAL = 2
SEL_CHUNK = 32
NSA_WINDOW = 512
NUM_BUCKETS = 32
MAX_DISTANCE = 128
EPS = 1e-6
NEG = -1e30
FORCE_BONUS = 1e6
A_IN = 2 * HQ + 2 * HKV
B_IN = HQ + N_BRANCH * N_HEADS + N_BRANCH * HQ

kernel_name = "yoco_swa_sink_nsa_hybrid"


def rms_norm(x, g):
    xf = x.astype(jnp.float32)
    xf = xf * lax.rsqrt(jnp.mean(xf * xf, axis=-1, keepdims=True) + EPS)
    return (xf * g.astype(jnp.float32)).astype(x.dtype)


def t5_bucket(dist):
    d = jnp.maximum(dist, 0)
    max_exact = NUM_BUCKETS // 2
    large = max_exact + (jnp.log(jnp.maximum(d, 1).astype(jnp.float32) / max_exact)
                         / math.log(MAX_DISTANCE / max_exact)
                         * (NUM_BUCKETS - max_exact)).astype(jnp.int32)
    large = jnp.minimum(large, NUM_BUCKETS - 1)
    return jnp.where(d < max_exact, d, large)


def head_bias(table, dist):
    b = table[t5_bucket(dist)]
    b = jnp.moveaxis(b, -1, 0)
    return b.reshape(N_KV_HEADS, GROUP, *dist.shape)


def banded_attention(q, k, v, window, table, sink):
    B_, S = q.shape[0], q.shape[1]
    nb = S // Q_BLOCK
    n_prev = -(-(window - 1) // Q_BLOCK)
    pad = n_prev * Q_BLOCK
    span = pad + Q_BLOCK
    kp = jnp.pad(k, ((0, 0), (pad, 0), (0, 0), (0, 0)))
    vp = jnp.pad(v, ((0, 0), (pad, 0), (0, 0), (0, 0)))
    scale = HEAD_DIM ** -0.5

    def block(i):
        start = i * Q_BLOCK
        qb = lax.dynamic_slice_in_dim(q, start, Q_BLOCK, axis=1)
        kb = lax.dynamic_slice_in_dim(kp, start, span, axis=1)
        vb = lax.dynamic_slice_in_dim(vp, start, span, axis=1)
        qpos = start + jnp.arange(Q_BLOCK)
        kpos = start - pad + jnp.arange(span)
        dist = qpos[:, None] - kpos[None, :]
        mask = (dist >= 0) & (dist < window) & (kpos[None, :] >= 0)
        logits = jnp.einsum('bqhgd,bkhd->bhgqk', qb, kb,
                            preferred_element_type=jnp.float32) * scale
        logits = jnp.where(mask, logits + head_bias(table, dist), NEG)
        if sink is None:
            p = jax.nn.softmax(logits, axis=-1)
        else:
            s = sink.astype(jnp.float32)[None, :, :, None, None]
            m = jnp.maximum(jnp.max(logits, axis=-1, keepdims=True), s)
            e = jnp.exp(logits - m)
            p = e / (jnp.sum(e, axis=-1, keepdims=True) + jnp.exp(s - m))
        return jnp.einsum('bhgqk,bkhd->bqhgd', p.astype(v.dtype), vb)

    out = lax.map(block, jnp.arange(nb))
    return out.transpose(1, 0, 2, 3, 4, 5).reshape(q.shape)


def swa_layer(x, norm_g, w_in, q_gain, k_gain, sink, w_out, table):
    B_, S, _ = x.shape
    proj = rms_norm(x, norm_g) @ w_in
    q = rms_norm(proj[..., :HQ].reshape(B_, S, N_KV_HEADS, GROUP, HEAD_DIM), q_gain)
    k = rms_norm(proj[..., HQ:HQ + HKV].reshape(B_, S, N_KV_HEADS, HEAD_DIM), k_gain)
    v = proj[..., HQ + HKV:HQ + 2 * HKV].reshape(B_, S, N_KV_HEADS, HEAD_DIM)
    z = proj[..., HQ + 2 * HKV:]
    o = banded_attention(q, k, v, SWA_WINDOW, table, sink.reshape(N_KV_HEADS, GROUP))
    o = o.reshape(B_, S, HQ) * jax.nn.silu(z)
    return x + o @ w_out


def compress(t, pos_emb, w1, w2):
    B_, S = t.shape[0], t.shape[1]
    r = CMP_LEN // CMP_STRIDE
    chunks = t.reshape(B_, S // CMP_STRIDE, CMP_STRIDE, N_KV_HEADS, HEAD_DIM)
    n_cmp = S // CMP_STRIDE - r + 1
    blocks = jnp.concatenate([chunks[:, j:j + n_cmp] for j in range(r)], axis=2)
    blocks = blocks + pos_emb[:, None, :]
    flat = blocks.transpose(0, 1, 3, 2, 4).reshape(B_, n_cmp, N_KV_HEADS, CMP_LEN * HEAD_DIM)
    return jax.nn.silu(flat @ w1) @ w2


def nsa_shared_kv(x, norm_g, w_kv, k_gain, cmp_k_pos, cmp_k_w1, cmp_k_w2,
                  cmp_v_pos, cmp_v_w1, cmp_v_w2):
    B_, S, _ = x.shape
    kv = (rms_norm(x, norm_g) @ w_kv).reshape(B_, S, 2 * N_BRANCH, N_KV_HEADS, HEAD_DIM)
    k_cmp = rms_norm(compress(kv[:, :, 0], cmp_k_pos, cmp_k_w1, cmp_k_w2), k_gain[0])
    v_cmp = compress(kv[:, :, 1], cmp_v_pos, cmp_v_w1, cmp_v_w2)
    k_slc = rms_norm(kv[:, :, 2], k_gain[1])
    v_slc = kv[:, :, 3]
    k_win = rms_norm(kv[:, :, 4], k_gain[2])
    v_win = kv[:, :, 5]
    return k_cmp, v_cmp, k_slc, v_slc, k_win, v_win


def selected_attention(q, k, v, sel_idx, table):
    B_, S = q.shape[0], q.shape[1]
    nsel = S // SEL_BLOCK
    n_top = sel_idx.shape[-1]
    n_keys = n_top * SEL_BLOCK
    kb = k.reshape(B_, nsel, SEL_BLOCK, N_KV_HEADS, HEAD_DIM).transpose(0, 3, 1, 2, 4)
    kb = kb.reshape(B_, N_KV_HEADS, nsel, SEL_BLOCK * HEAD_DIM)
    vb = v.reshape(B_, nsel, SEL_BLOCK, N_KV_HEADS, HEAD_DIM).transpose(0, 3, 1, 2, 4)
    vb = vb.reshape(B_, N_KV_HEADS, nsel, SEL_BLOCK * HEAD_DIM)
    table_t = table.reshape(NUM_BUCKETS, N_KV_HEADS, GROUP).transpose(1, 0, 2)
    head_idx = jnp.arange(N_KV_HEADS)[None, :, None, None]
    offs = jnp.arange(SEL_BLOCK)
    scale = HEAD_DIM ** -0.5

    def chunk(c):
        start = c * SEL_CHUNK
        qc = lax.dynamic_slice_in_dim(q, start, SEL_CHUNK, axis=1)
        ic = lax.dynamic_slice_in_dim(sel_idx, start, SEL_CHUNK, axis=2)
        flat = ic.reshape(B_, N_KV_HEADS, SEL_CHUNK * n_top, 1)
        kg = jnp.take_along_axis(kb, flat, axis=2).reshape(B_, N_KV_HEADS, SEL_CHUNK, n_keys, HEAD_DIM)
        vg = jnp.take_along_axis(vb, flat, axis=2).reshape(B_, N_KV_HEADS, SEL_CHUNK, n_keys, HEAD_DIM)
        kpos = (ic[..., None] * SEL_BLOCK + offs).reshape(B_, N_KV_HEADS, SEL_CHUNK, n_keys)
        dist = (start + jnp.arange(SEL_CHUNK))[None, None, :, None] - kpos
        logits = jnp.einsum('bqhgd,bhqkd->bhgqk', qc, kg,
                            preferred_element_type=jnp.float32) * scale
        bias = jnp.moveaxis(table_t[head_idx, t5_bucket(dist)], -1, 2)
        logits = jnp.where((dist >= 0)[:, :, None], logits + bias, NEG)
        p = jax.nn.softmax(logits, axis=-1)
        return jnp.einsum('bhgqk,bhqkd->bqhgd', p.astype(v.dtype), vg)

    out = lax.map(chunk, jnp.arange(S // SEL_CHUNK))
    return out.transpose(1, 0, 2, 3, 4, 5).reshape(q.shape)


def nsa_layer(x, norm_g, w_in, q_gain, w_out, table, k_cmp, v_cmp, k_slc, v_slc, k_win, v_win):
    B_, S, _ = x.shape
    proj = rms_norm(x, norm_g) @ w_in
    q = rms_norm(proj[..., :HQ].reshape(B_, S, N_KV_HEADS, GROUP, HEAD_DIM), q_gain)
    gate_logits = proj[..., HQ:HQ + N_BRANCH * N_HEADS].reshape(B_, S, N_BRANCH, N_HEADS)
    z = proj[..., HQ + N_BRANCH * N_HEADS:].reshape(B_, S, N_BRANCH, N_HEADS, HEAD_DIM)
    pos = jnp.arange(S)
    scale = HEAD_DIM ** -0.5

    n_cmp = k_cmp.shape[1]
    cmp_start = jnp.arange(n_cmp) * CMP_STRIDE
    dist_c = pos[:, None] - (cmp_start + CMP_LEN - 1)[None, :]
    valid_c = dist_c >= 0
    logits = jnp.einsum('bshgd,bchd->bhgsc', q, k_cmp,
                        preferred_element_type=jnp.float32) * scale
    logits = jnp.where(valid_c, logits + head_bias(table, dist_c), NEG)
    p_cmp = jax.nn.softmax(logits, axis=-1) * valid_c
    o_cmp = jnp.einsum('bhgsc,bchd->bshgd', p_cmp.astype(v_cmp.dtype), v_cmp)

    nsel = S // SEL_BLOCK
    sel_start = jnp.arange(nsel) * SEL_BLOCK
    overlap = ((cmp_start[:, None] < sel_start[None, :] + SEL_BLOCK)
               & (cmp_start[:, None] + CMP_LEN > sel_start[None, :])).astype(jnp.float32)
    imp = jnp.einsum('bhsc,cj->bhsj', p_cmp.sum(axis=2), overlap)
    blk = jnp.arange(nsel)
    causal = sel_start[None, :] <= pos[:, None]
    rel_blk = (pos // SEL_BLOCK)[:, None] - blk[None, :]
    forced = (blk[None, :] == 0) | ((rel_blk >= 0) & (rel_blk < SEL_FORCE_LOCAL))
    score = jnp.where(causal, imp + jnp.where(forced, FORCE_BONUS, 0.0), NEG)
    _, sel_idx = lax.top_k(score, min(SEL_TOP, nsel))
    o_slc = selected_attention(q, k_slc, v_slc, sel_idx, table)

    o_win = banded_attention(q, k_win, v_win, NSA_WINDOW, table, None)

    o_all = jnp.stack([o_cmp, o_slc, o_win], axis=2).reshape(B_, S, N_BRANCH, N_HEADS, HEAD_DIM)
    o = jnp.einsum('bschd,bsch->bshd', o_all * jax.nn.silu(z), jax.nn.sigmoid(gate_logits))
    return x + o.reshape(B_, S, HQ) @ w_out


def setup_inputs(seed: int = 0) -> dict:
    key = jax.random.key(seed)
    ks = jax.random.split(key, 24)
    f32 = jnp.float32
    nrm = lambda k, shape, s: jax.random.normal(k, shape, f32) * s
    return {
        "x": nrm(ks[0], (BATCH, SEQ, D_MODEL), 1.0),
        "rel_table": nrm(ks[1], (NUM_BUCKETS, N_HEADS), 0.5),
        "a_norm": 1.0 + nrm(ks[2], (N_A_LAYERS, D_MODEL), 0.02),
        "a_w_in": nrm(ks[3], (N_A_LAYERS, D_MODEL, A_IN), D_MODEL ** -0.5),
        "a_q_gain": 1.0 + nrm(ks[4], (N_A_LAYERS, HEAD_DIM), 0.02),
        "a_k_gain": 1.0 + nrm(ks[5], (N_A_LAYERS, HEAD_DIM), 0.02),
        "a_sink": nrm(ks[6], (N_A_LAYERS, N_HEADS), 1.0),
        "a_w_out": nrm(ks[7], (N_A_LAYERS, HQ, D_MODEL), HQ ** -0.5),
        "kv_norm": 1.0 + nrm(ks[8], (D_MODEL,), 0.02),
        "kv_w": nrm(ks[9], (D_MODEL, 2 * N_BRANCH * HKV), D_MODEL ** -0.5),
        "kv_k_gain": 1.0 + nrm(ks[10], (N_BRANCH, HEAD_DIM), 0.02),
        "cmp_k_pos": nrm(ks[11], (CMP_LEN, HEAD_DIM), 0.1),
        "cmp_k_w1": nrm(ks[12], (CMP_LEN * HEAD_DIM, CMP_HIDDEN), (CMP_LEN * HEAD_DIM) ** -0.5),
        "cmp_k_w2": nrm(ks[13], (CMP_HIDDEN, HEAD_DIM), CMP_HIDDEN ** -0.5),
        "cmp_v_pos": nrm(ks[14], (CMP_LEN, HEAD_DIM), 0.1),
        "cmp_v_w1": nrm(ks[15], (CMP_LEN * HEAD_DIM, CMP_HIDDEN), (CMP_LEN * HEAD_DIM) ** -0.5),
        "cmp_v_w2": nrm(ks[16], (CMP_HIDDEN, HEAD_DIM), CMP_HIDDEN ** -0.5),
        "b_norm": 1.0 + nrm(ks[17], (N_B_LAYERS, D_MODEL), 0.02),
        "b_w_in": nrm(ks[18], (N_B_LAYERS, D_MODEL, B_IN), D_MODEL ** -0.5),
        "b_q_gain": 1.0 + nrm(ks[19], (N_B_LAYERS, HEAD_DIM), 0.02),
        "b_w_out": nrm(ks[20], (N_B_LAYERS, HQ, D_MODEL), HQ ** -0.5),
    }


def reference(x, rel_table, a_norm, a_w_in, a_q_gain, a_k_gain, a_sink, a_w_out,
              kv_norm, kv_w, kv_k_gain, cmp_k_pos, cmp_k_w1, cmp_k_w2,
              cmp_v_pos, cmp_v_w1, cmp_v_w2, b_norm, b_w_in, b_q_gain, b_w_out):
    shared = None
    for layer in range(DEPTH):
        if layer < N_A_LAYERS:
            x = swa_layer(x, a_norm[layer], a_w_in[layer], a_q_gain[layer], a_k_gain[layer],
                          a_sink[layer], a_w_out[layer], rel_table)
        else:
            if layer == N_A_LAYERS:
                shared = nsa_shared_kv(x, kv_norm, kv_w, kv_k_gain, cmp_k_pos, cmp_k_w1,
                                       cmp_k_w2, cmp_v_pos, cmp_v_w1, cmp_v_w2)
            j = layer - N_A_LAYERS
            x = nsa_layer(x, b_norm[j], b_w_in[j], b_q_gain[j], b_w_out[j], rel_table, *shared)
    return x
```

```python
import functools
import math

import numpy as np
import jax
import jax.numpy as jnp
from jax import lax
from jax.experimental import pallas as pl
from jax.experimental.pallas import tpu as pltpu

D_MODEL = 1024
HEAD_DIM = 64
N_HEADS = 16
N_KV_HEADS = 4
GROUP = N_HEADS // N_KV_HEADS
HQ = N_HEADS * HEAD_DIM
HKV = N_KV_HEADS * HEAD_DIM
N_BRANCH = 3
SWA_WINDOW = 128
NSA_WINDOW = 512
CMP_LEN = 32
CMP_STRIDE = 16
CMP_HIDDEN = 256
SEL_BLOCK = 64
SEL_TOP = 8
SEL_FORCE_LOCAL = 2
NUM_BUCKETS = 32
MAX_DISTANCE = 128
EPS = 1e-6
NEG = -1e30
FORCE_BONUS = 1e6

TQ = 128
TM = 256
GL = GROUP * TQ
PAIR = 2 * HEAD_DIM
VMEM_LIMIT = 56 * 1024 * 1024

F32 = jnp.float32
BF16 = jnp.bfloat16


def _bucket_np(dist):
    d = np.maximum(dist, 0)
    max_exact = NUM_BUCKETS // 2
    ratio = (np.log(np.maximum(d, 1).astype(np.float32) / np.float32(max_exact))
             / np.float32(math.log(MAX_DISTANCE / max_exact))
             * np.float32(NUM_BUCKETS - max_exact))
    large = np.minimum(max_exact + ratio.astype(np.int32), NUM_BUCKETS - 1)
    return np.where(d < max_exact, d, large).astype(np.int32)


def _bias_table(table, dist, mask):
    idx = _bucket_np(dist)
    tb = table.T.reshape(N_KV_HEADS, GROUP, NUM_BUCKETS)
    g = tb[:, :, idx]
    g = jnp.where(jnp.asarray(mask)[None, None], g, NEG)
    return g.transpose(0, 2, 1, 3).reshape(N_KV_HEADS, dist.shape[0], GL).astype(F32)


def _make_tables(table, nq):
    r = np.arange(TQ)[None, :]
    kk = np.arange(TQ)[:, None]
    kk2 = np.arange(2 * TQ)[:, None]
    d_swa = r - kk2 + TQ
    m_swa = (d_swa >= 0) & (d_swa < SWA_WINDOW)
    swa1 = _bias_table(table, d_swa, m_swa)
    swa0 = _bias_table(table, d_swa, m_swa & (kk2 >= TQ))
    swa = jnp.stack([swa0, swa1])
    diag = _bias_table(table, r - kk, (r - kk) >= 0)
    prev = _bias_table(table, TQ + r - kk, np.ones((TQ, TQ), bool))
    d_far = NSA_WINDOW + r - kk
    far = _bias_table(table, d_far, d_far < NSA_WINDOW)
    t31 = _bias_table(table, np.full((1, TQ), MAX_DISTANCE), np.ones((1, TQ), bool))
    ncp = nq * TQ // CMP_STRIDE
    off = (TQ // CMP_STRIDE) * (nq - 1)
    cp = np.arange(ncp + off)[:, None] - off
    d_c = r - CMP_STRIDE * cp - (CMP_LEN - 1)
    cmpb = _bias_table(table, d_c, d_c >= 0)
    return swa, diag, prev, far, t31, cmpb


def _overlap_t(s):
    ncp = s // CMP_STRIDE
    nsel = s // SEL_BLOCK
    cs = np.arange(ncp)[None, :] * CMP_STRIDE
    ss = np.arange(nsel)[:, None] * SEL_BLOCK
    ov = (cs < ss + SEL_BLOCK) & (cs + CMP_LEN > ss) & (np.arange(ncp)[None, :] < ncp - 1)
    return jnp.asarray(ov.astype(np.float32), dtype=BF16)


def _rms_t(xt, gain_b):
    ms = jnp.mean(xt * xt, axis=0, keepdims=True)
    return (xt * lax.rsqrt(ms + EPS) * gain_b).astype(BF16)


def _head_norm_t(a, gain_b):
    outs = []
    for h in range(a.shape[0] // HEAD_DIM):
        blk = a[h * HEAD_DIM:(h + 1) * HEAD_DIM]
        ms = jnp.mean(blk * blk, axis=0, keepdims=True)
        outs.append(blk * lax.rsqrt(ms + EPS) * gain_b)
    return jnp.concatenate(outs, axis=0)


def _sigmoid(z):
    return 1.0 / (1.0 + jnp.exp(-z))


def _silu(z):
    return z * _sigmoid(z)


def _dot(a, b):
    return jnp.dot(a, b, preferred_element_type=F32)


def _proj_a(xn, w, qg, kg, q_out, k_out, v_out, g_out):
    p = _dot(w[...], xn)
    q_out[0] = _head_norm_t(p[:HQ], qg[...]).astype(BF16)
    k_out[0] = _head_norm_t(p[HQ:HQ + HKV], kg[...]).T.astype(BF16)
    v_out[0] = p[HQ + HKV:HQ + 2 * HKV].astype(BF16)
    g_out[0] = _silu(p[HQ + 2 * HKV:]).astype(BF16)


def _proj_b(xn, w, qg, q_out, g_out):
    p = _dot(w[...], xn)
    q_out[0] = _head_norm_t(p[:HQ], qg[...]).astype(BF16)
    n_gate = N_BRANCH * N_HEADS
    sg = _sigmoid(p[HQ:HQ + n_gate])
    for r in range(n_gate):
        z = p[HQ + n_gate + r * HEAD_DIM:HQ + n_gate + (r + 1) * HEAD_DIM]
        g_out[0, r * HEAD_DIM:(r + 1) * HEAD_DIM, :] = (_silu(z) * sg[r:r + 1]).astype(BF16)


def _proj_kv(xn, wkv, kg1, kg2, craw_out, kslc_out, vslc_out, kwin_out, vwin_out):
    p = _dot(wkv[...], xn)

    def rows(n):
        return p[n * HKV:(n + 1) * HKV]
    craw_out[0, 0] = rows(0).T
    craw_out[0, 1] = rows(1).T
    kslc_out[0] = _head_norm_t(rows(2), kg1[...]).T.astype(BF16)
    vslc_out[0] = rows(3).astype(BF16)
    kwin_out[0] = _head_norm_t(rows(4), kg2[...]).T.astype(BF16)
    vwin_out[0] = rows(5).astype(BF16)


def _first_kernel(x_ref, ng, w, qg, kg, xt_out, q_out, k_out, v_out, g_out):
    xt = x_ref[0].T
    xt_out[0] = xt
    _proj_a(_rms_t(xt, ng[...]), w, qg, kg, q_out, k_out, v_out, g_out)


def _a2a_kernel(x_ref, o_ref, wo, ng, w, qg, kg, xt_out, q_out, k_out, v_out, g_out):
    xt = x_ref[0] + _dot(wo[...], o_ref[0])
    xt_out[0] = xt
    _proj_a(_rms_t(xt, ng[...]), w, qg, kg, q_out, k_out, v_out, g_out)


def _a2b_kernel(x_ref, o_ref, wo, ngkv, wkv, kg1, kg2, ngb, w, qg,
                xt_out, craw_out, kslc_out, vslc_out, kwin_out, vwin_out, q_out, g_out):
    xt = x_ref[0] + _dot(wo[...], o_ref[0])
    xt_out[0] = xt
    _proj_kv(_rms_t(xt, ngkv[...]), wkv, kg1, kg2, craw_out, kslc_out, vslc_out, kwin_out, vwin_out)
    _proj_b(_rms_t(xt, ngb[...]), w, qg, q_out, g_out)


def _b2b_kernel(x_ref, o_ref, wo, ngb, w, qg, xt_out, q_out, g_out):
    xt = x_ref[0] + _dot(wo[...], o_ref[0])
    xt_out[0] = xt
    _proj_b(_rms_t(xt, ngb[...]), w, qg, q_out, g_out)


def _final_kernel(x_ref, o_ref, wo, x_out):
    x_out[0] = (x_ref[0] + _dot(wo[...], o_ref[0])).T


def _tok_spec(rows):
    return pl.BlockSpec((1, rows, TM), lambda b, t: (b, 0, t))


def _nat_spec(cols):
    return pl.BlockSpec((1, TM, cols), lambda b, t: (b, t, 0))


def _const_spec(shape):
    nd = len(shape)
    return pl.BlockSpec(shape, lambda b, t: (0,) * nd)


def _proj_call(body, name, b, s, ins, in_specs, out_shapes, out_specs):
    return pl.pallas_call(
        body, name=name, grid=(b, s // TM),
        in_specs=in_specs, out_specs=out_specs, out_shape=out_shapes,
        compiler_params=pltpu.CompilerParams(
            dimension_semantics=("parallel", "parallel"), vmem_limit_bytes=VMEM_LIMIT),
    )(*ins)


def _bcast(v, scale=1.0):
    return jnp.broadcast_to((v.astype(F32) * scale)[:, None], (v.shape[0], TM))


def _a_weights(w_in, q_gain, k_gain):
    ws = [w_in.T.astype(BF16)]
    gains = [_bcast(q_gain, HEAD_DIM ** -0.5), _bcast(k_gain)]
    return ws, gains


def _b_weights(w_in, q_gain):
    ws = [w_in.T.astype(BF16)]
    return ws, [_bcast(q_gain, HEAD_DIM ** -0.5)]


def _a_outs(b, s):
    shapes = [jax.ShapeDtypeStruct((b, D_MODEL, s), F32), jax.ShapeDtypeStruct((b, HQ, s), BF16),
              jax.ShapeDtypeStruct((b, s, HKV), BF16), jax.ShapeDtypeStruct((b, HKV, s), BF16),
              jax.ShapeDtypeStruct((b, HQ, s), BF16)]
    specs = [_tok_spec(D_MODEL), _tok_spec(HQ), _nat_spec(HKV), _tok_spec(HKV), _tok_spec(HQ)]
    return shapes, specs


def _b_outs(b, s):
    shapes = [jax.ShapeDtypeStruct((b, HQ, s), BF16), jax.ShapeDtypeStruct((b, N_BRANCH * HQ, s), BF16)]
    specs = [_tok_spec(HQ), _tok_spec(N_BRANCH * HQ)]
    return shapes, specs


def _specs_for(arrs):
    return [_const_spec(a.shape) for a in arrs]


def _padded_q(q_ref, h):
    q = q_ref[0]
    qs = jnp.concatenate([q[g * HEAD_DIM:(g + 1) * HEAD_DIM] for g in range(GROUP)], axis=1)
    zeros = jnp.zeros_like(qs)
    odd = (h % 2) == 1
    return jnp.concatenate([jnp.where(odd, zeros, qs), jnp.where(odd, qs, zeros)], axis=0)


def _swa_kernel(q_ref, kp_ref, kc_ref, vp_ref, vc_ref, g_ref, bias_ref, sink_ref, o_ref):
    h = pl.program_id(1)
    qp = _padded_q(q_ref, h)
    k = jnp.concatenate([kp_ref[0], kc_ref[0]], axis=0)
    s = _dot(k, qp) + bias_ref[0, 0]
    sink = sink_ref[0]
    m = jnp.maximum(jnp.max(s, axis=0, keepdims=True), sink)
    e = jnp.exp(s - m)
    l = jnp.sum(e, axis=0, keepdims=True) + jnp.exp(sink - m)
    v = jnp.concatenate([vp_ref[0], vc_ref[0]], axis=1)
    o = _dot(v, e.astype(BF16)) / l
    for g in range(GROUP):
        rows = slice(g * HEAD_DIM, (g + 1) * HEAD_DIM)
        o_ref[0, rows, :] = (o[:, g * TQ:(g + 1) * TQ] * g_ref[0, rows, :].astype(F32)).astype(BF16)


def _swa_attention(qt, k, vt, gt, swa_tab, sink_row):
    b, _, s = qt.shape
    nq = s // TQ
    prev = lambda i: jnp.maximum(i - 1, 0)
    return pl.pallas_call(
        _swa_kernel, name="swa_attention", grid=(b, N_KV_HEADS, nq),
        in_specs=[
            pl.BlockSpec((1, GROUP * HEAD_DIM, TQ), lambda b_, h, i: (b_, h, i)),
            pl.BlockSpec((1, TQ, PAIR), lambda b_, h, i: (b_, prev(i), h // 2)),
            pl.BlockSpec((1, TQ, PAIR), lambda b_, h, i: (b_, i, h // 2)),
            pl.BlockSpec((1, HEAD_DIM, TQ), lambda b_, h, i: (b_, h, prev(i))),
            pl.BlockSpec((1, HEAD_DIM, TQ), lambda b_, h, i: (b_, h, i)),
            pl.BlockSpec((1, GROUP * HEAD_DIM, TQ), lambda b_, h, i: (b_, h, i)),
            pl.BlockSpec((1, 1, 2 * TQ, GL), lambda b_, h, i: (jnp.minimum(i, 1), h, 0, 0)),
            pl.BlockSpec((1, 1, GL), lambda b_, h, i: (h, 0, 0)),
        ],
        out_specs=pl.BlockSpec((1, GROUP * HEAD_DIM, TQ), lambda b_, h, i: (b_, h, i)),
        out_shape=jax.ShapeDtypeStruct((b, HQ, s), BF16),
        compiler_params=pltpu.CompilerParams(
            dimension_semantics=("parallel", "parallel", "arbitrary"), vmem_limit_bytes=VMEM_LIMIT),
    )(qt, k, k, vt, vt, gt, swa_tab, sink_row)


def _nsa_kernel(nq, q_ref, g_ref, kcmp_ref, vcmp_ref, kslc_ref, vslc_ref, kwin_ref, vwin_ref,
                cmpb_ref, diag_ref, prev_ref, far_ref, t31_ref, ovt_ref, o_ref,
                m_ref, l_ref, acc_ref, selb_ref, selfar_ref):
    h = pl.program_id(1)
    i = pl.program_id(2)
    qp = _padded_q(q_ref, h)
    t31 = t31_ref[0]

    start = pl.multiple_of((TQ // CMP_STRIDE) * (nq - 1 - i), 8)
    ncp = kcmp_ref.shape[2]
    bc = cmpb_ref[0, pl.ds(start, ncp), :]
    sc = _dot(kcmp_ref[0, 0], qp) + bc
    valid = bc > 0.5 * NEG
    mc = jnp.max(sc, axis=0, keepdims=True)
    ec = jnp.where(valid, jnp.exp(sc - mc), 0.0)
    lc = jnp.sum(ec, axis=0, keepdims=True)
    pc = ec / jnp.where(lc > 0.0, lc, 1.0)
    o_cmp = _dot(vcmp_ref[0, 0], pc.astype(BF16))

    psum = pc[:, 0:TQ]
    for g in range(1, GROUP):
        psum = psum + pc[:, g * TQ:(g + 1) * TQ]
    ovt = ovt_ref[...]
    p1 = psum.astype(BF16)
    r1 = psum - p1.astype(F32)
    p2 = r1.astype(BF16)
    p3 = (r1 - p2.astype(F32)).astype(BF16)
    imp = _dot(ovt, p1) + _dot(ovt, p2) + _dot(ovt, p3)
    nsel = imp.shape[0]
    pos = i * TQ + lax.broadcasted_iota(jnp.int32, (nsel, TQ), 1)
    blk = lax.broadcasted_iota(jnp.int32, (nsel, TQ), 0)
    causal = blk * SEL_BLOCK <= pos
    rel = pos // SEL_BLOCK - blk
    forced = (blk == 0) | ((rel >= 0) & (rel < SEL_FORCE_LOCAL))
    score = jnp.where(causal, imp + jnp.where(forced, FORCE_BONUS, 0.0), NEG)
    cnt = jnp.zeros((nsel, TQ), jnp.int32)
    for jp in range(nsel):
        row = score[jp:jp + 1, :]
        beats = (row > score) | ((row == score) & (blk > jp))
        cnt = cnt + beats.astype(jnp.int32)
    selb = jnp.where(cnt < min(SEL_TOP, nsel), 0.0, NEG)
    selb = jnp.concatenate([selb] * GROUP, axis=1)
    selb_ref[...] = selb
    selfar_ref[...] = selb + t31

    def chunk_scores(k_ref, c):
        kc = k_ref[0, pl.ds(pl.multiple_of(c * TQ, TQ), TQ), :]
        return _dot(kc, qp)

    def v_chunk(v_ref, c):
        return v_ref[0, :, pl.ds(pl.multiple_of(c * TQ, TQ), TQ)]

    def row_sel(ref, c):
        per = TQ // SEL_BLOCK
        parts = [jnp.broadcast_to(ref[pl.ds(per * c + u, 1), :], (SEL_BLOCK, GL)) for u in range(per)]
        return jnp.concatenate(parts, axis=0)

    def flash_first(s, v):
        m = jnp.max(s, axis=0, keepdims=True)
        e = jnp.exp(s - m)
        m_ref[...] = m
        l_ref[...] = jnp.sum(e, axis=0, keepdims=True)
        acc_ref[...] = _dot(v, e.astype(BF16))

    def flash_update(s, v):
        m_old = m_ref[...]
        m_new = jnp.maximum(m_old, jnp.max(s, axis=0, keepdims=True))
        alpha = jnp.exp(m_old - m_new)
        e = jnp.exp(s - m_new)
        l_ref[...] = alpha * l_ref[...] + jnp.sum(e, axis=0, keepdims=True)
        acc_ref[...] = alpha * acc_ref[...] + _dot(v, e.astype(BF16))
        m_ref[...] = m_new

    flash_first(chunk_scores(kslc_ref, i) + diag_ref[0] + row_sel(selb_ref, i), v_chunk(vslc_ref, i))

    @pl.when(i >= 1)
    def _():
        flash_update(chunk_scores(kslc_ref, i - 1) + prev_ref[0] + row_sel(selb_ref, i - 1),
                     v_chunk(vslc_ref, i - 1))

    def far_body(c, carry):
        flash_update(chunk_scores(kslc_ref, c) + row_sel(selfar_ref, c), v_chunk(vslc_ref, c))
        return carry

    lax.fori_loop(0, jnp.maximum(i - 1, 0), far_body, 0)
    o_slc = acc_ref[...] / l_ref[...]

    flash_first(chunk_scores(kwin_ref, i) + diag_ref[0], v_chunk(vwin_ref, i))

    @pl.when(i >= 1)
    def _():
        flash_update(chunk_scores(kwin_ref, i - 1) + prev_ref[0], v_chunk(vwin_ref, i - 1))

    for back in range(2, NSA_WINDOW // TQ):
        @pl.when(i >= back)
        def _(back=back):
            flash_update(chunk_scores(kwin_ref, i - back) + t31, v_chunk(vwin_ref, i - back))

    @pl.when(i >= NSA_WINDOW // TQ)
    def _():
        c = i - NSA_WINDOW // TQ
        flash_update(chunk_scores(kwin_ref, c) + far_ref[0], v_chunk(vwin_ref, c))

    o_win = acc_ref[...] / l_ref[...]

    for g in range(GROUP):
        rows = slice(g * HEAD_DIM, (g + 1) * HEAD_DIM)
        lanes = slice(g * TQ, (g + 1) * TQ)
        o = (o_cmp[:, lanes] * g_ref[0, 0, rows, :].astype(F32)
             + o_slc[:, lanes] * g_ref[0, 1, rows, :].astype(F32)
             + o_win[:, lanes] * g_ref[0, 2, rows, :].astype(F32))
        o_ref[0, rows, :] = o.astype(BF16)


def _nsa_attention(qt, gt, cmp_nat, cmp_t, kslc, vslc, kwin, vwin, tabs, ovt):
    b, _, s = qt.shape
    nq = s // TQ
    ncp = s // CMP_STRIDE
    nsel = s // SEL_BLOCK
    diag, prev, far, t31, cmpb = tabs
    g4 = gt.reshape(b, N_BRANCH, HQ, s)
    gd = GROUP * HEAD_DIM
    tab_spec = lambda rows: pl.BlockSpec((1, rows, GL), lambda b_, h, i: (h, 0, 0))
    return pl.pallas_call(
        functools.partial(_nsa_kernel, nq), name="nsa_attention", grid=(b, N_KV_HEADS, nq),
        in_specs=[
            pl.BlockSpec((1, gd, TQ), lambda b_, h, i: (b_, h, i)),
            pl.BlockSpec((1, N_BRANCH, gd, TQ), lambda b_, h, i: (b_, 0, h, i)),
            pl.BlockSpec((1, 1, ncp, PAIR), lambda b_, h, i: (0, b_, 0, h // 2)),
            pl.BlockSpec((1, 1, HEAD_DIM, ncp), lambda b_, h, i: (1, b_, h, 0)),
            pl.BlockSpec((1, s, PAIR), lambda b_, h, i: (b_, 0, h // 2)),
            pl.BlockSpec((1, HEAD_DIM, s), lambda b_, h, i: (b_, h, 0)),
            pl.BlockSpec((1, s, PAIR), lambda b_, h, i: (b_, 0, h // 2)),
            pl.BlockSpec((1, HEAD_DIM, s), lambda b_, h, i: (b_, h, 0)),
            tab_spec(cmpb.shape[1]), tab_spec(TQ), tab_spec(TQ), tab_spec(TQ), tab_spec(1),
            pl.BlockSpec((nsel, ncp), lambda b_, h, i: (0, 0)),
        ],
        out_specs=pl.BlockSpec((1, gd, TQ), lambda b_, h, i: (b_, h, i)),
        out_shape=jax.ShapeDtypeStruct((b, HQ, s), BF16),
        scratch_shapes=[pltpu.VMEM((1, GL), F32), pltpu.VMEM((1, GL), F32),
                        pltpu.VMEM((HEAD_DIM, GL), F32),
                        pltpu.VMEM((nsel, GL), F32), pltpu.VMEM((nsel, GL), F32)],
        compiler_params=pltpu.CompilerParams(
            dimension_semantics=("parallel", "parallel", "arbitrary"), vmem_limit_bytes=VMEM_LIMIT),
    )(qt, g4, cmp_nat, cmp_t, kslc, vslc, kwin, vwin, cmpb, diag, prev, far, t31, ovt)


def _compress_kernel(u_ref, ptop_ref, pbot_ref, wtop_ref, wbot_ref, w2_ref, kg_ref, nat_out, t_out):
    t = pl.program_id(0)
    u = u_ref[0, 0]
    a = _dot((u + ptop_ref[0]).astype(BF16), wtop_ref[0])
    bm = _dot((u + pbot_ref[0]).astype(BF16), wbot_ref[0])
    n = bm.shape[0]
    hid = a + pltpu.roll(bm, n - 1, axis=0)
    out_t = _dot(_silu(hid).astype(BF16), w2_ref[0]).T
    out_t = jnp.where(t == 0, _head_norm_t(out_t, kg_ref[...]), out_t)
    t_out[0, 0] = out_t.astype(BF16)
    nat_out[0, 0] = out_t.T.astype(BF16)


def _compress(craw, cmp_k_pos, cmp_k_w1, cmp_k_w2, cmp_v_pos, cmp_v_w1, cmp_v_w2, k_gain):
    b, _, s, _ = craw.shape
    nch = s // CMP_STRIDE
    wide = CMP_STRIDE * HKV
    u = craw.reshape(b, 2, nch, wide)
    eye = jnp.eye(N_KV_HEADS, dtype=F32)

    def big_w1(w1_half):
        w = w1_half.reshape(CMP_STRIDE, HEAD_DIM, CMP_HIDDEN)
        return jnp.einsum('jdn,hk->jhdkn', w, eye).reshape(wide, N_KV_HEADS * CMP_HIDDEN).astype(BF16)

    def pos_row(p_half):
        return jnp.broadcast_to(p_half[:, None, :], (CMP_STRIDE, N_KV_HEADS, HEAD_DIM)).reshape(1, wide)

    def big_w2(w2):
        return jnp.einsum('nd,hk->hnkd', w2, eye).reshape(N_KV_HEADS * CMP_HIDDEN, HKV).astype(BF16)

    half = CMP_STRIDE * HEAD_DIM
    wtop = jnp.stack([big_w1(cmp_k_w1[:half]), big_w1(cmp_v_w1[:half])])
    wbot = jnp.stack([big_w1(cmp_k_w1[half:]), big_w1(cmp_v_w1[half:])])
    ptop = jnp.stack([pos_row(cmp_k_pos[:CMP_STRIDE]), pos_row(cmp_v_pos[:CMP_STRIDE])])
    pbot = jnp.stack([pos_row(cmp_k_pos[CMP_STRIDE:]), pos_row(cmp_v_pos[CMP_STRIDE:])])
    w2 = jnp.stack([big_w2(cmp_k_w2), big_w2(cmp_v_w2)])
    kg = jnp.broadcast_to(k_gain.astype(F32)[:, None], (HEAD_DIM, nch))
    sel = lambda shape: pl.BlockSpec((1,) + shape, lambda t, b_: (t,) + (0,) * len(shape))
    return pl.pallas_call(
        _compress_kernel, name="compress", grid=(2, b),
        in_specs=[
            pl.BlockSpec((1, 1, nch, wide), lambda t, b_: (b_, t, 0, 0)),
            sel((1, wide)), sel((1, wide)),
            sel((wide, N_KV_HEADS * CMP_HIDDEN)), sel((wide, N_KV_HEADS * CMP_HIDDEN)),
            sel((N_KV_HEADS * CMP_HIDDEN, HKV)),
            pl.BlockSpec((HEAD_DIM, nch), lambda t, b_: (0, 0)),
        ],
        out_specs=[pl.BlockSpec((1, 1, nch, HKV), lambda t, b_: (t, b_, 0, 0)),
                   pl.BlockSpec((1, 1, HKV, nch), lambda t, b_: (t, b_, 0, 0))],
        out_shape=[jax.ShapeDtypeStruct((2, b, nch, HKV), BF16),
                   jax.ShapeDtypeStruct((2, b, HKV, nch), BF16)],
        compiler_params=pltpu.CompilerParams(
            dimension_semantics=("arbitrary", "arbitrary"), vmem_limit_bytes=VMEM_LIMIT),
    )(u, ptop, pbot, wtop, wbot, w2, kg)


def kernel(x, rel_table, a_norm, a_w_in, a_q_gain, a_k_gain, a_sink, a_w_out, kv_norm, kv_w,
           kv_k_gain, cmp_k_pos, cmp_k_w1, cmp_k_w2, cmp_v_pos, cmp_v_w1, cmp_v_w2,
           b_norm, b_w_in, b_q_gain, b_w_out):
    b, s, _ = x.shape
    nq = s // TQ
    n_a = a_w_in.shape[0]
    n_b = b_w_in.shape[0]
    table = rel_table.astype(F32)
    swa_tab, diag, prev, far, t31, cmpb = _make_tables(table, nq)
    ovt = _overlap_t(s)

    a_shapes, a_specs = _a_outs(b, s)
    b_shapes, b_specs = _b_outs(b, s)
    xt_shape, xt_spec = a_shapes[0], a_specs[0]

    ws, gains = _a_weights(a_w_in[0], a_q_gain[0], a_k_gain[0])
    consts = [_bcast(a_norm[0])] + ws + gains
    xt, qt, k, vt, gt = _proj_call(
        _first_kernel, "proj_first", b, s, [x] + consts,
        [_nat_spec(D_MODEL)] + _specs_for(consts), a_shapes, a_specs)

    for layer in range(n_a):
        sink_row = jnp.broadcast_to(
            a_sink[layer].astype(F32).reshape(N_KV_HEADS, 1, GROUP, 1), (N_KV_HEADS, 1, GROUP, TQ)
        ).reshape(N_KV_HEADS, 1, GL)
        ot = _swa_attention(qt, k, vt, gt, swa_tab, sink_row)
        wo = a_w_out[layer].T.astype(BF16)
        if layer + 1 < n_a:
            ws, gains = _a_weights(a_w_in[layer + 1], a_q_gain[layer + 1], a_k_gain[layer + 1])
            consts = [wo, _bcast(a_norm[layer + 1])] + ws + gains
            xt, qt, k, vt, gt = _proj_call(
                _a2a_kernel, "proj_a2a", b, s, [xt, ot] + consts,
                [_tok_spec(D_MODEL), _tok_spec(HQ)] + _specs_for(consts), a_shapes, a_specs)
        else:
            wsb, gb = _b_weights(b_w_in[0], b_q_gain[0])
            consts = ([wo, _bcast(kv_norm), kv_w.T.astype(BF16), _bcast(kv_k_gain[1]), _bcast(kv_k_gain[2]),
                       _bcast(b_norm[0])] + wsb + gb)
            kv_shapes = [jax.ShapeDtypeStruct((b, 2, s, HKV), F32),
                         jax.ShapeDtypeStruct((b, s, HKV), BF16), jax.ShapeDtypeStruct((b, HKV, s), BF16),
                         jax.ShapeDtypeStruct((b, s, HKV), BF16), jax.ShapeDtypeStruct((b, HKV, s), BF16)]
            kv_specs = [pl.BlockSpec((1, 2, TM, HKV), lambda b_, t: (b_, 0, t, 0)),
                        _nat_spec(HKV), _tok_spec(HKV), _nat_spec(HKV), _tok_spec(HKV)]
            xt, craw, kslc, vslc, kwin, vwin, qt, gt = _proj_call(
                _a2b_kernel, "proj_a2b", b, s, [xt, ot] + consts,
                [_tok_spec(D_MODEL), _tok_spec(HQ)] + _specs_for(consts),
                [xt_shape] + kv_shapes + b_shapes, [xt_spec] + kv_specs + b_specs)

    cmp_nat, cmp_t = _compress(craw, cmp_k_pos, cmp_k_w1, cmp_k_w2, cmp_v_pos, cmp_v_w1, cmp_v_w2,
                               kv_k_gain[0])
    tabs = (diag, prev, far, t31, cmpb)
    for layer in range(n_b):
        ot = _nsa_attention(qt, gt, cmp_nat, cmp_t, kslc, vslc, kwin, vwin, tabs, ovt)
        wo = b_w_out[layer].T.astype(BF16)
        if layer + 1 < n_b:
            wsb, gb = _b_weights(b_w_in[layer + 1], b_q_gain[layer + 1])
            consts = [wo, _bcast(b_norm[layer + 1])] + wsb + gb
            xt, qt, gt = _proj_call(
                _b2b_kernel, "proj_b2b", b, s, [xt, ot] + consts,
                [_tok_spec(D_MODEL), _tok_spec(HQ)] + _specs_for(consts),
                [xt_shape] + b_shapes, [xt_spec] + b_specs)
        else:
            out = _proj_call(
                _final_kernel, "proj_final", b, s, [xt, ot, wo],
                [_tok_spec(D_MODEL), _tok_spec(HQ), _const_spec(wo.shape)],
                jax.ShapeDtypeStruct((b, s, D_MODEL), F32), _nat_spec(D_MODEL))
    return out
```

```python
import functools
import math

import numpy as np
import jax
import jax.numpy as jnp
from jax import lax
from jax.experimental import pallas as pl
from jax.experimental.pallas import tpu as pltpu

D_MODEL = 1024
HEAD_DIM = 64
N_HEADS = 16
N_KV_HEADS = 4
GROUP = N_HEADS // N_KV_HEADS
HQ = N_HEADS * HEAD_DIM
HKV = N_KV_HEADS * HEAD_DIM
N_BRANCH = 3
SWA_WINDOW = 128
NSA_WINDOW = 512
CMP_LEN = 32
CMP_STRIDE = 16
CMP_HIDDEN = 256
SEL_BLOCK = 64
SEL_TOP = 8
SEL_FORCE_LOCAL = 2
NUM_BUCKETS = 32
MAX_DISTANCE = 128
EPS = 1e-6
NEG = -1e30
FORCE_BONUS = 1e6

TQ = 128
TM = 256
GL = GROUP * TQ
PAIR = 2 * HEAD_DIM
NEAR = 2 * TQ
WIN_FAR = NSA_WINDOW - TQ
N_WIN_VARIANTS = NSA_WINDOW // TQ + 1
FAR = 4 * TQ
PER_FAR = FAR // SEL_BLOCK
VMEM_LIMIT = 56 * 1024 * 1024

F32 = jnp.float32
BF16 = jnp.bfloat16


def _bucket_np(dist):
    d = np.maximum(dist, 0)
    max_exact = NUM_BUCKETS // 2
    ratio = (np.log(np.maximum(d, 1).astype(np.float32) / np.float32(max_exact))
             / np.float32(math.log(MAX_DISTANCE / max_exact))
             * np.float32(NUM_BUCKETS - max_exact))
    large = np.minimum(max_exact + ratio.astype(np.int32), NUM_BUCKETS - 1)
    return np.where(d < max_exact, d, large).astype(np.int32)


def _dist_vector(table, d_lo, length, hi_valid):
    d = d_lo + np.arange(length)
    v = table.T[:, _bucket_np(d)]
    return jnp.where(jnp.asarray((d >= 0) & (d < hi_valid))[None], v, NEG)


def _skew(v, n, step):
    length = v.shape[1]
    assert step * (n - 1) + TQ <= length
    width = length + step
    reps = -(-(n * width) // length)
    w = jnp.tile(v, (1, reps))[:, :n * width].reshape(v.shape[0], n, width)
    return w[:, :, :TQ]


def _toeplitz(table, n_rows, d_first, step, hi_valid):
    length = step * (n_rows - 1) + TQ
    v = _dist_vector(table, d_first - step * (n_rows - 1), length, hi_valid)
    t = _skew(v, n_rows, step)[:, ::-1, :]
    t = t.reshape(N_KV_HEADS, GROUP, n_rows, TQ).transpose(0, 2, 1, 3)
    return t.reshape(N_KV_HEADS, n_rows, GL).astype(F32)


def _mask_rows(t, n_masked):
    rows = np.arange(t.shape[1])[None, :, None] < n_masked
    return jnp.where(jnp.asarray(rows), NEG, t)


def _make_tables(table, nq):
    big = 1 << 30
    near = _toeplitz(table, NEAR, TQ, 1, big)
    near = jnp.stack([_mask_rows(near, TQ), near])
    swa = _toeplitz(table, NEAR, TQ, 1, SWA_WINDOW)
    swa = jnp.stack([_mask_rows(swa, TQ), swa])
    t31 = _toeplitz(table, 1, MAX_DISTANCE, 1, big)

    def saturated(rows):
        return jnp.broadcast_to(t31, (N_KV_HEADS, rows, GL))

    wfar = jnp.concatenate([_toeplitz(table, TQ, NSA_WINDOW, 1, NSA_WINDOW), saturated(WIN_FAR - TQ)], axis=1)
    wfar = jnp.stack([_mask_rows(wfar, min(WIN_FAR, (N_WIN_VARIANTS - 1 - v) * TQ))
                      for v in range(N_WIN_VARIANTS)])
    per = TQ // CMP_STRIDE
    ncp = nq * per
    off = per * (nq - 1)
    sat_c = -(-(MAX_DISTANCE + CMP_LEN - 1) // CMP_STRIDE)
    assert off >= sat_c
    band = _toeplitz(table, sat_c + per - 1, CMP_STRIDE * (sat_c - 1) - (CMP_LEN - 1), CMP_STRIDE, big)
    cmpb = jnp.concatenate(
        [saturated(off - sat_c + 1), band, jnp.full((N_KV_HEADS, ncp - per, GL), NEG, F32)], axis=1)
    return swa, near, wfar, t31, cmpb


def _overlap_t(s):
    ncp = s // CMP_STRIDE
    nsel = s // SEL_BLOCK
    cs = np.arange(ncp)[None, :] * CMP_STRIDE
    ss = np.arange(nsel)[:, None] * SEL_BLOCK
    ov = (cs < ss + SEL_BLOCK) & (cs + CMP_LEN > ss) & (np.arange(ncp)[None, :] < ncp - 1)
    return jnp.asarray(ov.astype(np.float32), dtype=BF16)


def _rms_t(xt, gain_b):
    ms = jnp.mean(xt * xt, axis=0, keepdims=True)
    return (xt * lax.rsqrt(ms + EPS) * gain_b).astype(BF16)


def _head_norm_t(a, gain_b):
    outs = []
    for h in range(a.shape[0] // HEAD_DIM):
        blk = a[h * HEAD_DIM:(h + 1) * HEAD_DIM]
        ms = jnp.mean(blk * blk, axis=0, keepdims=True)
        outs.append(blk * lax.rsqrt(ms + EPS) * gain_b)
    return jnp.concatenate(outs, axis=0)


def _sigmoid(z):
    return 1.0 / (1.0 + jnp.exp(-z))


def _silu(z):
    return z * _sigmoid(z)


def _dot(a, b):
    return jnp.dot(a, b, preferred_element_type=F32)


def _proj_a(xn, w, qg, kg, q_out, k_out, v_out, g_out):
    p = _dot(w[...], xn)
    q_out[0] = _head_norm_t(p[:HQ], qg[...]).astype(BF16)
    k_out[0] = _head_norm_t(p[HQ:HQ + HKV], kg[...]).T.astype(BF16)
    v_out[0] = p[HQ + HKV:HQ + 2 * HKV].astype(BF16)
    g_out[0] = _silu(p[HQ + 2 * HKV:]).astype(BF16)


def _proj_b(xn, w, qg, q_out, g_out):
    p = _dot(w[...], xn)
    q_out[0] = _head_norm_t(p[:HQ], qg[...]).astype(BF16)
    n_gate = N_BRANCH * N_HEADS
    sg = _sigmoid(p[HQ:HQ + n_gate])
    for r in range(n_gate):
        z = p[HQ + n_gate + r * HEAD_DIM:HQ + n_gate + (r + 1) * HEAD_DIM]
        g_out[0, r * HEAD_DIM:(r + 1) * HEAD_DIM, :] = (_silu(z) * sg[r:r + 1]).astype(BF16)


def _proj_kv(xn, wkv, kg1, kg2, craw_out, kslc_out, vslc_out, kwin_out, vwin_out):
    p = _dot(wkv[...], xn)

    def rows(n):
        return p[n * HKV:(n + 1) * HKV]
    craw_out[0, 0] = rows(0).T
    craw_out[0, 1] = rows(1).T
    kslc_out[0] = _head_norm_t(rows(2), kg1[...]).T.astype(BF16)
    vslc_out[0] = rows(3).astype(BF16)
    kwin_out[0] = _head_norm_t(rows(4), kg2[...]).T.astype(BF16)
    vwin_out[0] = rows(5).astype(BF16)


def _first_kernel(x_ref, ng, w, qg, kg, xt_out, q_out, k_out, v_out, g_out):
    xt = x_ref[0].T
    xt_out[0] = xt
    _proj_a(_rms_t(xt, ng[...]), w, qg, kg, q_out, k_out, v_out, g_out)


def _a2a_kernel(x_ref, o_ref, wo, ng, w, qg, kg, xt_out, q_out, k_out, v_out, g_out):
    xt = x_ref[0] + _dot(wo[...], o_ref[0])
    xt_out[0] = xt
    _proj_a(_rms_t(xt, ng[...]), w, qg, kg, q_out, k_out, v_out, g_out)


def _a2b_kernel(x_ref, o_ref, wo, ngkv, wkv, kg1, kg2, ngb, w, qg,
                xt_out, craw_out, kslc_out, vslc_out, kwin_out, vwin_out, q_out, g_out):
    xt = x_ref[0] + _dot(wo[...], o_ref[0])
    xt_out[0] = xt
    _proj_kv(_rms_t(xt, ngkv[...]), wkv, kg1, kg2, craw_out, kslc_out, vslc_out, kwin_out, vwin_out)
    _proj_b(_rms_t(xt, ngb[...]), w, qg, q_out, g_out)


def _b2b_kernel(x_ref, o_ref, wo, ngb, w, qg, xt_out, q_out, g_out):
    xt = x_ref[0] + _dot(wo[...], o_ref[0])
    xt_out[0] = xt
    _proj_b(_rms_t(xt, ngb[...]), w, qg, q_out, g_out)


def _final_kernel(x_ref, o_ref, wo, x_out):
    x_out[0] = (x_ref[0] + _dot(wo[...], o_ref[0])).T


def _tok_spec(rows):
    return pl.BlockSpec((1, rows, TM), lambda b, t: (b, 0, t))


def _nat_spec(cols):
    return pl.BlockSpec((1, TM, cols), lambda b, t: (b, t, 0))


def _const_spec(shape):
    nd = len(shape)
    return pl.BlockSpec(shape, lambda b, t: (0,) * nd)


def _proj_call(body, name, b, s, ins, in_specs, out_shapes, out_specs):
    return pl.pallas_call(
        body, name=name, grid=(b, s // TM),
        in_specs=in_specs, out_specs=out_specs, out_shape=out_shapes,
        compiler_params=pltpu.CompilerParams(
            dimension_semantics=("parallel", "parallel"), vmem_limit_bytes=VMEM_LIMIT),
    )(*ins)


def _bcast(v, scale=1.0):
    return jnp.broadcast_to((v.astype(F32) * scale)[:, None], (v.shape[0], TM))


def _a_weights(w_in, q_gain, k_gain):
    ws = [w_in.T.astype(BF16)]
    gains = [_bcast(q_gain, HEAD_DIM ** -0.5), _bcast(k_gain)]
    return ws, gains


def _b_weights(w_in, q_gain):
    ws = [w_in.T.astype(BF16)]
    return ws, [_bcast(q_gain, HEAD_DIM ** -0.5)]


def _a_outs(b, s):
    shapes = [jax.ShapeDtypeStruct((b, D_MODEL, s), F32), jax.ShapeDtypeStruct((b, HQ, s), BF16),
              jax.ShapeDtypeStruct((b, s, HKV), BF16), jax.ShapeDtypeStruct((b, HKV, s), BF16),
              jax.ShapeDtypeStruct((b, HQ, s), BF16)]
    specs = [_tok_spec(D_MODEL), _tok_spec(HQ), _nat_spec(HKV), _tok_spec(HKV), _tok_spec(HQ)]
    return shapes, specs


def _b_outs(b, s):
    shapes = [jax.ShapeDtypeStruct((b, HQ, s), BF16), jax.ShapeDtypeStruct((b, N_BRANCH * HQ, s), BF16)]
    specs = [_tok_spec(HQ), _tok_spec(N_BRANCH * HQ)]
    return shapes, specs


def _specs_for(arrs):
    return [_const_spec(a.shape) for a in arrs]


def _head_rows(h, g):
    return slice((h * GROUP + g) * HEAD_DIM, (h * GROUP + g + 1) * HEAD_DIM)


def _kv_rows(h):
    return slice(h * HEAD_DIM, (h + 1) * HEAD_DIM)


def _pair_cols(h):
    return slice((h // 2) * PAIR, (h // 2 + 1) * PAIR)


def _padded_q(q_ref, h):
    qs = jnp.concatenate([q_ref[0, _head_rows(h, g), :] for g in range(GROUP)], axis=1)
    zeros = jnp.zeros_like(qs)
    return jnp.concatenate([qs, zeros] if h % 2 == 0 else [zeros, qs], axis=0)


def _swa_kernel(q_ref, kp_ref, kc_ref, vp_ref, vc_ref, g_ref, bias_ref, sink_ref, o_ref):
    for h in range(N_KV_HEADS):
        qp = _padded_q(q_ref, h)
        k = jnp.concatenate([kp_ref[0, :, _pair_cols(h)], kc_ref[0, :, _pair_cols(h)]], axis=0)
        s = _dot(k, qp) + bias_ref[0, h]
        sink = sink_ref[h]
        m = jnp.maximum(jnp.max(s, axis=0, keepdims=True), sink)
        e = jnp.exp(s - m)
        l = jnp.sum(e, axis=0, keepdims=True) + jnp.exp(sink - m)
        v = jnp.concatenate([vp_ref[0, _kv_rows(h), :], vc_ref[0, _kv_rows(h), :]], axis=1)
        o = _dot(v, e.astype(BF16)) / l
        for g in range(GROUP):
            rows = _head_rows(h, g)
            o_ref[0, rows, :] = (o[:, g * TQ:(g + 1) * TQ] * g_ref[0, rows, :].astype(F32)).astype(BF16)


def _swa_attention(qt, k, vt, gt, swa_tab, sink_row):
    b, _, s = qt.shape
    nq = s // TQ
    prev = lambda i: jnp.maximum(i - 1, 0)
    return pl.pallas_call(
        _swa_kernel, name="swa_attention", grid=(b, nq),
        in_specs=[
            pl.BlockSpec((1, HQ, TQ), lambda b_, i: (b_, 0, i)),
            pl.BlockSpec((1, TQ, HKV), lambda b_, i: (b_, prev(i), 0)),
            pl.BlockSpec((1, TQ, HKV), lambda b_, i: (b_, i, 0)),
            pl.BlockSpec((1, HKV, TQ), lambda b_, i: (b_, 0, prev(i))),
            pl.BlockSpec((1, HKV, TQ), lambda b_, i: (b_, 0, i)),
            pl.BlockSpec((1, HQ, TQ), lambda b_, i: (b_, 0, i)),
            pl.BlockSpec((1, N_KV_HEADS, NEAR, GL), lambda b_, i: (jnp.minimum(i, 1), 0, 0, 0)),
            pl.BlockSpec((N_KV_HEADS, 1, GL), lambda b_, i: (0, 0, 0)),
        ],
        out_specs=pl.BlockSpec((1, HQ, TQ), lambda b_, i: (b_, 0, i)),
        out_shape=jax.ShapeDtypeStruct((b, HQ, s), BF16),
        compiler_params=pltpu.CompilerParams(
            dimension_semantics=("parallel", "arbitrary"), vmem_limit_bytes=VMEM_LIMIT),
    )(qt, k, k, vt, vt, gt, swa_tab, sink_row)


def _nsa_kernel(nq, q_ref, g_ref, kcmp_ref, vcmp_ref, kslc_ref, vslc_ref, kwin_ref, vwin_ref,
                cmpb_ref, near_ref, wfar_ref, t31_ref, ovt_ref, o_ref,
                qp_ref, m_ref, l_ref, acc_ref, selb_ref, selfar_ref, part_ref):
    i = pl.program_id(1)
    ncp = kcmp_ref.shape[2]
    nsel = ovt_ref.shape[0]
    cmp_start = pl.multiple_of((TQ // CMP_STRIDE) * (nq - 1 - i), 8)
    prev_rows = pl.ds(pl.multiple_of(jnp.maximum(i - 1, 0) * TQ, TQ), TQ)
    cur_rows = pl.ds(pl.multiple_of(i * TQ, TQ), TQ)
    pos = i * TQ + lax.broadcasted_iota(jnp.int32, (nsel, TQ), 1)
    blk = lax.broadcasted_iota(jnp.int32, (nsel, TQ), 0)
    causal = blk * SEL_BLOCK <= pos
    rel = pos // SEL_BLOCK - blk
    forced = (blk == 0) | ((rel >= 0) & (rel < SEL_FORCE_LOCAL))
    n_far_blocks = (TQ // SEL_BLOCK) * (i - 1)

    def near_keys(k_ref, h):
        return jnp.concatenate([k_ref[0, prev_rows, _pair_cols(h)], k_ref[0, cur_rows, _pair_cols(h)]], axis=0)

    def near_values(v_ref, h):
        return jnp.concatenate([v_ref[0, _kv_rows(h), prev_rows], v_ref[0, _kv_rows(h), cur_rows]], axis=1)

    for h in range(N_KV_HEADS):
        qp = _padded_q(q_ref, h)
        qp_ref[h] = qp
        t31 = t31_ref[h]

        bc = cmpb_ref[h, pl.ds(cmp_start, ncp), :]
        sc = _dot(kcmp_ref[0, 0, :, _pair_cols(h)], qp) + bc
        valid = bc > 0.5 * NEG
        mc = jnp.max(sc, axis=0, keepdims=True)
        ec = jnp.where(valid, jnp.exp(sc - mc), 0.0)
        lc = jnp.sum(ec, axis=0, keepdims=True)
        pc = ec / jnp.where(lc > 0.0, lc, 1.0)
        o_cmp = _dot(vcmp_ref[0, 0, _kv_rows(h), :], pc.astype(BF16))

        psum = pc[:, 0:TQ]
        for g in range(1, GROUP):
            psum = psum + pc[:, g * TQ:(g + 1) * TQ]
        ovt = ovt_ref[...]
        p1 = psum.astype(BF16)
        r1 = psum - p1.astype(F32)
        p2 = r1.astype(BF16)
        p3 = (r1 - p2.astype(F32)).astype(BF16)
        imp = _dot(ovt, p1) + _dot(ovt, p2) + _dot(ovt, p3)
        score = jnp.where(causal, imp + jnp.where(forced, FORCE_BONUS, 0.0), NEG)
        cnt = jnp.zeros((nsel, TQ), jnp.int32)
        for jp in range(nsel):
            row = score[jp:jp + 1, :]
            beats = (row > score) | ((row == score) & (blk > jp))
            cnt = cnt + beats.astype(jnp.int32)
        sel = cnt < min(SEL_TOP, nsel)
        selb_ref[h] = jnp.concatenate([jnp.where(sel, 0.0, NEG)] * GROUP, axis=1)
        selfar_ref[h] = jnp.concatenate(
            [jnp.where(sel & (blk < n_far_blocks), 0.0, NEG)] * GROUP, axis=1) + t31

        near_tab = near_ref[0, h]
        s = _dot(near_keys(kslc_ref, h), qp) + near_tab
        first_blk = (TQ // SEL_BLOCK) * (i - 1)
        parts = []
        for u in range(NEAR // SEL_BLOCK):
            row = selb_ref[h, pl.ds(jnp.maximum(first_blk + u, 0), 1), :]
            parts.append(s[u * SEL_BLOCK:(u + 1) * SEL_BLOCK] + row)
        s = jnp.concatenate(parts, axis=0)
        m = jnp.max(s, axis=0, keepdims=True)
        e = jnp.exp(s - m)
        m_ref[h] = m
        l_ref[h] = jnp.sum(e, axis=0, keepdims=True)
        acc_ref[h] = _dot(near_values(vslc_ref, h), e.astype(BF16))

        s_near = _dot(near_keys(kwin_ref, h), qp) + near_tab
        far_k = jnp.concatenate(
            [kwin_ref[0, pl.ds(pl.multiple_of(jnp.maximum(i - back, 0) * TQ, TQ), TQ), _pair_cols(h)]
             for back in range(NSA_WINDOW // TQ, 1, -1)], axis=0)
        far_v = jnp.concatenate(
            [vwin_ref[0, _kv_rows(h), pl.ds(pl.multiple_of(jnp.maximum(i - back, 0) * TQ, TQ), TQ)]
             for back in range(NSA_WINDOW // TQ, 1, -1)], axis=1)
        s_far = _dot(far_k, qp) + wfar_ref[0, h]
        mw = jnp.maximum(jnp.max(s_near, axis=0, keepdims=True), jnp.max(s_far, axis=0, keepdims=True))
        e_near = jnp.exp(s_near - mw)
        e_far = jnp.exp(s_far - mw)
        lw = jnp.sum(e_near, axis=0, keepdims=True) + jnp.sum(e_far, axis=0, keepdims=True)
        o_win = (_dot(near_values(vwin_ref, h), e_near.astype(BF16))
                 + _dot(far_v, e_far.astype(BF16))) / lw

        for g in range(GROUP):
            rows = _head_rows(h, g)
            lanes = slice(g * TQ, (g + 1) * TQ)
            part_ref[rows, :] = (o_cmp[:, lanes] * g_ref[0, 0, rows, :].astype(F32)
                                 + o_win[:, lanes] * g_ref[0, 2, rows, :].astype(F32))

    def far_body(it, carry):
        base = pl.multiple_of(it * FAR, FAR)
        for h in range(N_KV_HEADS):
            s = _dot(kslc_ref[0, pl.ds(base, FAR), _pair_cols(h)], qp_ref[h])
            parts = [s[u * SEL_BLOCK:(u + 1) * SEL_BLOCK] + selfar_ref[h, pl.ds(it * PER_FAR + u, 1), :]
                     for u in range(PER_FAR)]
            s = jnp.concatenate(parts, axis=0)
            m_old = m_ref[h]
            m_new = jnp.maximum(m_old, jnp.max(s, axis=0, keepdims=True))
            alpha = jnp.exp(m_old - m_new)
            e = jnp.exp(s - m_new)
            l_ref[h] = alpha * l_ref[h] + jnp.sum(e, axis=0, keepdims=True)
            acc_ref[h] = alpha * acc_ref[h] + _dot(vslc_ref[0, _kv_rows(h), pl.ds(base, FAR)], e.astype(BF16))
            m_ref[h] = m_new
        return carry

    n_far_chunks = jnp.maximum(i - 1, 0)
    lax.fori_loop(0, (n_far_chunks + FAR // TQ - 1) // (FAR // TQ), far_body, 0)

    for h in range(N_KV_HEADS):
        o_slc = acc_ref[h] / l_ref[h]
        for g in range(GROUP):
            rows = _head_rows(h, g)
            o = part_ref[rows, :] + o_slc[:, g * TQ:(g + 1) * TQ] * g_ref[0, 1, rows, :].astype(F32)
            o_ref[0, rows, :] = o.astype(BF16)


def _nsa_attention(qt, gt, cmp_nat, cmp_t, kslc, vslc, kwin, vwin, tabs, ovt):
    b, _, s = qt.shape
    nq = s // TQ
    assert s % FAR == 0
    ncp = s // CMP_STRIDE
    nsel = s // SEL_BLOCK
    near, wfar, t31, cmpb = tabs
    g4 = gt.reshape(b, N_BRANCH, HQ, s)
    full = lambda shape: pl.BlockSpec(shape, lambda b_, i: (0,) * len(shape))
    return pl.pallas_call(
        functools.partial(_nsa_kernel, nq), name="nsa_attention", grid=(b, nq),
        in_specs=[
            pl.BlockSpec((1, HQ, TQ), lambda b_, i: (b_, 0, i)),
            pl.BlockSpec((1, N_BRANCH, HQ, TQ), lambda b_, i: (b_, 0, 0, i)),
            pl.BlockSpec((1, 1, ncp, HKV), lambda b_, i: (0, b_, 0, 0)),
            pl.BlockSpec((1, 1, HKV, ncp), lambda b_, i: (1, b_, 0, 0)),
            pl.BlockSpec((1, s, HKV), lambda b_, i: (b_, 0, 0)),
            pl.BlockSpec((1, HKV, s), lambda b_, i: (b_, 0, 0)),
            pl.BlockSpec((1, s, HKV), lambda b_, i: (b_, 0, 0)),
            pl.BlockSpec((1, HKV, s), lambda b_, i: (b_, 0, 0)),
            full(cmpb.shape),
            pl.BlockSpec((1, N_KV_HEADS, NEAR, GL), lambda b_, i: (jnp.minimum(i, 1), 0, 0, 0)),
            pl.BlockSpec((1, N_KV_HEADS, WIN_FAR, GL),
                         lambda b_, i: (jnp.minimum(i, N_WIN_VARIANTS - 1), 0, 0, 0)),
            full(t31.shape),
            full((nsel, ncp)),
        ],
        out_specs=pl.BlockSpec((1, HQ, TQ), lambda b_, i: (b_, 0, i)),
        out_shape=jax.ShapeDtypeStruct((b, HQ, s), BF16),
        scratch_shapes=[pltpu.VMEM((N_KV_HEADS, PAIR, GL), BF16),
                        pltpu.VMEM((N_KV_HEADS, 1, GL), F32), pltpu.VMEM((N_KV_HEADS, 1, GL), F32),
                        pltpu.VMEM((N_KV_HEADS, HEAD_DIM, GL), F32),
                        pltpu.VMEM((N_KV_HEADS, nsel, GL), F32), pltpu.VMEM((N_KV_HEADS, nsel, GL), F32),
                        pltpu.VMEM((HQ, TQ), F32)],
        compiler_params=pltpu.CompilerParams(
            dimension_semantics=("parallel", "arbitrary"), vmem_limit_bytes=VMEM_LIMIT),
    )(qt, g4, cmp_nat, cmp_t, kslc, vslc, kwin, vwin, cmpb, near, wfar, t31, ovt)


def _compress_kernel(u_ref, ptop_ref, pbot_ref, wtop_ref, wbot_ref, w2_ref, kg_ref, nat_out, t_out):
    t = pl.program_id(0)
    u = u_ref[0, 0]
    a = _dot((u + ptop_ref[0]).astype(BF16), wtop_ref[0])
    bm = _dot((u + pbot_ref[0]).astype(BF16), wbot_ref[0])
    n = bm.shape[0]
    hid = a + pltpu.roll(bm, n - 1, axis=0)
    out_t = _dot(_silu(hid).astype(BF16), w2_ref[0]).T
    out_t = jnp.where(t == 0, _head_norm_t(out_t, kg_ref[...]), out_t)
    t_out[0, 0] = out_t.astype(BF16)
    nat_out[0, 0] = out_t.T.astype(BF16)


def _compress(craw, cmp_k_pos, cmp_k_w1, cmp_k_w2, cmp_v_pos, cmp_v_w1, cmp_v_w2, k_gain):
    b, _, s, _ = craw.shape
    nch = s // CMP_STRIDE
    wide = CMP_STRIDE * HKV
    u = craw.reshape(b, 2, nch, wide)
    eye = jnp.eye(N_KV_HEADS, dtype=F32)

    def big_w1(w1_half):
        w = w1_half.reshape(CMP_STRIDE, HEAD_DIM, CMP_HIDDEN)
        return jnp.einsum('jdn,hk->jhdkn', w, eye).reshape(wide, N_KV_HEADS * CMP_HIDDEN).astype(BF16)

    def pos_row(p_half):
        return jnp.broadcast_to(p_half[:, None, :], (CMP_STRIDE, N_KV_HEADS, HEAD_DIM)).reshape(1, wide)

    def big_w2(w2):
        return jnp.einsum('nd,hk->hnkd', w2, eye).reshape(N_KV_HEADS * CMP_HIDDEN, HKV).astype(BF16)

    half = CMP_STRIDE * HEAD_DIM
    wtop = jnp.stack([big_w1(cmp_k_w1[:half]), big_w1(cmp_v_w1[:half])])
    wbot = jnp.stack([big_w1(cmp_k_w1[half:]), big_w1(cmp_v_w1[half:])])
    ptop = jnp.stack([pos_row(cmp_k_pos[:CMP_STRIDE]), pos_row(cmp_v_pos[:CMP_STRIDE])])
    pbot = jnp.stack([pos_row(cmp_k_pos[CMP_STRIDE:]), pos_row(cmp_v_pos[CMP_STRIDE:])])
    w2 = jnp.stack([big_w2(cmp_k_w2), big_w2(cmp_v_w2)])
    kg = jnp.broadcast_to(k_gain.astype(F32)[:, None], (HEAD_DIM, nch))
    sel = lambda shape: pl.BlockSpec((1,) + shape, lambda t, b_: (t,) + (0,) * len(shape))
    return pl.pallas_call(
        _compress_kernel, name="compress", grid=(2, b),
        in_specs=[
            pl.BlockSpec((1, 1, nch, wide), lambda t, b_: (b_, t, 0, 0)),
            sel((1, wide)), sel((1, wide)),
            sel((wide, N_KV_HEADS * CMP_HIDDEN)), sel((wide, N_KV_HEADS * CMP_HIDDEN)),
            sel((N_KV_HEADS * CMP_HIDDEN, HKV)),
            pl.BlockSpec((HEAD_DIM, nch), lambda t, b_: (0, 0)),
        ],
        out_specs=[pl.BlockSpec((1, 1, nch, HKV), lambda t, b_: (t, b_, 0, 0)),
                   pl.BlockSpec((1, 1, HKV, nch), lambda t, b_: (t, b_, 0, 0))],
        out_shape=[jax.ShapeDtypeStruct((2, b, nch, HKV), BF16),
                   jax.ShapeDtypeStruct((2, b, HKV, nch), BF16)],
        compiler_params=pltpu.CompilerParams(
            dimension_semantics=("arbitrary", "arbitrary"), vmem_limit_bytes=VMEM_LIMIT),
    )(u, ptop, pbot, wtop, wbot, w2, kg)


def kernel(x, rel_table, a_norm, a_w_in, a_q_gain, a_k_gain, a_sink, a_w_out, kv_norm, kv_w,
           kv_k_gain, cmp_k_pos, cmp_k_w1, cmp_k_w2, cmp_v_pos, cmp_v_w1, cmp_v_w2,
           b_norm, b_w_in, b_q_gain, b_w_out):
    b, s, _ = x.shape
    nq = s // TQ
    n_a = a_w_in.shape[0]
    n_b = b_w_in.shape[0]
    swa_tab, near, wfar, t31, cmpb = _make_tables(rel_table.astype(F32), nq)
    ovt = _overlap_t(s)

    a_shapes, a_specs = _a_outs(b, s)
    b_shapes, b_specs = _b_outs(b, s)
    xt_shape, xt_spec = a_shapes[0], a_specs[0]

    ws, gains = _a_weights(a_w_in[0], a_q_gain[0], a_k_gain[0])
    consts = [_bcast(a_norm[0])] + ws + gains
    xt, qt, k, vt, gt = _proj_call(
        _first_kernel, "proj_first", b, s, [x] + consts,
        [_nat_spec(D_MODEL)] + _specs_for(consts), a_shapes, a_specs)

    for layer in range(n_a):
        sink_row = jnp.broadcast_to(
            a_sink[layer].astype(F32).reshape(N_KV_HEADS, 1, GROUP, 1), (N_KV_HEADS, 1, GROUP, TQ)
        ).reshape(N_KV_HEADS, 1, GL)
        ot = _swa_attention(qt, k, vt, gt, swa_tab, sink_row)
        wo = a_w_out[layer].T.astype(BF16)
        if layer + 1 < n_a:
            ws, gains = _a_weights(a_w_in[layer + 1], a_q_gain[layer + 1], a_k_gain[layer + 1])
            consts = [wo, _bcast(a_norm[layer + 1])] + ws + gains
            xt, qt, k, vt, gt = _proj_call(
                _a2a_kernel, "proj_a2a", b, s, [xt, ot] + consts,
                [_tok_spec(D_MODEL), _tok_spec(HQ)] + _specs_for(consts), a_shapes, a_specs)
        else:
            wsb, gb = _b_weights(b_w_in[0], b_q_gain[0])
            consts = ([wo, _bcast(kv_norm), kv_w.T.astype(BF16), _bcast(kv_k_gain[1]), _bcast(kv_k_gain[2]),
                       _bcast(b_norm[0])] + wsb + gb)
            kv_shapes = [jax.ShapeDtypeStruct((b, 2, s, HKV), F32),
                         jax.ShapeDtypeStruct((b, s, HKV), BF16), jax.ShapeDtypeStruct((b, HKV, s), BF16),
                         jax.ShapeDtypeStruct((b, s, HKV), BF16), jax.ShapeDtypeStruct((b, HKV, s), BF16)]
            kv_specs = [pl.BlockSpec((1, 2, TM, HKV), lambda b_, t: (b_, 0, t, 0)),
                        _nat_spec(HKV), _tok_spec(HKV), _nat_spec(HKV), _tok_spec(HKV)]
            xt, craw, kslc, vslc, kwin, vwin, qt, gt = _proj_call(
                _a2b_kernel, "proj_a2b", b, s, [xt, ot] + consts,
                [_tok_spec(D_MODEL), _tok_spec(HQ)] + _specs_for(consts),
                [xt_shape] + kv_shapes + b_shapes, [xt_spec] + kv_specs + b_specs)

    cmp_nat, cmp_t = _compress(craw, cmp_k_pos, cmp_k_w1, cmp_k_w2, cmp_v_pos, cmp_v_w1, cmp_v_w2,
                               kv_k_gain[0])
    tabs = (near, wfar, t31, cmpb)
    for layer in range(n_b):
        ot = _nsa_attention(qt, gt, cmp_nat, cmp_t, kslc, vslc, kwin, vwin, tabs, ovt)
        wo = b_w_out[layer].T.astype(BF16)
        if layer + 1 < n_b:
            wsb, gb = _b_weights(b_w_in[layer + 1], b_q_gain[layer + 1])
            consts = [wo, _bcast(b_norm[layer + 1])] + wsb + gb
            xt, qt, gt = _proj_call(
                _b2b_kernel, "proj_b2b", b, s, [xt, ot] + consts,
                [_tok_spec(D_MODEL), _tok_spec(HQ)] + _specs_for(consts),
                [xt_shape] + b_shapes, [xt_spec] + b_specs)
        else:
            out = _proj_call(
                _final_kernel, "proj_final", b, s, [xt, ot, wo],
                [_tok_spec(D_MODEL), _tok_spec(HQ), _const_spec(wo.shape)],
                jax.ShapeDtypeStruct((b, s, D_MODEL), F32), _nat_spec(D_MODEL))
    return out
```

```python
import functools
import math

import numpy as np
import jax
import jax.numpy as jnp
from jax import lax
from jax.experimental import pallas as pl
from jax.experimental.pallas import tpu as pltpu

D_MODEL = 1024
HEAD_DIM = 64
N_HEADS = 16
N_KV_HEADS = 4
GROUP = N_HEADS // N_KV_HEADS
HQ = N_HEADS * HEAD_DIM
HKV = N_KV_HEADS * HEAD_DIM
N_BRANCH = 3
SWA_WINDOW = 128
NSA_WINDOW = 512
CMP_LEN = 32
CMP_STRIDE = 16
CMP_HIDDEN = 256
SEL_BLOCK = 64
SEL_TOP = 8
SEL_FORCE_LOCAL = 2
NUM_BUCKETS = 32
MAX_DISTANCE = 128
EPS = 1e-6
NEG = -1e30
FORCE_BONUS = 1e6

TQ = 128
TM = 256
GL = GROUP * TQ
PAIR = 2 * HEAD_DIM
NEAR = 2 * TQ
WIN_FAR = NSA_WINDOW - TQ
N_WIN_VARIANTS = NSA_WINDOW // TQ + 1
FAR = 4 * TQ
PER_FAR = FAR // SEL_BLOCK
ONES_ROWS = 16
VMEM_LIMIT = 56 * 1024 * 1024

F32 = jnp.float32
BF16 = jnp.bfloat16


def _bucket_np(dist):
    d = np.maximum(dist, 0)
    max_exact = NUM_BUCKETS // 2
    ratio = (np.log(np.maximum(d, 1).astype(np.float32) / np.float32(max_exact))
             / np.float32(math.log(MAX_DISTANCE / max_exact))
             * np.float32(NUM_BUCKETS - max_exact))
    large = np.minimum(max_exact + ratio.astype(np.int32), NUM_BUCKETS - 1)
    return np.where(d < max_exact, d, large).astype(np.int32)


def _dist_vector(table, d_lo, length, hi_valid):
    d = d_lo + np.arange(length)
    v = table.T[:, _bucket_np(d)]
    return jnp.where(jnp.asarray((d >= 0) & (d < hi_valid))[None], v, NEG)


def _skew(v, n, step):
    length = v.shape[1]
    assert step * (n - 1) + TQ <= length
    width = length + step
    reps = -(-(n * width) // length)
    w = jnp.tile(v, (1, reps))[:, :n * width].reshape(v.shape[0], n, width)
    return w[:, :, :TQ]


def _toeplitz(table, n_rows, d_first, step, hi_valid):
    length = step * (n_rows - 1) + TQ
    v = _dist_vector(table, d_first - step * (n_rows - 1), length, hi_valid)
    t = _skew(v, n_rows, step)[:, ::-1, :]
    t = t.reshape(N_KV_HEADS, GROUP, n_rows, TQ).transpose(0, 2, 1, 3)
    return t.reshape(N_KV_HEADS, n_rows, GL).astype(F32)


def _mask_rows(t, n_masked):
    rows = np.arange(t.shape[1])[None, :, None] < n_masked
    return jnp.where(jnp.asarray(rows), NEG, t)


def _make_tables(table, nq):
    big = 1 << 30
    near = _toeplitz(table, NEAR, TQ, 1, big)
    near = jnp.stack([_mask_rows(near, TQ), near])
    swa = _toeplitz(table, NEAR, TQ, 1, SWA_WINDOW)
    swa = jnp.stack([_mask_rows(swa, TQ), swa])
    t31 = _toeplitz(table, 1, MAX_DISTANCE, 1, big)

    def saturated(rows):
        return jnp.broadcast_to(t31, (N_KV_HEADS, rows, GL))

    wfar = jnp.concatenate([_toeplitz(table, TQ, NSA_WINDOW, 1, NSA_WINDOW), saturated(WIN_FAR - TQ)], axis=1)
    wfar = jnp.stack([_mask_rows(wfar, min(WIN_FAR, (N_WIN_VARIANTS - 1 - v) * TQ))
                      for v in range(N_WIN_VARIANTS)])
    per = TQ // CMP_STRIDE
    ncp = nq * per
    off = per * (nq - 1)
    sat_c = -(-(MAX_DISTANCE + CMP_LEN - 1) // CMP_STRIDE)
    assert off >= sat_c
    band = _toeplitz(table, sat_c + per - 1, CMP_STRIDE * (sat_c - 1) - (CMP_LEN - 1), CMP_STRIDE, big)
    cmpb = jnp.concatenate(
        [saturated(off - sat_c + 1), band, jnp.full((N_KV_HEADS, ncp - per, GL), NEG, F32)], axis=1)
    return swa, near, wfar, t31, cmpb


def _overlap_t(s):
    ncp = s // CMP_STRIDE
    nsel = s // SEL_BLOCK
    cs = np.arange(ncp)[None, :] * CMP_STRIDE
    ss = np.arange(nsel)[:, None] * SEL_BLOCK
    ov = (cs < ss + SEL_BLOCK) & (cs + CMP_LEN > ss) & (np.arange(ncp)[None, :] < ncp - 1)
    return jnp.asarray(ov.astype(np.float32), dtype=BF16)


def _rms_t(xt, gain_b):
    ms = jnp.mean(xt * xt, axis=0, keepdims=True)
    return (xt * lax.rsqrt(ms + EPS) * gain_b).astype(BF16)


def _head_norm_t(a, gain_b):
    outs = []
    for h in range(a.shape[0] // HEAD_DIM):
        blk = a[h * HEAD_DIM:(h + 1) * HEAD_DIM]
        ms = jnp.mean(blk * blk, axis=0, keepdims=True)
        outs.append(blk * lax.rsqrt(ms + EPS) * gain_b)
    return jnp.concatenate(outs, axis=0)


def _sigmoid(z):
    return 1.0 / (1.0 + jnp.exp(-z))


def _silu(z):
    return z * _sigmoid(z)


def _dot(a, b):
    return jnp.dot(a, b, preferred_element_type=F32)


def _proj_a(xn, w, qg, kg, q_out, k_out, v_out, g_out):
    p = _dot(w[...], xn)
    q_out[0] = _head_norm_t(p[:HQ], qg[...]).astype(BF16)
    k_out[0] = _head_norm_t(p[HQ:HQ + HKV], kg[...]).T.astype(BF16)
    v_out[0] = p[HQ + HKV:HQ + 2 * HKV].astype(BF16)
    g_out[0] = _silu(p[HQ + 2 * HKV:]).astype(BF16)


def _proj_b(xn, w, qg, q_out, g_out):
    p = _dot(w[...], xn)
    q_out[0] = _head_norm_t(p[:HQ], qg[...]).astype(BF16)
    n_gate = N_BRANCH * N_HEADS
    sg = _sigmoid(p[HQ:HQ + n_gate])
    for r in range(n_gate):
        z = p[HQ + n_gate + r * HEAD_DIM:HQ + n_gate + (r + 1) * HEAD_DIM]
        g_out[0, r * HEAD_DIM:(r + 1) * HEAD_DIM, :] = (_silu(z) * sg[r:r + 1]).astype(BF16)


def _proj_kv(xn, wkv, kg1, kg2, craw_out, kslc_out, vslc_out, kwin_out, vwin_out):
    p = _dot(wkv[...], xn)

    def rows(n):
        return p[n * HKV:(n + 1) * HKV]
    craw_out[0, 0] = rows(0).T
    craw_out[0, 1] = rows(1).T
    kslc_out[0] = _head_norm_t(rows(2), kg1[...]).T.astype(BF16)
    vslc_out[0] = rows(3).astype(BF16)
    kwin_out[0] = _head_norm_t(rows(4), kg2[...]).T.astype(BF16)
    vwin_out[0] = rows(5).astype(BF16)


def _first_kernel(x_ref, ng, w, qg, kg, xt_out, q_out, k_out, v_out, g_out):
    xt = x_ref[0].T
    xt_out[0] = xt
    _proj_a(_rms_t(xt, ng[...]), w, qg, kg, q_out, k_out, v_out, g_out)


def _a2a_kernel(x_ref, o_ref, wo, ng, w, qg, kg, xt_out, q_out, k_out, v_out, g_out):
    xt = x_ref[0] + _dot(wo[...], o_ref[0])
    xt_out[0] = xt
    _proj_a(_rms_t(xt, ng[...]), w, qg, kg, q_out, k_out, v_out, g_out)


def _a2b_kernel(x_ref, o_ref, wo, ngkv, wkv, kg1, kg2, ngb, w, qg,
                xt_out, craw_out, kslc_out, vslc_out, kwin_out, vwin_out, q_out, g_out):
    xt = x_ref[0] + _dot(wo[...], o_ref[0])
    xt_out[0] = xt
    _proj_kv(_rms_t(xt, ngkv[...]), wkv, kg1, kg2, craw_out, kslc_out, vslc_out, kwin_out, vwin_out)
    _proj_b(_rms_t(xt, ngb[...]), w, qg, q_out, g_out)


def _b2b_kernel(x_ref, o_ref, wo, ngb, w, qg, xt_out, q_out, g_out):
    xt = x_ref[0] + _dot(wo[...], o_ref[0])
    xt_out[0] = xt
    _proj_b(_rms_t(xt, ngb[...]), w, qg, q_out, g_out)


def _final_kernel(x_ref, o_ref, wo, x_out):
    x_out[0] = (x_ref[0] + _dot(wo[...], o_ref[0])).T


def _tok_spec(rows):
    return pl.BlockSpec((1, rows, TM), lambda b, t: (b, 0, t))


def _nat_spec(cols):
    return pl.BlockSpec((1, TM, cols), lambda b, t: (b, t, 0))


def _const_spec(shape):
    nd = len(shape)
    return pl.BlockSpec(shape, lambda b, t: (0,) * nd)


def _proj_call(body, name, b, s, ins, in_specs, out_shapes, out_specs):
    return pl.pallas_call(
        body, name=name, grid=(b, s // TM),
        in_specs=in_specs, out_specs=out_specs, out_shape=out_shapes,
        compiler_params=pltpu.CompilerParams(
            dimension_semantics=("parallel", "parallel"), vmem_limit_bytes=VMEM_LIMIT),
    )(*ins)


def _bcast(v, scale=1.0):
    return jnp.broadcast_to((v.astype(F32) * scale)[:, None], (v.shape[0], TM))


def _a_weights(w_in, q_gain, k_gain):
    ws = [w_in.T.astype(BF16)]
    gains = [_bcast(q_gain, HEAD_DIM ** -0.5), _bcast(k_gain)]
    return ws, gains


def _b_weights(w_in, q_gain):
    ws = [w_in.T.astype(BF16)]
    return ws, [_bcast(q_gain, HEAD_DIM ** -0.5)]


def _a_outs(b, s):
    shapes = [jax.ShapeDtypeStruct((b, D_MODEL, s), F32), jax.ShapeDtypeStruct((b, HQ, s), BF16),
              jax.ShapeDtypeStruct((b, s, HKV), BF16), jax.ShapeDtypeStruct((b, HKV, s), BF16),
              jax.ShapeDtypeStruct((b, HQ, s), BF16)]
    specs = [_tok_spec(D_MODEL), _tok_spec(HQ), _nat_spec(HKV), _tok_spec(HKV), _tok_spec(HQ)]
    return shapes, specs


def _b_outs(b, s):
    shapes = [jax.ShapeDtypeStruct((b, HQ, s), BF16), jax.ShapeDtypeStruct((b, N_BRANCH * HQ, s), BF16)]
    specs = [_tok_spec(HQ), _tok_spec(N_BRANCH * HQ)]
    return shapes, specs


def _specs_for(arrs):
    return [_const_spec(a.shape) for a in arrs]


def _head_rows(h, g):
    return slice((h * GROUP + g) * HEAD_DIM, (h * GROUP + g + 1) * HEAD_DIM)


def _kv_rows(h):
    return slice(h * HEAD_DIM, (h + 1) * HEAD_DIM)


def _pair_cols(h):
    return slice((h // 2) * PAIR, (h // 2 + 1) * PAIR)


def _padded_q(q_ref, h):
    qs = jnp.concatenate([q_ref[0, _head_rows(h, g), :] for g in range(GROUP)], axis=1)
    zeros = jnp.zeros_like(qs)
    return jnp.concatenate([qs, zeros] if h % 2 == 0 else [zeros, qs], axis=0)


def _with_ones(v):
    return jnp.concatenate([v, jnp.ones((ONES_ROWS, v.shape[1]), BF16)], axis=0)


def _exp_bf16(x):
    return jnp.exp(x.astype(BF16))


def _swa_kernel(q_ref, kp_ref, kc_ref, vp_ref, vc_ref, g_ref, bias_ref, sink_ref, o_ref):
    for h in range(N_KV_HEADS):
        qp = _padded_q(q_ref, h)
        k = jnp.concatenate([kp_ref[0, :, _pair_cols(h)], kc_ref[0, :, _pair_cols(h)]], axis=0)
        s = _dot(k, qp) + bias_ref[0, h]
        sink = sink_ref[h]
        m = jnp.maximum(jnp.max(s, axis=0, keepdims=True), sink)
        v = jnp.concatenate([vp_ref[0, _kv_rows(h), :], vc_ref[0, _kv_rows(h), :]], axis=1)
        acc = _dot(_with_ones(v), _exp_bf16(s - m))
        o = acc[:HEAD_DIM] / (acc[HEAD_DIM:HEAD_DIM + 1] + jnp.exp(sink - m))
        for g in range(GROUP):
            rows = _head_rows(h, g)
            o_ref[0, rows, :] = (o[:, g * TQ:(g + 1) * TQ] * g_ref[0, rows, :].astype(F32)).astype(BF16)


def _swa_attention(qt, k, vt, gt, swa_tab, sink_row):
    b, _, s = qt.shape
    nq = s // TQ
    prev = lambda i: jnp.maximum(i - 1, 0)
    return pl.pallas_call(
        _swa_kernel, name="swa_attention", grid=(b, nq),
        in_specs=[
            pl.BlockSpec((1, HQ, TQ), lambda b_, i: (b_, 0, i)),
            pl.BlockSpec((1, TQ, HKV), lambda b_, i: (b_, prev(i), 0)),
            pl.BlockSpec((1, TQ, HKV), lambda b_, i: (b_, i, 0)),
            pl.BlockSpec((1, HKV, TQ), lambda b_, i: (b_, 0, prev(i))),
            pl.BlockSpec((1, HKV, TQ), lambda b_, i: (b_, 0, i)),
            pl.BlockSpec((1, HQ, TQ), lambda b_, i: (b_, 0, i)),
            pl.BlockSpec((1, N_KV_HEADS, NEAR, GL), lambda b_, i: (jnp.minimum(i, 1), 0, 0, 0)),
            pl.BlockSpec((N_KV_HEADS, 1, GL), lambda b_, i: (0, 0, 0)),
        ],
        out_specs=pl.BlockSpec((1, HQ, TQ), lambda b_, i: (b_, 0, i)),
        out_shape=jax.ShapeDtypeStruct((b, HQ, s), BF16),
        compiler_params=pltpu.CompilerParams(
            dimension_semantics=("parallel", "arbitrary"), vmem_limit_bytes=VMEM_LIMIT),
    )(qt, k, k, vt, vt, gt, swa_tab, sink_row)


def _nsa_kernel(nq, q_ref, g_ref, kcmp_ref, vcmp_ref, kslc_ref, vslc_ref, kwin_ref, vwin_ref,
                cmpb_ref, near_ref, wfar_ref, t31_ref, ovt_ref, o_ref,
                qp_ref, m_ref, acc_ref, selb_ref, selfar_ref, part_ref):
    i = pl.program_id(1)
    ncp = kcmp_ref.shape[2]
    nsel = ovt_ref.shape[0]
    cmp_start = pl.multiple_of((TQ // CMP_STRIDE) * (nq - 1 - i), 8)
    prev_rows = pl.ds(pl.multiple_of(jnp.maximum(i - 1, 0) * TQ, TQ), TQ)
    cur_rows = pl.ds(pl.multiple_of(i * TQ, TQ), TQ)
    pos = i * TQ + lax.broadcasted_iota(jnp.int32, (nsel, TQ), 1)
    blk = lax.broadcasted_iota(jnp.int32, (nsel, TQ), 0)
    causal = blk * SEL_BLOCK <= pos
    rel = pos // SEL_BLOCK - blk
    forced = (blk == 0) | ((rel >= 0) & (rel < SEL_FORCE_LOCAL))
    n_far_blocks = (TQ // SEL_BLOCK) * (i - 1)

    def near_keys(k_ref, h):
        return jnp.concatenate([k_ref[0, prev_rows, _pair_cols(h)], k_ref[0, cur_rows, _pair_cols(h)]], axis=0)

    def near_values(v_ref, h):
        return jnp.concatenate([v_ref[0, _kv_rows(h), prev_rows], v_ref[0, _kv_rows(h), cur_rows]], axis=1)

    for h in range(N_KV_HEADS):
        qp = _padded_q(q_ref, h)
        qp_ref[h] = qp
        t31 = t31_ref[h]

        bc = cmpb_ref[h, pl.ds(cmp_start, ncp), :]
        sc = _dot(kcmp_ref[0, 0, :, _pair_cols(h)], qp) + bc
        valid = bc > 0.5 * NEG
        mc = jnp.max(sc, axis=0, keepdims=True)
        ec = jnp.where(valid, jnp.exp(sc - mc), 0.0)
        lc = jnp.sum(ec, axis=0, keepdims=True)
        pc = ec / jnp.where(lc > 0.0, lc, 1.0)
        o_cmp = _dot(vcmp_ref[0, 0, _kv_rows(h), :], pc.astype(BF16))

        psum = pc[:, 0:TQ]
        for g in range(1, GROUP):
            psum = psum + pc[:, g * TQ:(g + 1) * TQ]
        ovt = ovt_ref[...]
        p1 = psum.astype(BF16)
        r1 = psum - p1.astype(F32)
        p2 = r1.astype(BF16)
        p3 = (r1 - p2.astype(F32)).astype(BF16)
        imp = _dot(ovt, p1) + _dot(ovt, p2) + _dot(ovt, p3)
        score = jnp.where(causal, imp + jnp.where(forced, FORCE_BONUS, 0.0), NEG)
        cnt = jnp.zeros((nsel, TQ), jnp.int32)
        for jp in range(nsel):
            row = score[jp:jp + 1, :]
            beats = (row > score) | ((row == score) & (blk > jp))
            cnt = cnt + beats.astype(jnp.int32)
        sel = cnt < min(SEL_TOP, nsel)
        selb_ref[h] = jnp.concatenate([jnp.where(sel, 0.0, NEG)] * GROUP, axis=1)
        selfar_ref[h] = jnp.concatenate(
            [jnp.where(sel & (blk < n_far_blocks), 0.0, NEG)] * GROUP, axis=1) + t31

        near_tab = near_ref[0, h]
        s = _dot(near_keys(kslc_ref, h), qp) + near_tab
        first_blk = (TQ // SEL_BLOCK) * (i - 1)
        parts = []
        for u in range(NEAR // SEL_BLOCK):
            row = selb_ref[h, pl.ds(jnp.maximum(first_blk + u, 0), 1), :]
            parts.append(s[u * SEL_BLOCK:(u + 1) * SEL_BLOCK] + row)
        s = jnp.concatenate(parts, axis=0)
        m = jnp.max(s, axis=0, keepdims=True)
        m_ref[h] = m
        acc_ref[h] = _dot(_with_ones(near_values(vslc_ref, h)), _exp_bf16(s - m))

        s_near = _dot(near_keys(kwin_ref, h), qp) + near_tab
        far_k = jnp.concatenate(
            [kwin_ref[0, pl.ds(pl.multiple_of(jnp.maximum(i - back, 0) * TQ, TQ), TQ), _pair_cols(h)]
             for back in range(NSA_WINDOW // TQ, 1, -1)], axis=0)
        far_v = jnp.concatenate(
            [vwin_ref[0, _kv_rows(h), pl.ds(pl.multiple_of(jnp.maximum(i - back, 0) * TQ, TQ), TQ)]
             for back in range(NSA_WINDOW // TQ, 1, -1)], axis=1)
        s_far = _dot(far_k, qp) + wfar_ref[0, h]
        mw = jnp.maximum(jnp.max(s_near, axis=0, keepdims=True), jnp.max(s_far, axis=0, keepdims=True))
        acc_w = (_dot(_with_ones(near_values(vwin_ref, h)), _exp_bf16(s_near - mw))
                 + _dot(_with_ones(far_v), _exp_bf16(s_far - mw)))
        o_win = acc_w[:HEAD_DIM] / acc_w[HEAD_DIM:HEAD_DIM + 1]

        for g in range(GROUP):
            rows = _head_rows(h, g)
            lanes = slice(g * TQ, (g + 1) * TQ)
            part_ref[rows, :] = (o_cmp[:, lanes] * g_ref[0, 0, rows, :].astype(F32)
                                 + o_win[:, lanes] * g_ref[0, 2, rows, :].astype(F32))

    def far_body(it, carry):
        base = pl.multiple_of(it * FAR, FAR)
        scores = [_dot(kslc_ref[0, pl.ds(base, FAR), _pair_cols(h)], qp_ref[h]) for h in range(N_KV_HEADS)]
        for h in range(N_KV_HEADS):
            s = scores[h]
            parts = [s[u * SEL_BLOCK:(u + 1) * SEL_BLOCK] + selfar_ref[h, pl.ds(it * PER_FAR + u, 1), :]
                     for u in range(PER_FAR)]
            s = jnp.concatenate(parts, axis=0)
            m_old = m_ref[h]
            m_new = jnp.maximum(m_old, jnp.max(s, axis=0, keepdims=True))
            alpha = jnp.exp(m_old - m_new)
            v = _with_ones(vslc_ref[0, _kv_rows(h), pl.ds(base, FAR)])
            acc_ref[h] = alpha * acc_ref[h] + _dot(v, _exp_bf16(s - m_new))
            m_ref[h] = m_new
        return carry

    n_far_chunks = jnp.maximum(i - 1, 0)
    lax.fori_loop(0, (n_far_chunks + FAR // TQ - 1) // (FAR // TQ), far_body, 0)

    for h in range(N_KV_HEADS):
        o_slc = acc_ref[h, :HEAD_DIM, :] / acc_ref[h, HEAD_DIM:HEAD_DIM + 1, :]
        for g in range(GROUP):
            rows = _head_rows(h, g)
            o = part_ref[rows, :] + o_slc[:, g * TQ:(g + 1) * TQ] * g_ref[0, 1, rows, :].astype(F32)
            o_ref[0, rows, :] = o.astype(BF16)


def _nsa_attention(qt, gt, cmp_nat, cmp_t, kslc, vslc, kwin, vwin, tabs, ovt):
    b, _, s = qt.shape
    nq = s // TQ
    assert s % FAR == 0
    ncp = s // CMP_STRIDE
    nsel = s // SEL_BLOCK
    near, wfar, t31, cmpb = tabs
    g4 = gt.reshape(b, N_BRANCH, HQ, s)
    full = lambda shape: pl.BlockSpec(shape, lambda b_, i: (0,) * len(shape))
    return pl.pallas_call(
        functools.partial(_nsa_kernel, nq), name="nsa_attention", grid=(b, nq),
        in_specs=[
            pl.BlockSpec((1, HQ, TQ), lambda b_, i: (b_, 0, i)),
            pl.BlockSpec((1, N_BRANCH, HQ, TQ), lambda b_, i: (b_, 0, 0, i)),
            pl.BlockSpec((1, 1, ncp, HKV), lambda b_, i: (0, b_, 0, 0)),
            pl.BlockSpec((1, 1, HKV, ncp), lambda b_, i: (1, b_, 0, 0)),
            pl.BlockSpec((1, s, HKV), lambda b_, i: (b_, 0, 0)),
            pl.BlockSpec((1, HKV, s), lambda b_, i: (b_, 0, 0)),
            pl.BlockSpec((1, s, HKV), lambda b_, i: (b_, 0, 0)),
            pl.BlockSpec((1, HKV, s), lambda b_, i: (b_, 0, 0)),
            full(cmpb.shape),
            pl.BlockSpec((1, N_KV_HEADS, NEAR, GL), lambda b_, i: (jnp.minimum(i, 1), 0, 0, 0)),
            pl.BlockSpec((1, N_KV_HEADS, WIN_FAR, GL),
                         lambda b_, i: (jnp.minimum(i, N_WIN_VARIANTS - 1), 0, 0, 0)),
            full(t31.shape),
            full((nsel, ncp)),
        ],
        out_specs=pl.BlockSpec((1, HQ, TQ), lambda b_, i: (b_, 0, i)),
        out_shape=jax.ShapeDtypeStruct((b, HQ, s), BF16),
        scratch_shapes=[pltpu.VMEM((N_KV_HEADS, PAIR, GL), BF16),
                        pltpu.VMEM((N_KV_HEADS, 1, GL), F32),
                        pltpu.VMEM((N_KV_HEADS, HEAD_DIM + ONES_ROWS, GL), F32),
                        pltpu.VMEM((N_KV_HEADS, nsel, GL), F32), pltpu.VMEM((N_KV_HEADS, nsel, GL), F32),
                        pltpu.VMEM((HQ, TQ), F32)],
        compiler_params=pltpu.CompilerParams(
            dimension_semantics=("parallel", "arbitrary"), vmem_limit_bytes=VMEM_LIMIT),
    )(qt, g4, cmp_nat, cmp_t, kslc, vslc, kwin, vwin, cmpb, near, wfar, t31, ovt)


def _compress_kernel(u_ref, ptop_ref, pbot_ref, wtop_ref, wbot_ref, w2_ref, kg_ref, nat_out, t_out):
    t = pl.program_id(0)
    u = u_ref[0, 0]
    a = _dot((u + ptop_ref[0]).astype(BF16), wtop_ref[0])
    bm = _dot((u + pbot_ref[0]).astype(BF16), wbot_ref[0])
    n = bm.shape[0]
    hid = a + pltpu.roll(bm, n - 1, axis=0)
    out_t = _dot(_silu(hid).astype(BF16), w2_ref[0]).T
    out_t = jnp.where(t == 0, _head_norm_t(out_t, kg_ref[...]), out_t)
    t_out[0, 0] = out_t.astype(BF16)
    nat_out[0, 0] = out_t.T.astype(BF16)


def _compress(craw, cmp_k_pos, cmp_k_w1, cmp_k_w2, cmp_v_pos, cmp_v_w1, cmp_v_w2, k_gain):
    b, _, s, _ = craw.shape
    nch = s // CMP_STRIDE
    wide = CMP_STRIDE * HKV
    u = craw.reshape(b, 2, nch, wide)
    eye = jnp.eye(N_KV_HEADS, dtype=F32)

    def big_w1(w1_half):
        w = w1_half.reshape(CMP_STRIDE, HEAD_DIM, CMP_HIDDEN)
        return jnp.einsum('jdn,hk->jhdkn', w, eye).reshape(wide, N_KV_HEADS * CMP_HIDDEN).astype(BF16)

    def pos_row(p_half):
        return jnp.broadcast_to(p_half[:, None, :], (CMP_STRIDE, N_KV_HEADS, HEAD_DIM)).reshape(1, wide)

    def big_w2(w2):
        return jnp.einsum('nd,hk->hnkd', w2, eye).reshape(N_KV_HEADS * CMP_HIDDEN, HKV).astype(BF16)

    half = CMP_STRIDE * HEAD_DIM
    wtop = jnp.stack([big_w1(cmp_k_w1[:half]), big_w1(cmp_v_w1[:half])])
    wbot = jnp.stack([big_w1(cmp_k_w1[half:]), big_w1(cmp_v_w1[half:])])
    ptop = jnp.stack([pos_row(cmp_k_pos[:CMP_STRIDE]), pos_row(cmp_v_pos[:CMP_STRIDE])])
    pbot = jnp.stack([pos_row(cmp_k_pos[CMP_STRIDE:]), pos_row(cmp_v_pos[CMP_STRIDE:])])
    w2 = jnp.stack([big_w2(cmp_k_w2), big_w2(cmp_v_w2)])
    kg = jnp.broadcast_to(k_gain.astype(F32)[:, None], (HEAD_DIM, nch))
    sel = lambda shape: pl.BlockSpec((1,) + shape, lambda t, b_: (t,) + (0,) * len(shape))
    return pl.pallas_call(
        _compress_kernel, name="compress", grid=(2, b),
        in_specs=[
            pl.BlockSpec((1, 1, nch, wide), lambda t, b_: (b_, t, 0, 0)),
            sel((1, wide)), sel((1, wide)),
            sel((wide, N_KV_HEADS * CMP_HIDDEN)), sel((wide, N_KV_HEADS * CMP_HIDDEN)),
            sel((N_KV_HEADS * CMP_HIDDEN, HKV)),
            pl.BlockSpec((HEAD_DIM, nch), lambda t, b_: (0, 0)),
        ],
        out_specs=[pl.BlockSpec((1, 1, nch, HKV), lambda t, b_: (t, b_, 0, 0)),
                   pl.BlockSpec((1, 1, HKV, nch), lambda t, b_: (t, b_, 0, 0))],
        out_shape=[jax.ShapeDtypeStruct((2, b, nch, HKV), BF16),
                   jax.ShapeDtypeStruct((2, b, HKV, nch), BF16)],
        compiler_params=pltpu.CompilerParams(
            dimension_semantics=("arbitrary", "arbitrary"), vmem_limit_bytes=VMEM_LIMIT),
    )(u, ptop, pbot, wtop, wbot, w2, kg)


def kernel(x, rel_table, a_norm, a_w_in, a_q_gain, a_k_gain, a_sink, a_w_out, kv_norm, kv_w,
           kv_k_gain, cmp_k_pos, cmp_k_w1, cmp_k_w2, cmp_v_pos, cmp_v_w1, cmp_v_w2,
           b_norm, b_w_in, b_q_gain, b_w_out):
    b, s, _ = x.shape
    nq = s // TQ
    n_a = a_w_in.shape[0]
    n_b = b_w_in.shape[0]
    swa_tab, near, wfar, t31, cmpb = _make_tables(rel_table.astype(F32), nq)
    ovt = _overlap_t(s)

    a_shapes, a_specs = _a_outs(b, s)
    b_shapes, b_specs = _b_outs(b, s)
    xt_shape, xt_spec = a_shapes[0], a_specs[0]

    ws, gains = _a_weights(a_w_in[0], a_q_gain[0], a_k_gain[0])
    consts = [_bcast(a_norm[0])] + ws + gains
    xt, qt, k, vt, gt = _proj_call(
        _first_kernel, "proj_first", b, s, [x] + consts,
        [_nat_spec(D_MODEL)] + _specs_for(consts), a_shapes, a_specs)

    for layer in range(n_a):
        sink_row = jnp.broadcast_to(
            a_sink[layer].astype(F32).reshape(N_KV_HEADS, 1, GROUP, 1), (N_KV_HEADS, 1, GROUP, TQ)
        ).reshape(N_KV_HEADS, 1, GL)
        ot = _swa_attention(qt, k, vt, gt, swa_tab, sink_row)
        wo = a_w_out[layer].T.astype(BF16)
        if layer + 1 < n_a:
            ws, gains = _a_weights(a_w_in[layer + 1], a_q_gain[layer + 1], a_k_gain[layer + 1])
            consts = [wo, _bcast(a_norm[layer + 1])] + ws + gains
            xt, qt, k, vt, gt = _proj_call(
                _a2a_kernel, "proj_a2a", b, s, [xt, ot] + consts,
                [_tok_spec(D_MODEL), _tok_spec(HQ)] + _specs_for(consts), a_shapes, a_specs)
        else:
            wsb, gb = _b_weights(b_w_in[0], b_q_gain[0])
            consts = ([wo, _bcast(kv_norm), kv_w.T.astype(BF16), _bcast(kv_k_gain[1]), _bcast(kv_k_gain[2]),
                       _bcast(b_norm[0])] + wsb + gb)
            kv_shapes = [jax.ShapeDtypeStruct((b, 2, s, HKV), F32),
                         jax.ShapeDtypeStruct((b, s, HKV), BF16), jax.ShapeDtypeStruct((b, HKV, s), BF16),
                         jax.ShapeDtypeStruct((b, s, HKV), BF16), jax.ShapeDtypeStruct((b, HKV, s), BF16)]
            kv_specs = [pl.BlockSpec((1, 2, TM, HKV), lambda b_, t: (b_, 0, t, 0)),
                        _nat_spec(HKV), _tok_spec(HKV), _nat_spec(HKV), _tok_spec(HKV)]
            xt, craw, kslc, vslc, kwin, vwin, qt, gt = _proj_call(
                _a2b_kernel, "proj_a2b", b, s, [xt, ot] + consts,
                [_tok_spec(D_MODEL), _tok_spec(HQ)] + _specs_for(consts),
                [xt_shape] + kv_shapes + b_shapes, [xt_spec] + kv_specs + b_specs)

    cmp_nat, cmp_t = _compress(craw, cmp_k_pos, cmp_k_w1, cmp_k_w2, cmp_v_pos, cmp_v_w1, cmp_v_w2,
                               kv_k_gain[0])
    tabs = (near, wfar, t31, cmpb)
    for layer in range(n_b):
        ot = _nsa_attention(qt, gt, cmp_nat, cmp_t, kslc, vslc, kwin, vwin, tabs, ovt)
        wo = b_w_out[layer].T.astype(BF16)
        if layer + 1 < n_b:
            wsb, gb = _b_weights(b_w_in[layer + 1], b_q_gain[layer + 1])
            consts = [wo, _bcast(b_norm[layer + 1])] + wsb + gb
            xt, qt, gt = _proj_call(
                _b2b_kernel, "proj_b2b", b, s, [xt, ot] + consts,
                [_tok_spec(D_MODEL), _tok_spec(HQ)] + _specs_for(consts),
                [xt_shape] + b_shapes, [xt_spec] + b_specs)
        else:
            out = _proj_call(
                _final_kernel, "proj_final", b, s, [xt, ot, wo],
                [_tok_spec(D_MODEL), _tok_spec(HQ), _const_spec(wo.shape)],
                jax.ShapeDtypeStruct((b, s, D_MODEL), F32), _nat_spec(D_MODEL))
    return out
```

```python
import functools
import math

import numpy as np
import jax
import jax.numpy as jnp
from jax import lax
from jax.experimental import pallas as pl
from jax.experimental.pallas import tpu as pltpu

D_MODEL = 1024
HEAD_DIM = 64
N_HEADS = 16
N_KV_HEADS = 4
GROUP = N_HEADS // N_KV_HEADS
HQ = N_HEADS * HEAD_DIM
HKV = N_KV_HEADS * HEAD_DIM
N_BRANCH = 3
SWA_WINDOW = 128
NSA_WINDOW = 512
CMP_LEN = 32
CMP_STRIDE = 16
CMP_HIDDEN = 256
SEL_BLOCK = 64
SEL_TOP = 8
SEL_FORCE_LOCAL = 2
NUM_BUCKETS = 32
MAX_DISTANCE = 128
EPS = 1e-6
NEG = -1e30
FORCE_BONUS = 1e6

TQ = 128
TM = 256
GL = GROUP * TQ
PAIR = 2 * HEAD_DIM
NEAR = 2 * TQ
WIN_FAR = NSA_WINDOW - TQ
N_WIN_VARIANTS = NSA_WINDOW // TQ + 1
FAR = 4 * TQ
PER_FAR = FAR // SEL_BLOCK
ONES_ROWS = 16
LOG2E = math.log2(math.e)
Q_SCALE = HEAD_DIM ** -0.5 * LOG2E
VMEM_LIMIT = 56 * 1024 * 1024

F32 = jnp.float32
BF16 = jnp.bfloat16


def _bucket_np(dist):
    d = np.maximum(dist, 0)
    max_exact = NUM_BUCKETS // 2
    ratio = (np.log(np.maximum(d, 1).astype(np.float32) / np.float32(max_exact))
             / np.float32(math.log(MAX_DISTANCE / max_exact))
             * np.float32(NUM_BUCKETS - max_exact))
    large = np.minimum(max_exact + ratio.astype(np.int32), NUM_BUCKETS - 1)
    return np.where(d < max_exact, d, large).astype(np.int32)


def _dist_vector(table, d_lo, length, hi_valid):
    d = d_lo + np.arange(length)
    v = table.T[:, _bucket_np(d)]
    return jnp.where(jnp.asarray((d >= 0) & (d < hi_valid))[None], v, NEG)


def _skew(v, n, step):
    length = v.shape[1]
    assert step * (n - 1) + TQ <= length
    width = length + step
    reps = -(-(n * width) // length)
    w = jnp.tile(v, (1, reps))[:, :n * width].reshape(v.shape[0], n, width)
    return w[:, :, :TQ]


def _toeplitz(table, n_rows, d_first, step, hi_valid):
    length = step * (n_rows - 1) + TQ
    v = _dist_vector(table, d_first - step * (n_rows - 1), length, hi_valid)
    t = _skew(v, n_rows, step)[:, ::-1, :]
    t = t.reshape(N_KV_HEADS, GROUP, n_rows, TQ).transpose(0, 2, 1, 3)
    return t.reshape(N_KV_HEADS, n_rows, GL).astype(F32)


def _mask_rows(t, n_masked):
    rows = np.arange(t.shape[1])[None, :, None] < n_masked
    return jnp.where(jnp.asarray(rows), NEG, t)


def _make_tables(table, nq):
    big = 1 << 30
    near = _toeplitz(table, NEAR, TQ, 1, big)
    near = jnp.stack([_mask_rows(near, TQ), near])
    swa = _toeplitz(table, NEAR, TQ, 1, SWA_WINDOW)
    swa = jnp.stack([_mask_rows(swa, TQ), swa])
    t31 = _toeplitz(table, 1, MAX_DISTANCE, 1, big)

    def saturated(rows):
        return jnp.broadcast_to(t31, (N_KV_HEADS, rows, GL))

    wfar = jnp.concatenate([_toeplitz(table, TQ, NSA_WINDOW, 1, NSA_WINDOW), saturated(WIN_FAR - TQ)], axis=1)
    wfar = jnp.stack([_mask_rows(wfar, min(WIN_FAR, (N_WIN_VARIANTS - 1 - v) * TQ))
                      for v in range(N_WIN_VARIANTS)])
    per = TQ // CMP_STRIDE
    ncp = nq * per
    off = per * (nq - 1)
    sat_c = -(-(MAX_DISTANCE + CMP_LEN - 1) // CMP_STRIDE)
    assert off >= sat_c
    band = _toeplitz(table, sat_c + per - 1, CMP_STRIDE * (sat_c - 1) - (CMP_LEN - 1), CMP_STRIDE, big)
    cmpb = jnp.concatenate(
        [saturated(off - sat_c + 1), band, jnp.full((N_KV_HEADS, ncp - per, GL), NEG, F32)], axis=1)
    return swa, near, wfar, t31, cmpb


def _overlap_t(s):
    ncp = s // CMP_STRIDE
    nsel = s // SEL_BLOCK
    cs = np.arange(ncp)[None, :] * CMP_STRIDE
    ss = np.arange(nsel)[:, None] * SEL_BLOCK
    ov = (cs < ss + SEL_BLOCK) & (cs + CMP_LEN > ss) & (np.arange(ncp)[None, :] < ncp - 1)
    return jnp.asarray(ov.astype(np.float32), dtype=BF16)


def _rms_t(xt, gain_b):
    ms = jnp.mean(xt * xt, axis=0, keepdims=True)
    return (xt * lax.rsqrt(ms + EPS) * gain_b).astype(BF16)


def _head_norm_t(a, gain_b):
    outs = []
    for h in range(a.shape[0] // HEAD_DIM):
        blk = a[h * HEAD_DIM:(h + 1) * HEAD_DIM]
        ms = jnp.mean(blk * blk, axis=0, keepdims=True)
        outs.append(blk * lax.rsqrt(ms + EPS) * gain_b)
    return jnp.concatenate(outs, axis=0)


def _sigmoid(z):
    return 1.0 / (1.0 + jnp.exp(-z))


def _silu(z):
    return z * _sigmoid(z)


def _dot(a, b):
    return jnp.dot(a, b, preferred_element_type=F32)


def _proj_a(xn, w, qg, kg, q_out, k_out, v_out, g_out):
    p = _dot(w[...], xn)
    q_out[0] = _head_norm_t(p[:HQ], qg[...]).astype(BF16)
    k_out[0] = _head_norm_t(p[HQ:HQ + HKV], kg[...]).T.astype(BF16)
    v_out[0] = p[HQ + HKV:HQ + 2 * HKV].astype(BF16)
    g_out[0] = _silu(p[HQ + 2 * HKV:]).astype(BF16)


def _proj_b(xn, w, qg, q_out, g_out):
    p = _dot(w[...], xn)
    q_out[0] = _head_norm_t(p[:HQ], qg[...]).astype(BF16)
    n_gate = N_BRANCH * N_HEADS
    sg = _sigmoid(p[HQ:HQ + n_gate])
    for r in range(n_gate):
        z = p[HQ + n_gate + r * HEAD_DIM:HQ + n_gate + (r + 1) * HEAD_DIM]
        g_out[0, r * HEAD_DIM:(r + 1) * HEAD_DIM, :] = (_silu(z) * sg[r:r + 1]).astype(BF16)


def _proj_kv(xn, wkv, kg1, kg2, craw_out, kslc_out, vslc_out, kwin_out, vwin_out):
    p = _dot(wkv[...], xn)

    def rows(n):
        return p[n * HKV:(n + 1) * HKV]
    craw_out[0, 0] = rows(0).T
    craw_out[0, 1] = rows(1).T
    kslc_out[0] = _head_norm_t(rows(2), kg1[...]).T.astype(BF16)
    vslc_out[0] = rows(3).astype(BF16)
    kwin_out[0] = _head_norm_t(rows(4), kg2[...]).T.astype(BF16)
    vwin_out[0] = rows(5).astype(BF16)


def _first_kernel(x_ref, ng, w, qg, kg, xt_out, q_out, k_out, v_out, g_out):
    xt = x_ref[0].T
    xt_out[0] = xt
    _proj_a(_rms_t(xt, ng[...]), w, qg, kg, q_out, k_out, v_out, g_out)


def _a2a_kernel(x_ref, o_ref, wo, ng, w, qg, kg, xt_out, q_out, k_out, v_out, g_out):
    xt = x_ref[0] + _dot(wo[...], o_ref[0])
    xt_out[0] = xt
    _proj_a(_rms_t(xt, ng[...]), w, qg, kg, q_out, k_out, v_out, g_out)


def _a2b_kernel(x_ref, o_ref, wo, ngkv, wkv, kg1, kg2, ngb, w, qg,
                xt_out, craw_out, kslc_out, vslc_out, kwin_out, vwin_out, q_out, g_out):
    xt = x_ref[0] + _dot(wo[...], o_ref[0])
    xt_out[0] = xt
    _proj_kv(_rms_t(xt, ngkv[...]), wkv, kg1, kg2, craw_out, kslc_out, vslc_out, kwin_out, vwin_out)
    _proj_b(_rms_t(xt, ngb[...]), w, qg, q_out, g_out)


def _b2b_kernel(x_ref, o_ref, wo, ngb, w, qg, xt_out, q_out, g_out):
    xt = x_ref[0] + _dot(wo[...], o_ref[0])
    xt_out[0] = xt
    _proj_b(_rms_t(xt, ngb[...]), w, qg, q_out, g_out)


def _final_kernel(x_ref, o_ref, wo, x_out):
    x_out[0] = (x_ref[0] + _dot(wo[...], o_ref[0])).T


def _tok_spec(rows):
    return pl.BlockSpec((1, rows, TM), lambda b, t: (b, 0, t))


def _nat_spec(cols):
    return pl.BlockSpec((1, TM, cols), lambda b, t: (b, t, 0))


def _const_spec(shape):
    nd = len(shape)
    return pl.BlockSpec(shape, lambda b, t: (0,) * nd)


def _proj_call(body, name, b, s, ins, in_specs, out_shapes, out_specs):
    return pl.pallas_call(
        body, name=name, grid=(b, s // TM),
        in_specs=in_specs, out_specs=out_specs, out_shape=out_shapes,
        compiler_params=pltpu.CompilerParams(
            dimension_semantics=("parallel", "parallel"), vmem_limit_bytes=VMEM_LIMIT),
    )(*ins)


def _bcast(v, scale=1.0):
    return jnp.broadcast_to((v.astype(F32) * scale)[:, None], (v.shape[0], TM))


def _a_weights(w_in, q_gain, k_gain):
    ws = [w_in.T.astype(BF16)]
    gains = [_bcast(q_gain, Q_SCALE), _bcast(k_gain)]
    return ws, gains


def _b_weights(w_in, q_gain):
    ws = [w_in.T.astype(BF16)]
    return ws, [_bcast(q_gain, Q_SCALE)]


def _a_outs(b, s):
    shapes = [jax.ShapeDtypeStruct((b, D_MODEL, s), F32), jax.ShapeDtypeStruct((b, HQ, s), BF16),
              jax.ShapeDtypeStruct((b, s, HKV), BF16), jax.ShapeDtypeStruct((b, HKV, s), BF16),
              jax.ShapeDtypeStruct((b, HQ, s), BF16)]
    specs = [_tok_spec(D_MODEL), _tok_spec(HQ), _nat_spec(HKV), _tok_spec(HKV), _tok_spec(HQ)]
    return shapes, specs


def _b_outs(b, s):
    shapes = [jax.ShapeDtypeStruct((b, HQ, s), BF16), jax.ShapeDtypeStruct((b, N_BRANCH * HQ, s), BF16)]
    specs = [_tok_spec(HQ), _tok_spec(N_BRANCH * HQ)]
    return shapes, specs


def _specs_for(arrs):
    return [_const_spec(a.shape) for a in arrs]


def _head_rows(h, g):
    return slice((h * GROUP + g) * HEAD_DIM, (h * GROUP + g + 1) * HEAD_DIM)


def _kv_rows(h):
    return slice(h * HEAD_DIM, (h + 1) * HEAD_DIM)


def _pair_cols(h):
    return slice((h // 2) * PAIR, (h // 2 + 1) * PAIR)


def _padded_q(q_ref, h):
    qs = jnp.concatenate([q_ref[0, _head_rows(h, g), :] for g in range(GROUP)], axis=1)
    zeros = jnp.zeros_like(qs)
    return jnp.concatenate([qs, zeros] if h % 2 == 0 else [zeros, qs], axis=0)


def _with_ones(v):
    return jnp.concatenate([v, jnp.ones((ONES_ROWS, v.shape[1]), BF16)], axis=0)


def _exp2_bf16(x):
    return jnp.exp2(x.astype(BF16))


def _swa_kernel(q_ref, kp_ref, kc_ref, vp_ref, vc_ref, g_ref, bias_ref, sink_ref, o_ref):
    for h in range(N_KV_HEADS):
        qp = _padded_q(q_ref, h)
        k = jnp.concatenate([kp_ref[0, :, _pair_cols(h)], kc_ref[0, :, _pair_cols(h)]], axis=0)
        s = _dot(k, qp) + bias_ref[0, h]
        sink = sink_ref[h]
        m = jnp.maximum(jnp.max(s, axis=0, keepdims=True), sink)
        v = jnp.concatenate([vp_ref[0, _kv_rows(h), :], vc_ref[0, _kv_rows(h), :]], axis=1)
        acc = _dot(_with_ones(v), _exp2_bf16(s - m))
        o = acc[:HEAD_DIM] / (acc[HEAD_DIM:HEAD_DIM + 1] + jnp.exp2(sink - m))
        for g in range(GROUP):
            rows = _head_rows(h, g)
            o_ref[0, rows, :] = (o[:, g * TQ:(g + 1) * TQ] * g_ref[0, rows, :].astype(F32)).astype(BF16)


def _swa_attention(qt, k, vt, gt, swa_tab, sink_row):
    b, _, s = qt.shape
    nq = s // TQ
    prev = lambda i: jnp.maximum(i - 1, 0)
    return pl.pallas_call(
        _swa_kernel, name="swa_attention", grid=(b, nq),
        in_specs=[
            pl.BlockSpec((1, HQ, TQ), lambda b_, i: (b_, 0, i)),
            pl.BlockSpec((1, TQ, HKV), lambda b_, i: (b_, prev(i), 0)),
            pl.BlockSpec((1, TQ, HKV), lambda b_, i: (b_, i, 0)),
            pl.BlockSpec((1, HKV, TQ), lambda b_, i: (b_, 0, prev(i))),
            pl.BlockSpec((1, HKV, TQ), lambda b_, i: (b_, 0, i)),
            pl.BlockSpec((1, HQ, TQ), lambda b_, i: (b_, 0, i)),
            pl.BlockSpec((1, N_KV_HEADS, NEAR, GL), lambda b_, i: (jnp.minimum(i, 1), 0, 0, 0)),
            pl.BlockSpec((N_KV_HEADS, 1, GL), lambda b_, i: (0, 0, 0)),
        ],
        out_specs=pl.BlockSpec((1, HQ, TQ), lambda b_, i: (b_, 0, i)),
        out_shape=jax.ShapeDtypeStruct((b, HQ, s), BF16),
        compiler_params=pltpu.CompilerParams(
            dimension_semantics=("parallel", "arbitrary"), vmem_limit_bytes=VMEM_LIMIT),
    )(qt, k, k, vt, vt, gt, swa_tab, sink_row)


def _nsa_kernel(nq, q_ref, g_ref, kcmp_ref, vcmp_ref, kslc_ref, vslc_ref, kwin_ref, vwin_ref,
                cmpb_ref, near_ref, wfar_ref, t31_ref, ovt_ref, kauxf_ref, kauxn_ref, o_ref,
                qp_ref, m_ref, acc_ref, selb_ref, part_ref, sc0_ref, sc1_ref):
    i = pl.program_id(1)
    ncp = kcmp_ref.shape[2]
    nsel = ovt_ref.shape[0]
    cmp_start = pl.multiple_of((TQ // CMP_STRIDE) * (nq - 1 - i), 8)
    prev_rows = pl.ds(pl.multiple_of(jnp.maximum(i - 1, 0) * TQ, TQ), TQ)
    cur_rows = pl.ds(pl.multiple_of(i * TQ, TQ), TQ)
    pos = i * TQ + lax.broadcasted_iota(jnp.int32, (nsel, TQ), 1)
    blk = lax.broadcasted_iota(jnp.int32, (nsel, TQ), 0)
    causal = blk * SEL_BLOCK <= pos
    rel = pos // SEL_BLOCK - blk
    forced = (blk == 0) | ((rel >= 0) & (rel < SEL_FORCE_LOCAL))
    n_far_blocks = (TQ // SEL_BLOCK) * (i - 1)

    def near_keys(k_ref, h):
        return jnp.concatenate([k_ref[0, prev_rows, _pair_cols(h)], k_ref[0, cur_rows, _pair_cols(h)]], axis=0)

    def near_values(v_ref, h):
        return jnp.concatenate([v_ref[0, _kv_rows(h), prev_rows], v_ref[0, _kv_rows(h), cur_rows]], axis=1)

    def win_far(ref, h, lanes_axis):
        starts = [pl.multiple_of(jnp.maximum(i - back, 0) * TQ, TQ) for back in range(NSA_WINDOW // TQ, 1, -1)]
        if lanes_axis:
            return jnp.concatenate([ref[0, _kv_rows(h), pl.ds(st, TQ)] for st in starts], axis=1)
        return jnp.concatenate([ref[0, pl.ds(st, TQ), _pair_cols(h)] for st in starts], axis=0)

    s_cmp, s_wnear, s_wfar = [], [], []
    for h in range(N_KV_HEADS):
        qp = _padded_q(q_ref, h)
        qp_ref[h, :PAIR, :] = qp
        s_cmp.append(_dot(kcmp_ref[0, 0, :, _pair_cols(h)], qp))
        s_wnear.append(_dot(near_keys(kwin_ref, h), qp))
        s_wfar.append(_dot(win_far(kwin_ref, h, False), qp))

    o_cmp = []
    row8 = lax.broadcasted_iota(jnp.int32, (8, GL), 0)
    first_blk = (TQ // SEL_BLOCK) * (i - 1)
    for h in range(N_KV_HEADS):
        bc = cmpb_ref[h, pl.ds(cmp_start, ncp), :]
        sc = s_cmp[h] + bc
        valid = bc > 0.5 * NEG
        mc = jnp.max(sc, axis=0, keepdims=True)
        ec = jnp.where(valid, jnp.exp2(sc - mc), 0.0)
        lc = jnp.sum(ec, axis=0, keepdims=True)
        pc = ec / jnp.where(lc > 0.0, lc, 1.0)
        o_cmp.append(_dot(vcmp_ref[0, 0, _kv_rows(h), :], pc.astype(BF16)))

        psum = pc[:, 0:TQ]
        for g in range(1, GROUP):
            psum = psum + pc[:, g * TQ:(g + 1) * TQ]
        ovt = ovt_ref[...]
        p1 = psum.astype(BF16)
        r1 = psum - p1.astype(F32)
        p2 = r1.astype(BF16)
        p3 = (r1 - p2.astype(F32)).astype(BF16)
        imp = _dot(ovt, p1) + _dot(ovt, p2) + _dot(ovt, p3)
        score = jnp.where(causal, imp + jnp.where(forced, FORCE_BONUS, 0.0), NEG)
        cnt = jnp.zeros((nsel, TQ), jnp.int32)
        for jp in range(nsel):
            row = score[jp:jp + 1, :]
            beats = (row > score) | ((row == score) & (blk > jp))
            cnt = cnt + beats.astype(jnp.int32)
        sel = cnt < min(SEL_TOP, nsel)
        selb_ref[h] = jnp.concatenate([jnp.where(sel, 0.0, NEG)] * GROUP, axis=1)
        sel_far = jnp.concatenate([jnp.where(sel & (blk < n_far_blocks), 0.0, NEG)] * GROUP, axis=1)
        t31 = t31_ref[h]
        hi = t31.astype(BF16).astype(F32)
        const_rows = jnp.where(row8 == 0, hi, jnp.where(row8 == 1, t31 - hi, 0.0))
        near_rows = [selb_ref[h, pl.ds(jnp.maximum(first_blk + u, 0), 1), :] for u in range(NEAR // SEL_BLOCK)]
        aux = jnp.concatenate(
            [sel_far, const_rows] + near_rows
            + [jnp.zeros((PAIR - nsel - 8 - NEAR // SEL_BLOCK, GL), F32)], axis=0)
        qp_ref[h, PAIR:, :] = aux.astype(BF16)

    s_snear = [_dot(jnp.concatenate([near_keys(kslc_ref, h), kauxn_ref[...]], axis=1), qp_ref[h])
               for h in range(N_KV_HEADS)]

    sc_refs = (sc0_ref, sc1_ref)

    def sweep_scores(st, dst_ref, h):
        keys = slice(st * FAR, (st + 1) * FAR)
        lhs = jnp.concatenate([kslc_ref[0, keys, _pair_cols(h)], kauxf_ref[keys, :]], axis=1)
        dst_ref[h] = _dot(lhs, qp_ref[h])

    for h in range(N_KV_HEADS):
        sweep_scores(0, sc_refs[0], h)

    for h in range(N_KV_HEADS):
        near_tab = near_ref[0, h]
        s = s_snear[h] + near_tab
        m = jnp.max(s, axis=0, keepdims=True)
        m_ref[h] = m
        acc_ref[h] = _dot(_with_ones(near_values(vslc_ref, h)), _exp2_bf16(s - m))

        s_near = s_wnear[h] + near_tab
        s_far = s_wfar[h] + wfar_ref[0, h]
        mw = jnp.maximum(jnp.max(s_near, axis=0, keepdims=True), jnp.max(s_far, axis=0, keepdims=True))
        acc_w = (_dot(_with_ones(near_values(vwin_ref, h)), _exp2_bf16(s_near - mw))
                 + _dot(_with_ones(win_far(vwin_ref, h, True)), _exp2_bf16(s_far - mw)))
        o_win = acc_w[:HEAD_DIM] / acc_w[HEAD_DIM:HEAD_DIM + 1]

        for g in range(GROUP):
            rows = _head_rows(h, g)
            lanes = slice(g * TQ, (g + 1) * TQ)
            part_ref[rows, :] = (o_cmp[h][:, lanes] * g_ref[0, 0, rows, :].astype(F32)
                                 + o_win[:, lanes] * g_ref[0, 2, rows, :].astype(F32))

    def sweep_consume(st, src_ref, h):
        s = src_ref[h]
        m_old = m_ref[h]
        m_new = jnp.maximum(m_old, jnp.max(s, axis=0, keepdims=True))
        alpha = jnp.exp2(m_old - m_new)
        v = _with_ones(vslc_ref[0, _kv_rows(h), st * FAR:(st + 1) * FAR])
        acc_ref[h] = alpha * acc_ref[h] + _dot(v, _exp2_bf16(s - m_new))
        m_ref[h] = m_new

    n_far_chunks = jnp.maximum(i - 1, 0)
    n_steps = (n_far_chunks + FAR // TQ - 1) // (FAR // TQ)
    max_steps = kslc_ref.shape[1] // FAR
    for st in range(max_steps):
        if st + 1 < max_steps:
            @pl.when(st + 1 < n_steps)
            def _(st=st):
                for h in range(N_KV_HEADS):
                    sweep_consume(st, sc_refs[st % 2], h)
                    sweep_scores(st + 1, sc_refs[(st + 1) % 2], h)

        @pl.when(st + 1 == n_steps)
        def _(st=st):
            for h in range(N_KV_HEADS):
                sweep_consume(st, sc_refs[st % 2], h)

    for h in range(N_KV_HEADS):
        o_slc = acc_ref[h, :HEAD_DIM, :] / acc_ref[h, HEAD_DIM:HEAD_DIM + 1, :]
        for g in range(GROUP):
            rows = _head_rows(h, g)
            o = part_ref[rows, :] + o_slc[:, g * TQ:(g + 1) * TQ] * g_ref[0, 1, rows, :].astype(F32)
            o_ref[0, rows, :] = o.astype(BF16)


def _nsa_attention(qt, gt, cmp_nat, cmp_t, kslc, vslc, kwin, vwin, tabs, ovt):
    b, _, s = qt.shape
    nq = s // TQ
    assert s % FAR == 0
    ncp = s // CMP_STRIDE
    nsel = s // SEL_BLOCK
    near, wfar, t31, cmpb = tabs
    g4 = gt.reshape(b, N_BRANCH, HQ, s)
    assert nsel + 8 + NEAR // SEL_BLOCK <= PAIR
    cols = np.arange(PAIR)[None, :]
    key_blk = np.arange(s)[:, None] // SEL_BLOCK
    kaux_far = jnp.asarray(((cols == key_blk) | (cols == nsel) | (cols == nsel + 1)).astype(np.float32), BF16)
    near_blk = np.arange(NEAR)[:, None] // SEL_BLOCK
    kaux_near = jnp.asarray((cols == nsel + 8 + near_blk).astype(np.float32), BF16)
    full = lambda shape: pl.BlockSpec(shape, lambda b_, i: (0,) * len(shape))
    return pl.pallas_call(
        functools.partial(_nsa_kernel, nq), name="nsa_attention", grid=(b, nq),
        in_specs=[
            pl.BlockSpec((1, HQ, TQ), lambda b_, i: (b_, 0, i)),
            pl.BlockSpec((1, N_BRANCH, HQ, TQ), lambda b_, i: (b_, 0, 0, i)),
            pl.BlockSpec((1, 1, ncp, HKV), lambda b_, i: (0, b_, 0, 0)),
            pl.BlockSpec((1, 1, HKV, ncp), lambda b_, i: (1, b_, 0, 0)),
            pl.BlockSpec((1, s, HKV), lambda b_, i: (b_, 0, 0)),
            pl.BlockSpec((1, HKV, s), lambda b_, i: (b_, 0, 0)),
            pl.BlockSpec((1, s, HKV), lambda b_, i: (b_, 0, 0)),
            pl.BlockSpec((1, HKV, s), lambda b_, i: (b_, 0, 0)),
            full(cmpb.shape),
            pl.BlockSpec((1, N_KV_HEADS, NEAR, GL), lambda b_, i: (jnp.minimum(i, 1), 0, 0, 0)),
            pl.BlockSpec((1, N_KV_HEADS, WIN_FAR, GL),
                         lambda b_, i: (jnp.minimum(i, N_WIN_VARIANTS - 1), 0, 0, 0)),
            full(t31.shape),
            full((nsel, ncp)),
            full((s, PAIR)), full((NEAR, PAIR)),
        ],
        out_specs=pl.BlockSpec((1, HQ, TQ), lambda b_, i: (b_, 0, i)),
        out_shape=jax.ShapeDtypeStruct((b, HQ, s), BF16),
        scratch_shapes=[pltpu.VMEM((N_KV_HEADS, 2 * PAIR, GL), BF16),
                        pltpu.VMEM((N_KV_HEADS, 1, GL), F32),
                        pltpu.VMEM((N_KV_HEADS, HEAD_DIM + ONES_ROWS, GL), F32),
                        pltpu.VMEM((N_KV_HEADS, nsel, GL), F32),
                        pltpu.VMEM((HQ, TQ), F32),
                        pltpu.VMEM((N_KV_HEADS, FAR, GL), F32), pltpu.VMEM((N_KV_HEADS, FAR, GL), F32)],
        compiler_params=pltpu.CompilerParams(
            dimension_semantics=("parallel", "arbitrary"), vmem_limit_bytes=VMEM_LIMIT),
    )(qt, g4, cmp_nat, cmp_t, kslc, vslc, kwin, vwin, cmpb, near, wfar, t31, ovt, kaux_far, kaux_near)


def _compress_kernel(u_ref, ptop_ref, pbot_ref, wtop_ref, wbot_ref, w2_ref, kg_ref, nat_out, t_out):
    t = pl.program_id(0)
    u = u_ref[0, 0]
    a = _dot((u + ptop_ref[0]).astype(BF16), wtop_ref[0])
    bm = _dot((u + pbot_ref[0]).astype(BF16), wbot_ref[0])
    n = bm.shape[0]
    hid = a + pltpu.roll(bm, n - 1, axis=0)
    out_t = _dot(_silu(hid).astype(BF16), w2_ref[0]).T
    out_t = jnp.where(t == 0, _head_norm_t(out_t, kg_ref[...]), out_t)
    t_out[0, 0] = out_t.astype(BF16)
    nat_out[0, 0] = out_t.T.astype(BF16)


def _compress(craw, cmp_k_pos, cmp_k_w1, cmp_k_w2, cmp_v_pos, cmp_v_w1, cmp_v_w2, k_gain):
    b, _, s, _ = craw.shape
    nch = s // CMP_STRIDE
    wide = CMP_STRIDE * HKV
    u = craw.reshape(b, 2, nch, wide)
    eye = jnp.eye(N_KV_HEADS, dtype=F32)

    def big_w1(w1_half):
        w = w1_half.reshape(CMP_STRIDE, HEAD_DIM, CMP_HIDDEN)
        return jnp.einsum('jdn,hk->jhdkn', w, eye).reshape(wide, N_KV_HEADS * CMP_HIDDEN).astype(BF16)

    def pos_row(p_half):
        return jnp.broadcast_to(p_half[:, None, :], (CMP_STRIDE, N_KV_HEADS, HEAD_DIM)).reshape(1, wide)

    def big_w2(w2):
        return jnp.einsum('nd,hk->hnkd', w2, eye).reshape(N_KV_HEADS * CMP_HIDDEN, HKV).astype(BF16)

    half = CMP_STRIDE * HEAD_DIM
    wtop = jnp.stack([big_w1(cmp_k_w1[:half]), big_w1(cmp_v_w1[:half])])
    wbot = jnp.stack([big_w1(cmp_k_w1[half:]), big_w1(cmp_v_w1[half:])])
    ptop = jnp.stack([pos_row(cmp_k_pos[:CMP_STRIDE]), pos_row(cmp_v_pos[:CMP_STRIDE])])
    pbot = jnp.stack([pos_row(cmp_k_pos[CMP_STRIDE:]), pos_row(cmp_v_pos[CMP_STRIDE:])])
    w2 = jnp.stack([big_w2(cmp_k_w2), big_w2(cmp_v_w2)])
    kg = jnp.broadcast_to(k_gain.astype(F32)[:, None], (HEAD_DIM, nch))
    sel = lambda shape: pl.BlockSpec((1,) + shape, lambda t, b_: (t,) + (0,) * len(shape))
    return pl.pallas_call(
        _compress_kernel, name="compress", grid=(2, b),
        in_specs=[
            pl.BlockSpec((1, 1, nch, wide), lambda t, b_: (b_, t, 0, 0)),
            sel((1, wide)), sel((1, wide)),
            sel((wide, N_KV_HEADS * CMP_HIDDEN)), sel((wide, N_KV_HEADS * CMP_HIDDEN)),
            sel((N_KV_HEADS * CMP_HIDDEN, HKV)),
            pl.BlockSpec((HEAD_DIM, nch), lambda t, b_: (0, 0)),
        ],
        out_specs=[pl.BlockSpec((1, 1, nch, HKV), lambda t, b_: (t, b_, 0, 0)),
                   pl.BlockSpec((1, 1, HKV, nch), lambda t, b_: (t, b_, 0, 0))],
        out_shape=[jax.ShapeDtypeStruct((2, b, nch, HKV), BF16),
                   jax.ShapeDtypeStruct((2, b, HKV, nch), BF16)],
        compiler_params=pltpu.CompilerParams(
            dimension_semantics=("arbitrary", "arbitrary"), vmem_limit_bytes=VMEM_LIMIT),
    )(u, ptop, pbot, wtop, wbot, w2, kg)


def kernel(x, rel_table, a_norm, a_w_in, a_q_gain, a_k_gain, a_sink, a_w_out, kv_norm, kv_w,
           kv_k_gain, cmp_k_pos, cmp_k_w1, cmp_k_w2, cmp_v_pos, cmp_v_w1, cmp_v_w2,
           b_norm, b_w_in, b_q_gain, b_w_out):
    b, s, _ = x.shape
    nq = s // TQ
    n_a = a_w_in.shape[0]
    n_b = b_w_in.shape[0]
    swa_tab, near, wfar, t31, cmpb = _make_tables(rel_table.astype(F32) * LOG2E, nq)
    ovt = _overlap_t(s)

    a_shapes, a_specs = _a_outs(b, s)
    b_shapes, b_specs = _b_outs(b, s)
    xt_shape, xt_spec = a_shapes[0], a_specs[0]

    ws, gains = _a_weights(a_w_in[0], a_q_gain[0], a_k_gain[0])
    consts = [_bcast(a_norm[0])] + ws + gains
    xt, qt, k, vt, gt = _proj_call(
        _first_kernel, "proj_first", b, s, [x] + consts,
        [_nat_spec(D_MODEL)] + _specs_for(consts), a_shapes, a_specs)

    for layer in range(n_a):
        sink_row = jnp.broadcast_to(
            (a_sink[layer].astype(F32) * LOG2E).reshape(N_KV_HEADS, 1, GROUP, 1), (N_KV_HEADS, 1, GROUP, TQ)
        ).reshape(N_KV_HEADS, 1, GL)
        ot = _swa_attention(qt, k, vt, gt, swa_tab, sink_row)
        wo = a_w_out[layer].T.astype(BF16)
        if layer + 1 < n_a:
            ws, gains = _a_weights(a_w_in[layer + 1], a_q_gain[layer + 1], a_k_gain[layer + 1])
            consts = [wo, _bcast(a_norm[layer + 1])] + ws + gains
            xt, qt, k, vt, gt = _proj_call(
                _a2a_kernel, "proj_a2a", b, s, [xt, ot] + consts,
                [_tok_spec(D_MODEL), _tok_spec(HQ)] + _specs_for(consts), a_shapes, a_specs)
        else:
            wsb, gb = _b_weights(b_w_in[0], b_q_gain[0])
            consts = ([wo, _bcast(kv_norm), kv_w.T.astype(BF16), _bcast(kv_k_gain[1]), _bcast(kv_k_gain[2]),
                       _bcast(b_norm[0])] + wsb + gb)
            kv_shapes = [jax.ShapeDtypeStruct((b, 2, s, HKV), F32),
                         jax.ShapeDtypeStruct((b, s, HKV), BF16), jax.ShapeDtypeStruct((b, HKV, s), BF16),
                         jax.ShapeDtypeStruct((b, s, HKV), BF16), jax.ShapeDtypeStruct((b, HKV, s), BF16)]
            kv_specs = [pl.BlockSpec((1, 2, TM, HKV), lambda b_, t: (b_, 0, t, 0)),
                        _nat_spec(HKV), _tok_spec(HKV), _nat_spec(HKV), _tok_spec(HKV)]
            xt, craw, kslc, vslc, kwin, vwin, qt, gt = _proj_call(
                _a2b_kernel, "proj_a2b", b, s, [xt, ot] + consts,
                [_tok_spec(D_MODEL), _tok_spec(HQ)] + _specs_for(consts),
                [xt_shape] + kv_shapes + b_shapes, [xt_spec] + kv_specs + b_specs)

    cmp_nat, cmp_t = _compress(craw, cmp_k_pos, cmp_k_w1, cmp_k_w2, cmp_v_pos, cmp_v_w1, cmp_v_w2,
                               kv_k_gain[0])
    tabs = (near, wfar, t31, cmpb)
    for layer in range(n_b):
        ot = _nsa_attention(qt, gt, cmp_nat, cmp_t, kslc, vslc, kwin, vwin, tabs, ovt)
        wo = b_w_out[layer].T.astype(BF16)
        if layer + 1 < n_b:
            wsb, gb = _b_weights(b_w_in[layer + 1], b_q_gain[layer + 1])
            consts = [wo, _bcast(b_norm[layer + 1])] + wsb + gb
            xt, qt, gt = _proj_call(
                _b2b_kernel, "proj_b2b", b, s, [xt, ot] + consts,
                [_tok_spec(D_MODEL), _tok_spec(HQ)] + _specs_for(consts),
                [xt_shape] + b_shapes, [xt_spec] + b_specs)
        else:
            out = _proj_call(
                _final_kernel, "proj_final", b, s, [xt, ot, wo],
                [_tok_spec(D_MODEL), _tok_spec(HQ), _const_spec(wo.shape)],
                jax.ShapeDtypeStruct((b, s, D_MODEL), F32), _nat_spec(D_MODEL))
    return out
```

```python
import functools
import math

import numpy as np
import jax
import jax.numpy as jnp
from jax import lax
from jax.experimental import pallas as pl
from jax.experimental.pallas import tpu as pltpu

D_MODEL = 1024
HEAD_DIM = 64
N_HEADS = 16
N_KV_HEADS = 4
GROUP = N_HEADS // N_KV_HEADS
HQ = N_HEADS * HEAD_DIM
HKV = N_KV_HEADS * HEAD_DIM
N_BRANCH = 3
SWA_WINDOW = 128
NSA_WINDOW = 512
CMP_LEN = 32
CMP_STRIDE = 16
CMP_HIDDEN = 256
SEL_BLOCK = 64
SEL_TOP = 8
SEL_FORCE_LOCAL = 2
NUM_BUCKETS = 32
MAX_DISTANCE = 128
EPS = 1e-6
NEG = -1e30
FORCE_BONUS = 1e6

TQ = 128
TM = 512
GL = GROUP * TQ
PAIR = 2 * HEAD_DIM
NEAR = 2 * TQ
WIN_FAR = NSA_WINDOW - TQ
N_WIN_VARIANTS = NSA_WINDOW // TQ + 1
FAR = 4 * TQ
ONES_ROWS = 16
LOG2E = math.log2(math.e)
Q_SCALE = HEAD_DIM ** -0.5 * LOG2E
VMEM_LIMIT = 56 * 1024 * 1024

F32 = jnp.float32
BF16 = jnp.bfloat16


def _bucket_np(dist):
    d = np.maximum(dist, 0)
    max_exact = NUM_BUCKETS // 2
    ratio = (np.log(np.maximum(d, 1).astype(np.float32) / np.float32(max_exact))
             / np.float32(math.log(MAX_DISTANCE / max_exact))
             * np.float32(NUM_BUCKETS - max_exact))
    large = np.minimum(max_exact + ratio.astype(np.int32), NUM_BUCKETS - 1)
    return np.where(d < max_exact, d, large).astype(np.int32)


def _dist_vector(table, d_lo, length, hi_valid):
    d = d_lo + np.arange(length)
    v = table.T[:, _bucket_np(d)]
    return jnp.where(jnp.asarray((d >= 0) & (d < hi_valid))[None], v, NEG)


def _skew(v, n, step):
    length = v.shape[1]
    assert step * (n - 1) + TQ <= length
    width = length + step
    reps = -(-(n * width) // length)
    w = jnp.tile(v, (1, reps))[:, :n * width].reshape(v.shape[0], n, width)
    return w[:, :, :TQ]


def _toeplitz(table, n_rows, d_first, step, hi_valid):
    length = step * (n_rows - 1) + TQ
    v = _dist_vector(table, d_first - step * (n_rows - 1), length, hi_valid)
    t = _skew(v, n_rows, step)[:, ::-1, :]
    t = t.reshape(N_KV_HEADS, GROUP, n_rows, TQ).transpose(0, 2, 1, 3)
    return t.reshape(N_KV_HEADS, n_rows, GL).astype(F32)


def _mask_rows(t, n_masked):
    rows = np.arange(t.shape[1])[None, :, None] < n_masked
    return jnp.where(jnp.asarray(rows), NEG, t)


def _make_tables(table, nq):
    big = 1 << 30
    near = _toeplitz(table, NEAR, TQ, 1, big)
    near = jnp.stack([_mask_rows(near, TQ), near])
    swa = _toeplitz(table, NEAR, TQ, 1, SWA_WINDOW)
    swa = jnp.stack([_mask_rows(swa, TQ), swa])
    t31 = _toeplitz(table, 1, MAX_DISTANCE, 1, big)

    def saturated(rows):
        return jnp.broadcast_to(t31, (N_KV_HEADS, rows, GL))

    wfar = jnp.concatenate([_toeplitz(table, TQ, NSA_WINDOW, 1, NSA_WINDOW), saturated(WIN_FAR - TQ)], axis=1)
    wfar = jnp.stack([_mask_rows(wfar, min(WIN_FAR, (N_WIN_VARIANTS - 1 - v) * TQ))
                      for v in range(N_WIN_VARIANTS)])
    per = TQ // CMP_STRIDE
    ncp = nq * per
    off = per * (nq - 1)
    sat_c = -(-(MAX_DISTANCE + CMP_LEN - 1) // CMP_STRIDE)
    assert off >= sat_c
    band = _toeplitz(table, sat_c + per - 1, CMP_STRIDE * (sat_c - 1) - (CMP_LEN - 1), CMP_STRIDE, big)
    cmpb = jnp.concatenate(
        [saturated(off - sat_c + 1), band, jnp.full((N_KV_HEADS, ncp - per, GL), NEG, F32)], axis=1)

    def widen(t):
        t = jnp.swapaxes(t, -3, -2)
        return t.reshape(t.shape[:-2] + (N_KV_HEADS * GL,))

    return widen(swa), widen(near), widen(wfar), widen(t31), widen(cmpb)


def _overlap_t(s):
    ncp = s // CMP_STRIDE
    nsel = s // SEL_BLOCK
    cs = np.arange(ncp)[None, :] * CMP_STRIDE
    ss = np.arange(nsel)[:, None] * SEL_BLOCK
    ov = (cs < ss + SEL_BLOCK) & (cs + CMP_LEN > ss) & (np.arange(ncp)[None, :] < ncp - 1)
    return jnp.asarray(ov.astype(np.float32), dtype=BF16)


def _lanes(gain_b, n):
    return jnp.concatenate([gain_b] * (n // gain_b.shape[1]), axis=1)


def _rms_t(xt, gain_b):
    ms = jnp.mean(xt * xt, axis=0, keepdims=True)
    return (xt * lax.rsqrt(ms + EPS) * _lanes(gain_b, xt.shape[1])).astype(BF16)


def _head_norm_t(a, gain_b):
    outs = []
    for h in range(a.shape[0] // HEAD_DIM):
        blk = a[h * HEAD_DIM:(h + 1) * HEAD_DIM]
        ms = jnp.mean(blk * blk, axis=0, keepdims=True)
        outs.append(blk * lax.rsqrt(ms + EPS) * _lanes(gain_b, a.shape[1]))
    return jnp.concatenate(outs, axis=0)


def _sigmoid_h(hz):
    return 0.5 + 0.5 * jnp.tanh(hz)


def _silu_h(hz):
    return hz + hz * jnp.tanh(hz)


def _silu(z):
    return _silu_h(0.5 * z)


def _dot(a, b):
    return jnp.dot(a, b, preferred_element_type=F32)


def _proj_a(xn, w, qg, kg, q_out, k_out, v_out, g_out):
    p = _dot(w[...], xn)
    q_out[0] = _head_norm_t(p[:HQ], qg[...]).astype(BF16)
    k_out[0] = _head_norm_t(p[HQ:HQ + HKV], kg[...]).T.astype(BF16)
    v_out[0] = p[HQ + HKV:HQ + 2 * HKV].astype(BF16)
    g_out[0] = _silu_h(p[HQ + 2 * HKV:]).astype(BF16)


def _proj_b(xn, w, qg, q_out, g_out):
    p = _dot(w[...], xn)
    q_out[0] = _head_norm_t(p[:HQ], qg[...]).astype(BF16)
    n_gate = N_BRANCH * N_HEADS
    sg = _sigmoid_h(p[HQ:HQ + n_gate])
    for r in range(n_gate):
        z = p[HQ + n_gate + r * HEAD_DIM:HQ + n_gate + (r + 1) * HEAD_DIM]
        g_out[0, r * HEAD_DIM:(r + 1) * HEAD_DIM, :] = (_silu_h(z) * sg[r:r + 1]).astype(BF16)


def _proj_kv(xn, wkv, kg1, kg2, craw_out, kslc_out, vslc_out, kwin_out, vwin_out):
    p = _dot(wkv[...], xn)

    def rows(n):
        return p[n * HKV:(n + 1) * HKV]
    craw_out[0, 0] = rows(0).T
    craw_out[0, 1] = rows(1).T
    kslc_out[0] = _head_norm_t(rows(2), kg1[...]).T.astype(BF16)
    vslc_out[0] = rows(3).astype(BF16)
    kwin_out[0] = _head_norm_t(rows(4), kg2[...]).T.astype(BF16)
    vwin_out[0] = rows(5).astype(BF16)


def _first_kernel(x_ref, ng, w, qg, kg, xt_out, q_out, k_out, v_out, g_out):
    xt = x_ref[0].T
    xt_out[0] = xt
    _proj_a(_rms_t(xt, ng[...]), w, qg, kg, q_out, k_out, v_out, g_out)


def _a2a_kernel(x_ref, o_ref, wo, ng, w, qg, kg, xt_out, q_out, k_out, v_out, g_out):
    xt = x_ref[0] + _dot(wo[...], o_ref[0])
    xt_out[0] = xt
    _proj_a(_rms_t(xt, ng[...]), w, qg, kg, q_out, k_out, v_out, g_out)


def _a2b_kernel(x_ref, o_ref, wo, ngkv, wkv, kg1, kg2, ngb, w, qg,
                xt_out, craw_out, kslc_out, vslc_out, kwin_out, vwin_out, q_out, g_out):
    xt = x_ref[0] + _dot(wo[...], o_ref[0])
    xt_out[0] = xt
    _proj_kv(_rms_t(xt, ngkv[...]), wkv, kg1, kg2, craw_out, kslc_out, vslc_out, kwin_out, vwin_out)
    _proj_b(_rms_t(xt, ngb[...]), w, qg, q_out, g_out)


def _b2b_kernel(x_ref, o_ref, wo, ngb, w, qg, xt_out, q_out, g_out):
    xt = x_ref[0] + _dot(wo[...], o_ref[0])
    xt_out[0] = xt
    _proj_b(_rms_t(xt, ngb[...]), w, qg, q_out, g_out)


def _final_kernel(x_ref, o_ref, wo, x_out):
    x_out[0] = (x_ref[0] + _dot(wo[...], o_ref[0])).T


def _tok_spec(rows):
    return pl.BlockSpec((1, rows, TM), lambda b, t: (b, 0, t))


def _nat_spec(cols):
    return pl.BlockSpec((1, TM, cols), lambda b, t: (b, t, 0))


def _const_spec(shape):
    nd = len(shape)
    return pl.BlockSpec(shape, lambda b, t: (0,) * nd, pipeline_mode=pl.Buffered(1))


def _proj_call(body, name, b, s, ins, in_specs, out_shapes, out_specs):
    return pl.pallas_call(
        body, name=name, grid=(b, s // TM),
        in_specs=in_specs, out_specs=out_specs, out_shape=out_shapes,
        compiler_params=pltpu.CompilerParams(
            dimension_semantics=("parallel", "parallel"), vmem_limit_bytes=VMEM_LIMIT),
    )(*ins)


def _bcast(v, scale=1.0):
    return jnp.broadcast_to((v.astype(F32) * scale)[:, None], (v.shape[0], TQ))


def _a_weights(w_in, q_gain, k_gain):
    halve = np.where(np.arange(w_in.shape[1]) >= HQ + 2 * HKV, 0.5, 1.0).astype(np.float32)
    ws = [(w_in * halve).T.astype(BF16)]
    gains = [_bcast(q_gain, Q_SCALE), _bcast(k_gain)]
    return ws, gains


def _b_weights(w_in, q_gain):
    halve = np.where(np.arange(w_in.shape[1]) >= HQ, 0.5, 1.0).astype(np.float32)
    ws = [(w_in * halve).T.astype(BF16)]
    return ws, [_bcast(q_gain, Q_SCALE)]


def _a_outs(b, s):
    shapes = [jax.ShapeDtypeStruct((b, D_MODEL, s), F32), jax.ShapeDtypeStruct((b, HQ, s), BF16),
              jax.ShapeDtypeStruct((b, s, HKV), BF16), jax.ShapeDtypeStruct((b, HKV, s), BF16),
              jax.ShapeDtypeStruct((b, HQ, s), BF16)]
    specs = [_tok_spec(D_MODEL), _tok_spec(HQ), _nat_spec(HKV), _tok_spec(HKV), _tok_spec(HQ)]
    return shapes, specs


def _b_outs(b, s):
    shapes = [jax.ShapeDtypeStruct((b, HQ, s), BF16), jax.ShapeDtypeStruct((b, N_BRANCH * HQ, s), BF16)]
    specs = [_tok_spec(HQ), _tok_spec(N_BRANCH * HQ)]
    return shapes, specs


def _specs_for(arrs):
    return [_const_spec(a.shape) for a in arrs]


def _head_rows(h, g):
    return slice((h * GROUP + g) * HEAD_DIM, (h * GROUP + g + 1) * HEAD_DIM)


def _kv_rows(h):
    return slice(h * HEAD_DIM, (h + 1) * HEAD_DIM)


def _head_lanes(h):
    return slice(h * GL, (h + 1) * GL)


def _pair_lanes(p):
    return slice(2 * p * GL, 2 * (p + 1) * GL)


def _pair_cols(p):
    return slice(p * PAIR, (p + 1) * PAIR)


def _padded_q(q_ref, h):
    qs = jnp.concatenate([q_ref[0, _head_rows(h, g), :] for g in range(GROUP)], axis=1)
    zeros = jnp.zeros_like(qs)
    return jnp.concatenate([qs, zeros] if h % 2 == 0 else [zeros, qs], axis=0)


def _with_ones(v):
    return jnp.concatenate([v, jnp.ones((ONES_ROWS, v.shape[1]), BF16)], axis=0)


def _exp2_bf16(x):
    return jnp.exp2(x.astype(BF16))


def _swa_kernel(q_ref, kp_ref, kc_ref, vp_ref, vc_ref, g_ref, bias_ref, sink_ref, o_ref):
    pairs = []
    for p in range(N_KV_HEADS // 2):
        k = jnp.concatenate([kp_ref[0, :, _pair_cols(p)], kc_ref[0, :, _pair_cols(p)]], axis=0)
        qp = jnp.concatenate([_padded_q(q_ref, 2 * p), _padded_q(q_ref, 2 * p + 1)], axis=1)
        pairs.append(_dot(k, qp))
    s = jnp.concatenate(pairs, axis=1) + bias_ref[0]
    sink = sink_ref[...]
    m = jnp.maximum(jnp.max(s, axis=0, keepdims=True), sink)
    e = _exp2_bf16(s - m)
    e_sink = jnp.exp2(sink - m)
    for h in range(N_KV_HEADS):
        v = jnp.concatenate([vp_ref[0, _kv_rows(h), :], vc_ref[0, _kv_rows(h), :]], axis=1)
        acc = _dot(_with_ones(v), e[:, _head_lanes(h)])
        o = acc[:HEAD_DIM] / (acc[HEAD_DIM:HEAD_DIM + 1] + e_sink[:, _head_lanes(h)])
        for g in range(GROUP):
            rows = _head_rows(h, g)
            o_ref[0, rows, :] = (o[:, g * TQ:(g + 1) * TQ] * g_ref[0, rows, :].astype(F32)).astype(BF16)


def _swa_attention(qt, k, vt, gt, swa_tab, sink_row):
    b, _, s = qt.shape
    nq = s // TQ
    prev = lambda i: jnp.maximum(i - 1, 0)
    return pl.pallas_call(
        _swa_kernel, name="swa_attention", grid=(b, nq),
        in_specs=[
            pl.BlockSpec((1, HQ, TQ), lambda b_, i: (b_, 0, i)),
            pl.BlockSpec((1, TQ, HKV), lambda b_, i: (b_, prev(i), 0)),
            pl.BlockSpec((1, TQ, HKV), lambda b_, i: (b_, i, 0)),
            pl.BlockSpec((1, HKV, TQ), lambda b_, i: (b_, 0, prev(i))),
            pl.BlockSpec((1, HKV, TQ), lambda b_, i: (b_, 0, i)),
            pl.BlockSpec((1, HQ, TQ), lambda b_, i: (b_, 0, i)),
            pl.BlockSpec((1, NEAR, N_KV_HEADS * GL), lambda b_, i: (jnp.minimum(i, 1), 0, 0)),
            pl.BlockSpec((1, N_KV_HEADS * GL), lambda b_, i: (0, 0)),
        ],
        out_specs=pl.BlockSpec((1, HQ, TQ), lambda b_, i: (b_, 0, i)),
        out_shape=jax.ShapeDtypeStruct((b, HQ, s), BF16),
        compiler_params=pltpu.CompilerParams(
            dimension_semantics=("parallel", "arbitrary"), vmem_limit_bytes=VMEM_LIMIT),
    )(qt, k, k, vt, vt, gt, swa_tab, sink_row)


def _nsa_kernel(nq, q_ref, g_ref, kcmp_ref, vcmp_ref, kslc_ref, vslc_ref, kwin_ref, vwin_ref,
                cmpb_ref, near_ref, wfar_ref, t31_ref, ovt_ref, kauxf_ref, kauxn_ref, o_ref,
                qp_ref, m_ref, acc_ref, selb_ref, part_ref, sc0_ref, sc1_ref):
    i = pl.program_id(1)
    ncp = kcmp_ref.shape[2]
    nsel = ovt_ref.shape[0]
    n_pairs = N_KV_HEADS // 2
    wide = N_KV_HEADS * GL
    cmp_start = pl.multiple_of((TQ // CMP_STRIDE) * (nq - 1 - i), 8)
    prev_rows = pl.ds(pl.multiple_of(jnp.maximum(i - 1, 0) * TQ, TQ), TQ)
    cur_rows = pl.ds(pl.multiple_of(i * TQ, TQ), TQ)
    far_starts = [pl.multiple_of(jnp.maximum(i - back, 0) * TQ, TQ) for back in range(NSA_WINDOW // TQ, 1, -1)]
    n_far_blocks = (TQ // SEL_BLOCK) * (i - 1)

    def near_keys(k_ref, p):
        return jnp.concatenate([k_ref[0, prev_rows, _pair_cols(p)], k_ref[0, cur_rows, _pair_cols(p)]], axis=0)

    def near_values(v_ref, h):
        return jnp.concatenate([v_ref[0, _kv_rows(h), prev_rows], v_ref[0, _kv_rows(h), cur_rows]], axis=1)

    def scores(lhs_of_pair, rows):
        return jnp.concatenate([_dot(lhs_of_pair(p), qp_ref[rows, _pair_lanes(p)]) for p in range(n_pairs)], axis=1)

    for h in range(N_KV_HEADS):
        qp_ref[:PAIR, _head_lanes(h)] = _padded_q(q_ref, h)
    q_rows = slice(0, PAIR)
    s_cmp = scores(lambda p: kcmp_ref[0, 0, :, _pair_cols(p)], q_rows)
    s_wnear = scores(lambda p: near_keys(kwin_ref, p), q_rows)
    s_wfar = scores(lambda p: jnp.concatenate([kwin_ref[0, pl.ds(st, TQ), _pair_cols(p)] for st in far_starts], axis=0),
                    q_rows)

    bc = cmpb_ref[pl.ds(cmp_start, ncp), :]
    sc = s_cmp + bc
    valid = bc > 0.5 * NEG
    mc = jnp.max(sc, axis=0, keepdims=True)
    ec = jnp.where(valid, jnp.exp2(sc - mc), 0.0)
    lc = jnp.sum(ec, axis=0, keepdims=True)
    pc = ec / jnp.where(lc > 0.0, lc, 1.0)
    pc_b = pc.astype(BF16)
    o_cmp = [_dot(vcmp_ref[0, 0, _kv_rows(h), :], pc_b[:, _head_lanes(h)]) for h in range(N_KV_HEADS)]

    psum = jnp.concatenate(
        [sum(pc[:, h * GL + g * TQ:h * GL + (g + 1) * TQ] for g in range(GROUP)) for h in range(N_KV_HEADS)], axis=1)
    ovt = ovt_ref[...]
    p1 = psum.astype(BF16)
    r1 = psum - p1.astype(F32)
    p2 = r1.astype(BF16)
    p3 = (r1 - p2.astype(F32)).astype(BF16)
    imp = _dot(ovt, p1) + _dot(ovt, p2) + _dot(ovt, p3)
    lane = lax.broadcasted_iota(jnp.int32, imp.shape, 1)
    pos = i * TQ + lane % TQ
    blk = lax.broadcasted_iota(jnp.int32, imp.shape, 0)
    causal = blk * SEL_BLOCK <= pos
    rel = pos // SEL_BLOCK - blk
    forced = (blk == 0) | ((rel >= 0) & (rel < SEL_FORCE_LOCAL))
    score = jnp.where(causal, imp + jnp.where(forced, FORCE_BONUS, 0.0), NEG)
    cnt = jnp.zeros(imp.shape, jnp.int32)
    for jp in range(nsel):
        row = score[jp:jp + 1, :]
        beats = (row > score) | ((row == score) & (blk > jp))
        cnt = cnt + beats.astype(jnp.int32)
    sel = cnt < min(SEL_TOP, nsel)

    def per_group(x):
        return jnp.concatenate([x[:, h * TQ:(h + 1) * TQ] for h in range(N_KV_HEADS) for _ in range(GROUP)], axis=1)

    selb_ref[...] = per_group(jnp.where(sel, 0.0, NEG))
    sel_far = per_group(jnp.where(sel & (blk < n_far_blocks), 0.0, NEG))
    t31 = t31_ref[...]
    hi = t31.astype(BF16).astype(F32)
    row8 = lax.broadcasted_iota(jnp.int32, (8, wide), 0)
    const_rows = jnp.where(row8 == 0, hi, jnp.where(row8 == 1, t31 - hi, 0.0))
    first_blk = (TQ // SEL_BLOCK) * (i - 1)
    near_rows = [selb_ref[pl.ds(jnp.maximum(first_blk + u, 0), 1), :] for u in range(NEAR // SEL_BLOCK)]
    aux = jnp.concatenate(
        [sel_far, const_rows] + near_rows + [jnp.zeros((PAIR - nsel - 8 - NEAR // SEL_BLOCK, wide), F32)], axis=0)
    qp_ref[PAIR:, :] = aux.astype(BF16)

    all_rows = slice(0, 2 * PAIR)
    s_snear = scores(lambda p: jnp.concatenate([near_keys(kslc_ref, p), kauxn_ref[...]], axis=1), all_rows)

    sc_refs = (sc0_ref, sc1_ref)

    def sweep_scores(step, dst_ref, cols, lanes):
        keys = slice(step * FAR, (step + 1) * FAR)
        lhs = jnp.concatenate([kslc_ref[0, keys, cols], kauxf_ref[keys, :]], axis=1)
        dst_ref[:, lanes] = _dot(lhs, qp_ref[:, lanes])

    for p in range(n_pairs):
        sweep_scores(0, sc_refs[0], _pair_cols(p), _pair_lanes(p))

    near_tab = near_ref[0]
    s = s_snear + near_tab
    m = jnp.max(s, axis=0, keepdims=True)
    m_ref[...] = m
    e = _exp2_bf16(s - m)
    for h in range(N_KV_HEADS):
        acc_ref[h] = _dot(_with_ones(near_values(vslc_ref, h)), e[:, _head_lanes(h)])

    s_near = s_wnear + near_tab
    s_far = s_wfar + wfar_ref[0]
    mw = jnp.maximum(jnp.max(s_near, axis=0, keepdims=True), jnp.max(s_far, axis=0, keepdims=True))
    e_near = _exp2_bf16(s_near - mw)
    e_far = _exp2_bf16(s_far - mw)
    for h in range(N_KV_HEADS):
        far_v = jnp.concatenate([vwin_ref[0, _kv_rows(h), pl.ds(st, TQ)] for st in far_starts], axis=1)
        acc_w = (_dot(_with_ones(near_values(vwin_ref, h)), e_near[:, _head_lanes(h)])
                 + _dot(_with_ones(far_v), e_far[:, _head_lanes(h)]))
        o_win = acc_w[:HEAD_DIM] / acc_w[HEAD_DIM:HEAD_DIM + 1]
        for g in range(GROUP):
            rows = _head_rows(h, g)
            lanes = slice(g * TQ, (g + 1) * TQ)
            part_ref[rows, :] = (o_cmp[h][:, lanes] * g_ref[0, 0, rows, :].astype(F32)
                                 + o_win[:, lanes] * g_ref[0, 2, rows, :].astype(F32))

    def sweep_consume(step, src_ref, lanes, heads):
        s = src_ref[:, lanes]
        m_old = m_ref[:, lanes]
        m_new = jnp.maximum(m_old, jnp.max(s, axis=0, keepdims=True))
        alpha = jnp.exp2(m_old - m_new)
        e = _exp2_bf16(s - m_new)
        for j, h in enumerate(heads):
            sub = slice(j * GL, (j + 1) * GL)
            v = _with_ones(vslc_ref[0, _kv_rows(h), step * FAR:(step + 1) * FAR])
            acc_ref[h] = alpha[:, sub] * acc_ref[h] + _dot(v, e[:, sub])
        m_ref[:, lanes] = m_new

    n_far_chunks = jnp.maximum(i - 1, 0)
    n_steps = (n_far_chunks + FAR // TQ - 1) // (FAR // TQ)
    max_steps = kslc_ref.shape[1] // FAR
    for step in range(max_steps):
        if step + 1 < max_steps:
            @pl.when(step + 1 < n_steps)
            def _(step=step):
                for h in range(N_KV_HEADS):
                    sweep_consume(step, sc_refs[step % 2], _head_lanes(h), (h,))
                    sweep_scores(step + 1, sc_refs[(step + 1) % 2], _pair_cols(h // 2), _head_lanes(h))

        @pl.when(step + 1 == n_steps)
        def _(step=step):
            sweep_consume(step, sc_refs[step % 2], slice(0, wide), tuple(range(N_KV_HEADS)))

    for h in range(N_KV_HEADS):
        o_slc = acc_ref[h, :HEAD_DIM, :] / acc_ref[h, HEAD_DIM:HEAD_DIM + 1, :]
        for g in range(GROUP):
            rows = _head_rows(h, g)
            o = part_ref[rows, :] + o_slc[:, g * TQ:(g + 1) * TQ] * g_ref[0, 1, rows, :].astype(F32)
            o_ref[0, rows, :] = o.astype(BF16)


def _nsa_attention(qt, gt, cmp_nat, cmp_t, kslc, vslc, kwin, vwin, tabs, ovt):
    b, _, s = qt.shape
    nq = s // TQ
    assert s % FAR == 0
    ncp = s // CMP_STRIDE
    nsel = s // SEL_BLOCK
    near, wfar, t31, cmpb = tabs
    g4 = gt.reshape(b, N_BRANCH, HQ, s)
    assert nsel + 8 + NEAR // SEL_BLOCK <= PAIR
    cols = np.arange(PAIR)[None, :]
    key_blk = np.arange(s)[:, None] // SEL_BLOCK
    kaux_far = jnp.asarray(((cols == key_blk) | (cols == nsel) | (cols == nsel + 1)).astype(np.float32), BF16)
    near_blk = np.arange(NEAR)[:, None] // SEL_BLOCK
    kaux_near = jnp.asarray((cols == nsel + 8 + near_blk).astype(np.float32), BF16)
    full = lambda shape: pl.BlockSpec(shape, lambda b_, i: (0,) * len(shape))
    return pl.pallas_call(
        functools.partial(_nsa_kernel, nq), name="nsa_attention", grid=(b, nq),
        in_specs=[
            pl.BlockSpec((1, HQ, TQ), lambda b_, i: (b_, 0, i)),
            pl.BlockSpec((1, N_BRANCH, HQ, TQ), lambda b_, i: (b_, 0, 0, i)),
            pl.BlockSpec((1, 1, ncp, HKV), lambda b_, i: (0, b_, 0, 0)),
            pl.BlockSpec((1, 1, HKV, ncp), lambda b_, i: (1, b_, 0, 0)),
            pl.BlockSpec((1, s, HKV), lambda b_, i: (b_, 0, 0)),
            pl.BlockSpec((1, HKV, s), lambda b_, i: (b_, 0, 0)),
            pl.BlockSpec((1, s, HKV), lambda b_, i: (b_, 0, 0)),
            pl.BlockSpec((1, HKV, s), lambda b_, i: (b_, 0, 0)),
            full(cmpb.shape),
            pl.BlockSpec((1, NEAR, N_KV_HEADS * GL), lambda b_, i: (jnp.minimum(i, 1), 0, 0)),
            pl.BlockSpec((1, WIN_FAR, N_KV_HEADS * GL),
                         lambda b_, i: (jnp.minimum(i, N_WIN_VARIANTS - 1), 0, 0)),
            full(t31.shape),
            full((nsel, ncp)),
            full((s, PAIR)), full((NEAR, PAIR)),
        ],
        out_specs=pl.BlockSpec((1, HQ, TQ), lambda b_, i: (b_, 0, i)),
        out_shape=jax.ShapeDtypeStruct((b, HQ, s), BF16),
        scratch_shapes=[pltpu.VMEM((2 * PAIR, N_KV_HEADS * GL), BF16),
                        pltpu.VMEM((1, N_KV_HEADS * GL), F32),
                        pltpu.VMEM((N_KV_HEADS, HEAD_DIM + ONES_ROWS, GL), F32),
                        pltpu.VMEM((nsel, N_KV_HEADS * GL), F32),
                        pltpu.VMEM((HQ, TQ), F32),
                        pltpu.VMEM((FAR, N_KV_HEADS * GL), F32), pltpu.VMEM((FAR, N_KV_HEADS * GL), F32)],
        compiler_params=pltpu.CompilerParams(
            dimension_semantics=("parallel", "arbitrary"), vmem_limit_bytes=VMEM_LIMIT),
    )(qt, g4, cmp_nat, cmp_t, kslc, vslc, kwin, vwin, cmpb, near, wfar, t31, ovt, kaux_far, kaux_near)


def _compress_kernel(u_ref, ptop_ref, pbot_ref, wtop_ref, wbot_ref, w2_ref, kg_ref, nat_out, t_out):
    t = pl.program_id(0)
    u = u_ref[0, 0]
    a = _dot((u + ptop_ref[0]).astype(BF16), wtop_ref[0])
    bm = _dot((u + pbot_ref[0]).astype(BF16), wbot_ref[0])
    n = bm.shape[0]
    hid = a + pltpu.roll(bm, n - 1, axis=0)
    out_t = _dot(_silu(hid).astype(BF16), w2_ref[0]).T
    out_t = jnp.where(t == 0, _head_norm_t(out_t, kg_ref[...]), out_t)
    t_out[0, 0] = out_t.astype(BF16)
    nat_out[0, 0] = out_t.T.astype(BF16)


def _compress(craw, cmp_k_pos, cmp_k_w1, cmp_k_w2, cmp_v_pos, cmp_v_w1, cmp_v_w2, k_gain):
    b, _, s, _ = craw.shape
    nch = s // CMP_STRIDE
    wide = CMP_STRIDE * HKV
    u = craw.reshape(b, 2, nch, wide)
    eye = jnp.eye(N_KV_HEADS, dtype=F32)

    def big_w1(w1_half):
        w = w1_half.reshape(CMP_STRIDE, HEAD_DIM, CMP_HIDDEN)
        return jnp.einsum('jdn,hk->jhdkn', w, eye).reshape(wide, N_KV_HEADS * CMP_HIDDEN).astype(BF16)

    def pos_row(p_half):
        return jnp.broadcast_to(p_half[:, None, :], (CMP_STRIDE, N_KV_HEADS, HEAD_DIM)).reshape(1, wide)

    def big_w2(w2):
        return jnp.einsum('nd,hk->hnkd', w2, eye).reshape(N_KV_HEADS * CMP_HIDDEN, HKV).astype(BF16)

    half = CMP_STRIDE * HEAD_DIM
    wtop = jnp.stack([big_w1(cmp_k_w1[:half]), big_w1(cmp_v_w1[:half])])
    wbot = jnp.stack([big_w1(cmp_k_w1[half:]), big_w1(cmp_v_w1[half:])])
    ptop = jnp.stack([pos_row(cmp_k_pos[:CMP_STRIDE]), pos_row(cmp_v_pos[:CMP_STRIDE])])
    pbot = jnp.stack([pos_row(cmp_k_pos[CMP_STRIDE:]), pos_row(cmp_v_pos[CMP_STRIDE:])])
    w2 = jnp.stack([big_w2(cmp_k_w2), big_w2(cmp_v_w2)])
    kg = jnp.broadcast_to(k_gain.astype(F32)[:, None], (HEAD_DIM, nch))
    sel = lambda shape: pl.BlockSpec((1,) + shape, lambda t, b_: (t,) + (0,) * len(shape))
    return pl.pallas_call(
        _compress_kernel, name="compress", grid=(2, b),
        in_specs=[
            pl.BlockSpec((1, 1, nch, wide), lambda t, b_: (b_, t, 0, 0)),
            sel((1, wide)), sel((1, wide)),
            sel((wide, N_KV_HEADS * CMP_HIDDEN)), sel((wide, N_KV_HEADS * CMP_HIDDEN)),
            sel((N_KV_HEADS * CMP_HIDDEN, HKV)),
            pl.BlockSpec((HEAD_DIM, nch), lambda t, b_: (0, 0)),
        ],
        out_specs=[pl.BlockSpec((1, 1, nch, HKV), lambda t, b_: (t, b_, 0, 0)),
                   pl.BlockSpec((1, 1, HKV, nch), lambda t, b_: (t, b_, 0, 0))],
        out_shape=[jax.ShapeDtypeStruct((2, b, nch, HKV), BF16),
                   jax.ShapeDtypeStruct((2, b, HKV, nch), BF16)],
        compiler_params=pltpu.CompilerParams(
            dimension_semantics=("arbitrary", "arbitrary"), vmem_limit_bytes=VMEM_LIMIT),
    )(u, ptop, pbot, wtop, wbot, w2, kg)


def kernel(x, rel_table, a_norm, a_w_in, a_q_gain, a_k_gain, a_sink, a_w_out, kv_norm, kv_w,
           kv_k_gain, cmp_k_pos, cmp_k_w1, cmp_k_w2, cmp_v_pos, cmp_v_w1, cmp_v_w2,
           b_norm, b_w_in, b_q_gain, b_w_out):
    b, s, _ = x.shape
    nq = s // TQ
    n_a = a_w_in.shape[0]
    n_b = b_w_in.shape[0]
    swa_tab, near, wfar, t31, cmpb = _make_tables(rel_table.astype(F32) * LOG2E, nq)
    ovt = _overlap_t(s)

    a_shapes, a_specs = _a_outs(b, s)
    b_shapes, b_specs = _b_outs(b, s)
    xt_shape, xt_spec = a_shapes[0], a_specs[0]

    ws, gains = _a_weights(a_w_in[0], a_q_gain[0], a_k_gain[0])
    consts = [_bcast(a_norm[0])] + ws + gains
    xt, qt, k, vt, gt = _proj_call(
        _first_kernel, "proj_first", b, s, [x] + consts,
        [_nat_spec(D_MODEL)] + _specs_for(consts), a_shapes, a_specs)

    for layer in range(n_a):
        sink_row = jnp.broadcast_to(
            (a_sink[layer].astype(F32) * LOG2E).reshape(N_KV_HEADS, 1, GROUP, 1), (N_KV_HEADS, 1, GROUP, TQ)
        ).reshape(1, N_KV_HEADS * GL)
        ot = _swa_attention(qt, k, vt, gt, swa_tab, sink_row)
        wo = a_w_out[layer].T.astype(BF16)
        if layer + 1 < n_a:
            ws, gains = _a_weights(a_w_in[layer + 1], a_q_gain[layer + 1], a_k_gain[layer + 1])
            consts = [wo, _bcast(a_norm[layer + 1])] + ws + gains
            xt, qt, k, vt, gt = _proj_call(
                _a2a_kernel, "proj_a2a", b, s, [xt, ot] + consts,
                [_tok_spec(D_MODEL), _tok_spec(HQ)] + _specs_for(consts), a_shapes, a_specs)
        else:
            wsb, gb = _b_weights(b_w_in[0], b_q_gain[0])
            consts = ([wo, _bcast(kv_norm), kv_w.T.astype(BF16), _bcast(kv_k_gain[1]), _bcast(kv_k_gain[2]),
                       _bcast(b_norm[0])] + wsb + gb)
            kv_shapes = [jax.ShapeDtypeStruct((b, 2, s, HKV), F32),
                         jax.ShapeDtypeStruct((b, s, HKV), BF16), jax.ShapeDtypeStruct((b, HKV, s), BF16),
                         jax.ShapeDtypeStruct((b, s, HKV), BF16), jax.ShapeDtypeStruct((b, HKV, s), BF16)]
            kv_specs = [pl.BlockSpec((1, 2, TM, HKV), lambda b_, t: (b_, 0, t, 0)),
                        _nat_spec(HKV), _tok_spec(HKV), _nat_spec(HKV), _tok_spec(HKV)]
            xt, craw, kslc, vslc, kwin, vwin, qt, gt = _proj_call(
                _a2b_kernel, "proj_a2b", b, s, [xt, ot] + consts,
                [_tok_spec(D_MODEL), _tok_spec(HQ)] + _specs_for(consts),
                [xt_shape] + kv_shapes + b_shapes, [xt_spec] + kv_specs + b_specs)

    cmp_nat, cmp_t = _compress(craw, cmp_k_pos, cmp_k_w1, cmp_k_w2, cmp_v_pos, cmp_v_w1, cmp_v_w2,
                               kv_k_gain[0])
    tabs = (near, wfar, t31, cmpb)
    for layer in range(n_b):
        ot = _nsa_attention(qt, gt, cmp_nat, cmp_t, kslc, vslc, kwin, vwin, tabs, ovt)
        wo = b_w_out[layer].T.astype(BF16)
        if layer + 1 < n_b:
            wsb, gb = _b_weights(b_w_in[layer + 1], b_q_gain[layer + 1])
            consts = [wo, _bcast(b_norm[layer + 1])] + wsb + gb
            xt, qt, gt = _proj_call(
                _b2b_kernel, "proj_b2b", b, s, [xt, ot] + consts,
                [_tok_spec(D_MODEL), _tok_spec(HQ)] + _specs_for(consts),
                [xt_shape] + b_shapes, [xt_spec] + b_specs)
        else:
            out = _proj_call(
                _final_kernel, "proj_final", b, s, [xt, ot, wo],
                [_tok_spec(D_MODEL), _tok_spec(HQ), _const_spec(wo.shape)],
                jax.ShapeDtypeStruct((b, s, D_MODEL), F32), _nat_spec(D_MODEL))
    return out
```

```python
import functools
import math

import numpy as np
import jax
import jax.numpy as jnp
from jax import lax
from jax.experimental import pallas as pl
from jax.experimental.pallas import tpu as pltpu

D_MODEL = 1024
HEAD_DIM = 64
N_HEADS = 16
N_KV_HEADS = 4
GROUP = N_HEADS // N_KV_HEADS
HQ = N_HEADS * HEAD_DIM
HKV = N_KV_HEADS * HEAD_DIM
N_BRANCH = 3
SWA_WINDOW = 128
NSA_WINDOW = 512
CMP_LEN = 32
CMP_STRIDE = 16
CMP_HIDDEN = 256
SEL_BLOCK = 64
SEL_TOP = 8
SEL_FORCE_LOCAL = 2
NUM_BUCKETS = 32
MAX_DISTANCE = 128
EPS = 1e-6
NEG = -1e30
FORCE_BONUS = 1e6

TQ = 128
TM = 512
GL = GROUP * TQ
PAIR = 2 * HEAD_DIM
NEAR = 2 * TQ
WIN_FAR = NSA_WINDOW - TQ
N_WIN_VARIANTS = NSA_WINDOW // TQ + 1
FAR = 4 * TQ
ONES_ROWS = 16
LOG2E = math.log2(math.e)
Q_SCALE = HEAD_DIM ** -0.5 * LOG2E
VMEM_LIMIT = 56 * 1024 * 1024

F32 = jnp.float32
BF16 = jnp.bfloat16


def _bucket_np(dist):
    d = np.maximum(dist, 0)
    max_exact = NUM_BUCKETS // 2
    ratio = (np.log(np.maximum(d, 1).astype(np.float32) / np.float32(max_exact))
             / np.float32(math.log(MAX_DISTANCE / max_exact))
             * np.float32(NUM_BUCKETS - max_exact))
    large = np.minimum(max_exact + ratio.astype(np.int32), NUM_BUCKETS - 1)
    return np.where(d < max_exact, d, large).astype(np.int32)


def _dist_vector(table, d_lo, length, hi_valid):
    d = d_lo + np.arange(length)
    v = table.T[:, _bucket_np(d)]
    return jnp.where(jnp.asarray((d >= 0) & (d < hi_valid))[None], v, NEG)


def _skew(v, n, step):
    length = v.shape[1]
    assert step * (n - 1) + TQ <= length
    width = length + step
    reps = -(-(n * width) // length)
    w = jnp.tile(v, (1, reps))[:, :n * width].reshape(v.shape[0], n, width)
    return w[:, :, :TQ]


def _toeplitz(table, n_rows, d_first, step, hi_valid):
    length = step * (n_rows - 1) + TQ
    v = _dist_vector(table, d_first - step * (n_rows - 1), length, hi_valid)
    t = _skew(v, n_rows, step)[:, ::-1, :]
    t = t.reshape(N_KV_HEADS, GROUP, n_rows, TQ).transpose(0, 2, 1, 3)
    return t.reshape(N_KV_HEADS, n_rows, GL).astype(F32)


def _mask_rows(t, n_masked):
    rows = np.arange(t.shape[1])[None, :, None] < n_masked
    return jnp.where(jnp.asarray(rows), NEG, t)


def _make_tables(table, nq):
    big = 1 << 30
    near = _toeplitz(table, NEAR, TQ, 1, big)
    near = jnp.stack([_mask_rows(near, TQ), near])
    swa = _toeplitz(table, NEAR, TQ, 1, SWA_WINDOW)
    swa = jnp.stack([_mask_rows(swa, TQ), swa])
    t31 = _toeplitz(table, 1, MAX_DISTANCE, 1, big)

    def saturated(rows):
        return jnp.broadcast_to(t31, (N_KV_HEADS, rows, GL))

    wfar = jnp.concatenate([_toeplitz(table, TQ, NSA_WINDOW, 1, NSA_WINDOW), saturated(WIN_FAR - TQ)], axis=1)
    wfar = jnp.stack([_mask_rows(wfar, min(WIN_FAR, (N_WIN_VARIANTS - 1 - v) * TQ))
                      for v in range(N_WIN_VARIANTS)])
    per = TQ // CMP_STRIDE
    ncp = nq * per
    off = per * (nq - 1)
    sat_c = -(-(MAX_DISTANCE + CMP_LEN - 1) // CMP_STRIDE)
    assert off >= sat_c
    band = _toeplitz(table, sat_c + per - 1, CMP_STRIDE * (sat_c - 1) - (CMP_LEN - 1), CMP_STRIDE, big)
    cmpb = jnp.concatenate(
        [saturated(off - sat_c + 1), band, jnp.full((N_KV_HEADS, ncp - per, GL), NEG, F32)], axis=1)

    def widen(t):
        t = jnp.swapaxes(t, -3, -2)
        return t.reshape(t.shape[:-2] + (N_KV_HEADS * GL,))

    return widen(swa), widen(near), widen(wfar), widen(t31), widen(cmpb)


def _overlap_t(s):
    ncp = s // CMP_STRIDE
    nsel = s // SEL_BLOCK
    cs = np.arange(ncp)[None, :] * CMP_STRIDE
    ss = np.arange(nsel)[:, None] * SEL_BLOCK
    ov = (cs < ss + SEL_BLOCK) & (cs + CMP_LEN > ss) & (np.arange(ncp)[None, :] < ncp - 1)
    return jnp.asarray(ov.astype(np.float32), dtype=BF16)


def _lanes(gain_b, n):
    return jnp.concatenate([gain_b] * (n // gain_b.shape[1]), axis=1)


def _rms_t(xt, gain_b):
    ms = jnp.mean(xt * xt, axis=0, keepdims=True)
    return (xt * lax.rsqrt(ms + EPS) * _lanes(gain_b, xt.shape[1])).astype(BF16)


def _head_norm_t(a, gain_b):
    outs = []
    for h in range(a.shape[0] // HEAD_DIM):
        blk = a[h * HEAD_DIM:(h + 1) * HEAD_DIM]
        ms = jnp.mean(blk * blk, axis=0, keepdims=True)
        outs.append(blk * lax.rsqrt(ms + EPS) * _lanes(gain_b, a.shape[1]))
    return jnp.concatenate(outs, axis=0)


def _sigmoid_h(hz):
    return 0.5 + 0.5 * jnp.tanh(hz)


def _silu_h(hz):
    return hz + hz * jnp.tanh(hz)


def _silu(z):
    return _silu_h(0.5 * z)


def _dot(a, b):
    return jnp.dot(a, b, preferred_element_type=F32)


def _proj_a(xn, w, qg, kg, q_out, k_out, v_out, g_out):
    pq = _dot(w[:HQ, :], xn)
    pkv = _dot(w[HQ:HQ + 2 * HKV, :], xn)
    q_out[0] = _head_norm_t(pq, qg[...]).astype(BF16)
    pz = _dot(w[HQ + 2 * HKV:, :], xn)
    k_out[0] = _head_norm_t(pkv[:HKV], kg[...]).T.astype(BF16)
    v_out[0] = pkv[HKV:].astype(BF16)
    g_out[0] = _silu_h(pz).astype(BF16)


def _proj_b(xn, w, qg, q_out, g_out):
    n_gate = N_BRANCH * N_HEADS
    pq = _dot(w[:HQ, :], xn)
    sg = _sigmoid_h(_dot(w[HQ:HQ + n_gate, :], xn))

    def gate_rows(pz, c):
        for hd in range(N_HEADS):
            r = c * N_HEADS + hd
            z = pz[hd * HEAD_DIM:(hd + 1) * HEAD_DIM]
            g_out[0, r * HEAD_DIM:(r + 1) * HEAD_DIM, :] = (_silu_h(z) * sg[r:r + 1]).astype(BF16)

    z0 = HQ + n_gate
    pz = _dot(w[z0:z0 + HQ, :], xn)
    q_out[0] = _head_norm_t(pq, qg[...]).astype(BF16)
    for c in range(1, N_BRANCH):
        nxt = _dot(w[z0 + c * HQ:z0 + (c + 1) * HQ, :], xn)
        gate_rows(pz, c - 1)
        pz = nxt
    gate_rows(pz, N_BRANCH - 1)


def _proj_kv(xn, wkv, kg1, kg2, craw_out, kslc_out, vslc_out, kwin_out, vwin_out):
    p = _dot(wkv[...], xn)

    def rows(n):
        return p[n * HKV:(n + 1) * HKV]
    for t in range(2):
        nat = rows(t).T
        for pr in range(HKV // PAIR):
            craw_out[0, t, pr] = nat[:, _pair_cols(pr)]
    kslc_out[0] = _head_norm_t(rows(2), kg1[...]).T.astype(BF16)
    vslc_out[0] = rows(3).astype(BF16)
    kwin_out[0] = _head_norm_t(rows(4), kg2[...]).T.astype(BF16)
    vwin_out[0] = rows(5).astype(BF16)


def _first_kernel(x_ref, ng, w, qg, kg, xt_out, q_out, k_out, v_out, g_out):
    xt = x_ref[0].T
    xt_out[0] = xt
    _proj_a(_rms_t(xt, ng[...]), w, qg, kg, q_out, k_out, v_out, g_out)


def _a2a_kernel(x_ref, o_ref, wo, ng, w, qg, kg, xt_out, q_out, k_out, v_out, g_out):
    xt = x_ref[0] + _dot(wo[...], o_ref[0])
    xt_out[0] = xt
    _proj_a(_rms_t(xt, ng[...]), w, qg, kg, q_out, k_out, v_out, g_out)


def _a2b_kernel(x_ref, o_ref, wo, ngkv, wkv, kg1, kg2, ngb, w, qg,
                xt_out, craw_out, kslc_out, vslc_out, kwin_out, vwin_out, q_out, g_out):
    xt = x_ref[0] + _dot(wo[...], o_ref[0])
    xt_out[0] = xt
    _proj_kv(_rms_t(xt, ngkv[...]), wkv, kg1, kg2, craw_out, kslc_out, vslc_out, kwin_out, vwin_out)
    _proj_b(_rms_t(xt, ngb[...]), w, qg, q_out, g_out)


def _b2b_kernel(x_ref, o_ref, wo, ngb, w, qg, xt_out, q_out, g_out):
    xt = x_ref[0] + _dot(wo[...], o_ref[0])
    xt_out[0] = xt
    _proj_b(_rms_t(xt, ngb[...]), w, qg, q_out, g_out)


def _final_kernel(x_ref, o_ref, wo, x_out):
    x_out[0] = (x_ref[0] + _dot(wo[...], o_ref[0])).T


def _tok_spec(rows):
    return pl.BlockSpec((1, rows, TM), lambda b, t: (b, 0, t))


def _nat_spec(cols):
    return pl.BlockSpec((1, TM, cols), lambda b, t: (b, t, 0))


def _const_spec(shape):
    nd = len(shape)
    return pl.BlockSpec(shape, lambda b, t: (0,) * nd, pipeline_mode=pl.Buffered(1))


def _proj_call(body, name, b, s, ins, in_specs, out_shapes, out_specs):
    return pl.pallas_call(
        body, name=name, grid=(b, s // TM),
        in_specs=in_specs, out_specs=out_specs, out_shape=out_shapes,
        compiler_params=pltpu.CompilerParams(
            dimension_semantics=("parallel", "parallel"), vmem_limit_bytes=VMEM_LIMIT),
    )(*ins)


def _bcast(v, scale=1.0):
    return jnp.broadcast_to((v.astype(F32) * scale)[:, None], (v.shape[0], TQ))


def _a_weights(w_in, q_gain, k_gain):
    halve = np.where(np.arange(w_in.shape[1]) >= HQ + 2 * HKV, 0.5, 1.0).astype(np.float32)
    ws = [(w_in * halve).T.astype(BF16)]
    gains = [_bcast(q_gain, Q_SCALE), _bcast(k_gain)]
    return ws, gains


def _b_weights(w_in, q_gain):
    halve = np.where(np.arange(w_in.shape[1]) >= HQ, 0.5, 1.0).astype(np.float32)
    ws = [(w_in * halve).T.astype(BF16)]
    return ws, [_bcast(q_gain, Q_SCALE)]


def _a_outs(b, s):
    shapes = [jax.ShapeDtypeStruct((b, D_MODEL, s), F32), jax.ShapeDtypeStruct((b, HQ, s), BF16),
              jax.ShapeDtypeStruct((b, s, HKV), BF16), jax.ShapeDtypeStruct((b, HKV, s), BF16),
              jax.ShapeDtypeStruct((b, HQ, s), BF16)]
    specs = [_tok_spec(D_MODEL), _tok_spec(HQ), _nat_spec(HKV), _tok_spec(HKV), _tok_spec(HQ)]
    return shapes, specs


def _b_outs(b, s):
    shapes = [jax.ShapeDtypeStruct((b, HQ, s), BF16), jax.ShapeDtypeStruct((b, N_BRANCH * HQ, s), BF16)]
    specs = [_tok_spec(HQ), _tok_spec(N_BRANCH * HQ)]
    return shapes, specs


def _specs_for(arrs):
    return [_const_spec(a.shape) for a in arrs]


def _head_rows(h, g):
    return slice((h * GROUP + g) * HEAD_DIM, (h * GROUP + g + 1) * HEAD_DIM)


def _kv_rows(h):
    return slice(h * HEAD_DIM, (h + 1) * HEAD_DIM)


def _head_lanes(h):
    return slice(h * GL, (h + 1) * GL)


def _pair_lanes(p):
    return slice(2 * p * GL, 2 * (p + 1) * GL)


def _pair_cols(p):
    return slice(p * PAIR, (p + 1) * PAIR)


def _padded_q(q_ref, h):
    qs = jnp.concatenate([q_ref[0, _head_rows(h, g), :] for g in range(GROUP)], axis=1)
    zeros = jnp.zeros_like(qs)
    return jnp.concatenate([qs, zeros] if h % 2 == 0 else [zeros, qs], axis=0)


def _with_ones(v):
    return jnp.concatenate([v, jnp.ones((ONES_ROWS, v.shape[1]), BF16)], axis=0)


def _exp2_bf16(x):
    return jnp.exp2(x.astype(BF16))


def _swa_kernel(q_ref, kp_ref, kc_ref, vp_ref, vc_ref, g_ref, bias_ref, sink_ref, o_ref):
    pairs = []
    for p in range(N_KV_HEADS // 2):
        k = jnp.concatenate([kp_ref[0, :, _pair_cols(p)], kc_ref[0, :, _pair_cols(p)]], axis=0)
        qp = jnp.concatenate([_padded_q(q_ref, 2 * p), _padded_q(q_ref, 2 * p + 1)], axis=1)
        pairs.append(_dot(k, qp))
    s = jnp.concatenate(pairs, axis=1) + bias_ref[0]
    sink = sink_ref[...]
    m = jnp.maximum(jnp.max(s, axis=0, keepdims=True), sink)
    e = _exp2_bf16(s - m)
    e_sink = jnp.exp2(sink - m)
    for h in range(N_KV_HEADS):
        v = jnp.concatenate([vp_ref[0, _kv_rows(h), :], vc_ref[0, _kv_rows(h), :]], axis=1)
        acc = _dot(_with_ones(v), e[:, _head_lanes(h)])
        o = acc[:HEAD_DIM] / (acc[HEAD_DIM:HEAD_DIM + 1] + e_sink[:, _head_lanes(h)])
        for g in range(GROUP):
            rows = _head_rows(h, g)
            o_ref[0, rows, :] = (o[:, g * TQ:(g + 1) * TQ] * g_ref[0, rows, :].astype(F32)).astype(BF16)


def _swa_attention(qt, k, vt, gt, swa_tab, sink_row):
    b, _, s = qt.shape
    nq = s // TQ
    prev = lambda i: jnp.maximum(i - 1, 0)
    return pl.pallas_call(
        _swa_kernel, name="swa_attention", grid=(b, nq),
        in_specs=[
            pl.BlockSpec((1, HQ, TQ), lambda b_, i: (b_, 0, i)),
            pl.BlockSpec((1, TQ, HKV), lambda b_, i: (b_, prev(i), 0)),
            pl.BlockSpec((1, TQ, HKV), lambda b_, i: (b_, i, 0)),
            pl.BlockSpec((1, HKV, TQ), lambda b_, i: (b_, 0, prev(i))),
            pl.BlockSpec((1, HKV, TQ), lambda b_, i: (b_, 0, i)),
            pl.BlockSpec((1, HQ, TQ), lambda b_, i: (b_, 0, i)),
            pl.BlockSpec((1, NEAR, N_KV_HEADS * GL), lambda b_, i: (jnp.minimum(i, 1), 0, 0)),
            pl.BlockSpec((1, N_KV_HEADS * GL), lambda b_, i: (0, 0)),
        ],
        out_specs=pl.BlockSpec((1, HQ, TQ), lambda b_, i: (b_, 0, i)),
        out_shape=jax.ShapeDtypeStruct((b, HQ, s), BF16),
        compiler_params=pltpu.CompilerParams(
            dimension_semantics=("parallel", "arbitrary"), vmem_limit_bytes=VMEM_LIMIT),
    )(qt, k, k, vt, vt, gt, swa_tab, sink_row)


def _nsa_kernel(nq, q_ref, g_ref, kcmp_ref, vcmp_ref, kslc_ref, vslc_ref, kwin_ref, vwin_ref,
                cmpb_ref, near_ref, wfar_ref, t31_ref, ovt_ref, kauxf_ref, kauxn_ref, o_ref,
                qp_ref, m_ref, acc_ref, selb_ref, part_ref, sc0_ref, sc1_ref):
    i = pl.program_id(1)
    ncp = kcmp_ref.shape[2]
    nsel = ovt_ref.shape[0]
    n_pairs = N_KV_HEADS // 2
    wide = N_KV_HEADS * GL
    cmp_start = pl.multiple_of((TQ // CMP_STRIDE) * (nq - 1 - i), 8)
    prev_rows = pl.ds(pl.multiple_of(jnp.maximum(i - 1, 0) * TQ, TQ), TQ)
    cur_rows = pl.ds(pl.multiple_of(i * TQ, TQ), TQ)
    far_starts = [pl.multiple_of(jnp.maximum(i - back, 0) * TQ, TQ) for back in range(NSA_WINDOW // TQ, 1, -1)]
    n_far_blocks = (TQ // SEL_BLOCK) * (i - 1)

    def near_keys(k_ref, p):
        return jnp.concatenate([k_ref[0, prev_rows, _pair_cols(p)], k_ref[0, cur_rows, _pair_cols(p)]], axis=0)

    def near_values(v_ref, h):
        return jnp.concatenate([v_ref[0, _kv_rows(h), prev_rows], v_ref[0, _kv_rows(h), cur_rows]], axis=1)

    def scores(lhs_of_pair, rows):
        return jnp.concatenate([_dot(lhs_of_pair(p), qp_ref[rows, _pair_lanes(p)]) for p in range(n_pairs)], axis=1)

    for h in range(N_KV_HEADS):
        qp_ref[:PAIR, _head_lanes(h)] = _padded_q(q_ref, h)
    q_rows = slice(0, PAIR)
    near_tab = near_ref[0]
    bc = cmpb_ref[pl.ds(cmp_start, ncp), :]
    sc = scores(lambda p: kcmp_ref[0, 0, :, _pair_cols(p)], q_rows) + bc
    mc = jnp.max(sc, axis=0, keepdims=True)
    s_near = scores(lambda p: near_keys(kwin_ref, p), q_rows) + near_tab
    s_far = scores(lambda p: jnp.concatenate([kwin_ref[0, pl.ds(st, TQ), _pair_cols(p)] for st in far_starts], axis=0),
                   q_rows) + wfar_ref[0]
    mw = jnp.maximum(jnp.max(s_near, axis=0, keepdims=True), jnp.max(s_far, axis=0, keepdims=True))

    valid = bc > 0.5 * NEG
    ec = jnp.where(valid, jnp.exp2(sc - mc), 0.0)
    lc = jnp.sum(ec, axis=0, keepdims=True)
    pc = ec / jnp.where(lc > 0.0, lc, 1.0)
    pc_b = pc.astype(BF16)
    o_cmp = [_dot(vcmp_ref[0, 0, _kv_rows(h), :], pc_b[:, _head_lanes(h)]) for h in range(N_KV_HEADS)]

    psum = jnp.concatenate(
        [sum(pc[:, h * GL + g * TQ:h * GL + (g + 1) * TQ] for g in range(GROUP)) for h in range(N_KV_HEADS)], axis=1)
    ovt = ovt_ref[...]
    p1 = psum.astype(BF16)
    r1 = psum - p1.astype(F32)
    p2 = r1.astype(BF16)
    p3 = (r1 - p2.astype(F32)).astype(BF16)
    imp = _dot(ovt, p1) + _dot(ovt, p2) + _dot(ovt, p3)
    lane = lax.broadcasted_iota(jnp.int32, imp.shape, 1)
    pos = i * TQ + lane % TQ
    blk = lax.broadcasted_iota(jnp.int32, imp.shape, 0)
    causal = blk * SEL_BLOCK <= pos
    rel = pos // SEL_BLOCK - blk
    forced = (blk == 0) | ((rel >= 0) & (rel < SEL_FORCE_LOCAL))
    score = jnp.where(causal, imp + jnp.where(forced, FORCE_BONUS, 0.0), NEG)
    groups = [score[r:r + 8] for r in range(0, nsel, 8)]
    counts = [jnp.zeros((8, imp.shape[1]), jnp.int32) for _ in groups]
    row_in_group = lax.broadcasted_iota(jnp.int32, (8, imp.shape[1]), 0)
    for jp in range(nsel):
        row = score[jp:jp + 1, :]
        for gi, grp in enumerate(groups):
            if gi * 8 > jp:
                beats = row >= grp
            elif gi * 8 + 7 < jp:
                beats = row > grp
            else:
                beats = (row > grp) | ((row == grp) & (row_in_group > jp - gi * 8))
            counts[gi] = counts[gi] + beats.astype(jnp.int32)
    sel = jnp.concatenate(counts, axis=0) < min(SEL_TOP, nsel)

    def per_group(x):
        return jnp.concatenate([x[:, h * TQ:(h + 1) * TQ] for h in range(N_KV_HEADS) for _ in range(GROUP)], axis=1)

    selb_ref[...] = per_group(jnp.where(sel, 0.0, NEG))
    sel_far = per_group(jnp.where(sel & (blk < n_far_blocks), 0.0, NEG))
    t31 = t31_ref[...]
    hi = t31.astype(BF16).astype(F32)
    row8 = lax.broadcasted_iota(jnp.int32, (8, wide), 0)
    const_rows = jnp.where(row8 == 0, hi, jnp.where(row8 == 1, t31 - hi, 0.0))
    first_blk = (TQ // SEL_BLOCK) * (i - 1)
    near_rows = [selb_ref[pl.ds(jnp.maximum(first_blk + u, 0), 1), :] for u in range(NEAR // SEL_BLOCK)]
    aux = jnp.concatenate(
        [sel_far, const_rows] + near_rows + [jnp.zeros((PAIR - nsel - 8 - NEAR // SEL_BLOCK, wide), F32)], axis=0)
    qp_ref[PAIR:, :] = aux.astype(BF16)

    all_rows = slice(0, 2 * PAIR)
    s = scores(lambda p: jnp.concatenate([near_keys(kslc_ref, p), kauxn_ref[...]], axis=1), all_rows) + near_tab
    m = jnp.max(s, axis=0, keepdims=True)

    sc_refs = (sc0_ref, sc1_ref)

    def sweep_scores(step, dst_ref, cols, lanes):
        keys = slice(step * FAR, (step + 1) * FAR)
        lhs = jnp.concatenate([kslc_ref[0, keys, cols], kauxf_ref[keys, :]], axis=1)
        sc_new = _dot(lhs, qp_ref[:, lanes])
        dst_ref[:FAR, lanes] = sc_new
        dst_ref[FAR:FAR + 1, lanes] = jnp.max(sc_new, axis=0, keepdims=True)

    for p in range(n_pairs):
        sweep_scores(0, sc_refs[0], _pair_cols(p), _pair_lanes(p))

    m_ref[...] = m
    e = _exp2_bf16(s - m)
    for h in range(N_KV_HEADS):
        acc_ref[h] = _dot(_with_ones(near_values(vslc_ref, h)), e[:, _head_lanes(h)])

    e_near = _exp2_bf16(s_near - mw)
    e_far = _exp2_bf16(s_far - mw)
    for h in range(N_KV_HEADS):
        far_v = jnp.concatenate([vwin_ref[0, _kv_rows(h), pl.ds(st, TQ)] for st in far_starts], axis=1)
        acc_w = (_dot(_with_ones(near_values(vwin_ref, h)), e_near[:, _head_lanes(h)])
                 + _dot(_with_ones(far_v), e_far[:, _head_lanes(h)]))
        o_win = acc_w[:HEAD_DIM] / acc_w[HEAD_DIM:HEAD_DIM + 1]
        for g in range(GROUP):
            rows = _head_rows(h, g)
            lanes = slice(g * TQ, (g + 1) * TQ)
            part_ref[rows, :] = (o_cmp[h][:, lanes] * g_ref[0, 0, rows, :].astype(F32)
                                 + o_win[:, lanes] * g_ref[0, 2, rows, :].astype(F32))

    def sweep_consume(step, src_ref, lanes, heads):
        s = src_ref[:FAR, lanes]
        m_old = m_ref[:, lanes]
        m_new = jnp.maximum(m_old, src_ref[FAR:FAR + 1, lanes])
        alpha = jnp.exp2(m_old - m_new)
        e = _exp2_bf16(s - m_new)
        for j, h in enumerate(heads):
            sub = slice(j * GL, (j + 1) * GL)
            v = _with_ones(vslc_ref[0, _kv_rows(h), step * FAR:(step + 1) * FAR])
            acc_ref[h] = alpha[:, sub] * acc_ref[h] + _dot(v, e[:, sub])
        m_ref[:, lanes] = m_new

    n_far_chunks = jnp.maximum(i - 1, 0)
    n_steps = (n_far_chunks + FAR // TQ - 1) // (FAR // TQ)
    max_steps = kslc_ref.shape[1] // FAR
    for step in range(max_steps):
        if step + 1 < max_steps:
            @pl.when(step + 1 < n_steps)
            def _(step=step):
                for h in range(N_KV_HEADS):
                    sweep_consume(step, sc_refs[step % 2], _head_lanes(h), (h,))
                    sweep_scores(step + 1, sc_refs[(step + 1) % 2], _pair_cols(h // 2), _head_lanes(h))

        @pl.when(step + 1 == n_steps)
        def _(step=step):
            sweep_consume(step, sc_refs[step % 2], slice(0, wide), tuple(range(N_KV_HEADS)))

    for h in range(N_KV_HEADS):
        o_slc = acc_ref[h, :HEAD_DIM, :] / acc_ref[h, HEAD_DIM:HEAD_DIM + 1, :]
        for g in range(GROUP):
            rows = _head_rows(h, g)
            o = part_ref[rows, :] + o_slc[:, g * TQ:(g + 1) * TQ] * g_ref[0, 1, rows, :].astype(F32)
            o_ref[0, rows, :] = o.astype(BF16)


def _nsa_attention(qt, gt, cmp_nat, cmp_t, kslc, vslc, kwin, vwin, tabs, ovt):
    b, _, s = qt.shape
    nq = s // TQ
    assert s % FAR == 0
    ncp = s // CMP_STRIDE
    nsel = s // SEL_BLOCK
    near, wfar, t31, cmpb = tabs
    g4 = gt.reshape(b, N_BRANCH, HQ, s)
    assert nsel + 8 + NEAR // SEL_BLOCK <= PAIR
    cols = np.arange(PAIR)[None, :]
    key_blk = np.arange(s)[:, None] // SEL_BLOCK
    kaux_far = jnp.asarray(((cols == key_blk) | (cols == nsel) | (cols == nsel + 1)).astype(np.float32), BF16)
    near_blk = np.arange(NEAR)[:, None] // SEL_BLOCK
    kaux_near = jnp.asarray((cols == nsel + 8 + near_blk).astype(np.float32), BF16)
    full = lambda shape: pl.BlockSpec(shape, lambda b_, i: (0,) * len(shape))
    return pl.pallas_call(
        functools.partial(_nsa_kernel, nq), name="nsa_attention", grid=(b, nq),
        in_specs=[
            pl.BlockSpec((1, HQ, TQ), lambda b_, i: (b_, 0, i)),
            pl.BlockSpec((1, N_BRANCH, HQ, TQ), lambda b_, i: (b_, 0, 0, i)),
            pl.BlockSpec((1, 1, ncp, HKV), lambda b_, i: (0, b_, 0, 0)),
            pl.BlockSpec((1, 1, HKV, ncp), lambda b_, i: (1, b_, 0, 0)),
            pl.BlockSpec((1, s, HKV), lambda b_, i: (b_, 0, 0)),
            pl.BlockSpec((1, HKV, s), lambda b_, i: (b_, 0, 0)),
            pl.BlockSpec((1, s, HKV), lambda b_, i: (b_, 0, 0)),
            pl.BlockSpec((1, HKV, s), lambda b_, i: (b_, 0, 0)),
            full(cmpb.shape),
            pl.BlockSpec((1, NEAR, N_KV_HEADS * GL), lambda b_, i: (jnp.minimum(i, 1), 0, 0)),
            pl.BlockSpec((1, WIN_FAR, N_KV_HEADS * GL),
                         lambda b_, i: (jnp.minimum(i, N_WIN_VARIANTS - 1), 0, 0)),
            full(t31.shape),
            full((nsel, ncp)),
            full((s, PAIR)), full((NEAR, PAIR)),
        ],
        out_specs=pl.BlockSpec((1, HQ, TQ), lambda b_, i: (b_, 0, i)),
        out_shape=jax.ShapeDtypeStruct((b, HQ, s), BF16),
        scratch_shapes=[pltpu.VMEM((2 * PAIR, N_KV_HEADS * GL), BF16),
                        pltpu.VMEM((1, N_KV_HEADS * GL), F32),
                        pltpu.VMEM((N_KV_HEADS, HEAD_DIM + ONES_ROWS, GL), F32),
                        pltpu.VMEM((nsel, N_KV_HEADS * GL), F32),
                        pltpu.VMEM((HQ, TQ), F32),
                        pltpu.VMEM((FAR + 8, N_KV_HEADS * GL), F32), pltpu.VMEM((FAR + 8, N_KV_HEADS * GL), F32)],
        compiler_params=pltpu.CompilerParams(
            dimension_semantics=("parallel", "arbitrary"), vmem_limit_bytes=VMEM_LIMIT),
    )(qt, g4, cmp_nat, cmp_t, kslc, vslc, kwin, vwin, cmpb, near, wfar, t31, ovt, kaux_far, kaux_near)


def _compress_kernel(u_ref, ptop_ref, pbot_ref, wtop_ref, wbot_ref, w2_ref, kg_ref, nat_out, t_out):
    t = pl.program_id(0)
    nch = u_ref.shape[3] // CMP_STRIDE
    u = jnp.concatenate([u_ref[0, 0, p, pl.ds(j, nch, stride=CMP_STRIDE), :]
                         for j in range(CMP_STRIDE) for p in range(HKV // PAIR)], axis=1)
    a = _dot((u + ptop_ref[0]).astype(BF16), wtop_ref[0])
    bm = _dot((u + pbot_ref[0]).astype(BF16), wbot_ref[0])
    n = bm.shape[0]
    hid = a + pltpu.roll(bm, n - 1, axis=0)
    out_t = _dot(_silu(hid).astype(BF16), w2_ref[0]).T
    out_t = jnp.where(t == 0, _head_norm_t(out_t, kg_ref[...]), out_t)
    t_out[0, 0] = out_t.astype(BF16)
    nat_out[0, 0] = out_t.T.astype(BF16)


def _compress(craw, cmp_k_pos, cmp_k_w1, cmp_k_w2, cmp_v_pos, cmp_v_w1, cmp_v_w2, k_gain):
    b, _, _, s, _ = craw.shape
    nch = s // CMP_STRIDE
    wide = CMP_STRIDE * HKV
    eye = jnp.eye(N_KV_HEADS, dtype=F32)

    def big_w1(w1_half):
        w = w1_half.reshape(CMP_STRIDE, HEAD_DIM, CMP_HIDDEN)
        return jnp.einsum('jdn,hk->jhdkn', w, eye).reshape(wide, N_KV_HEADS * CMP_HIDDEN).astype(BF16)

    def pos_row(p_half):
        return jnp.broadcast_to(p_half[:, None, :], (CMP_STRIDE, N_KV_HEADS, HEAD_DIM)).reshape(1, wide)

    def big_w2(w2):
        return jnp.einsum('nd,hk->hnkd', w2, eye).reshape(N_KV_HEADS * CMP_HIDDEN, HKV).astype(BF16)

    half = CMP_STRIDE * HEAD_DIM
    wtop = jnp.stack([big_w1(cmp_k_w1[:half]), big_w1(cmp_v_w1[:half])])
    wbot = jnp.stack([big_w1(cmp_k_w1[half:]), big_w1(cmp_v_w1[half:])])
    ptop = jnp.stack([pos_row(cmp_k_pos[:CMP_STRIDE]), pos_row(cmp_v_pos[:CMP_STRIDE])])
    pbot = jnp.stack([pos_row(cmp_k_pos[CMP_STRIDE:]), pos_row(cmp_v_pos[CMP_STRIDE:])])
    w2 = jnp.stack([big_w2(cmp_k_w2), big_w2(cmp_v_w2)])
    kg = jnp.broadcast_to(k_gain.astype(F32)[:, None], (HEAD_DIM, nch))
    sel = lambda shape: pl.BlockSpec((1,) + shape, lambda t, b_: (t,) + (0,) * len(shape))
    return pl.pallas_call(
        _compress_kernel, name="compress", grid=(2, b),
        in_specs=[
            pl.BlockSpec((1, 1, HKV // PAIR, s, PAIR), lambda t, b_: (b_, t, 0, 0, 0)),
            sel((1, wide)), sel((1, wide)),
            sel((wide, N_KV_HEADS * CMP_HIDDEN)), sel((wide, N_KV_HEADS * CMP_HIDDEN)),
            sel((N_KV_HEADS * CMP_HIDDEN, HKV)),
            pl.BlockSpec((HEAD_DIM, nch), lambda t, b_: (0, 0)),
        ],
        out_specs=[pl.BlockSpec((1, 1, nch, HKV), lambda t, b_: (t, b_, 0, 0)),
                   pl.BlockSpec((1, 1, HKV, nch), lambda t, b_: (t, b_, 0, 0))],
        out_shape=[jax.ShapeDtypeStruct((2, b, nch, HKV), BF16),
                   jax.ShapeDtypeStruct((2, b, HKV, nch), BF16)],
        compiler_params=pltpu.CompilerParams(
            dimension_semantics=("arbitrary", "arbitrary"), vmem_limit_bytes=VMEM_LIMIT),
    )(craw, ptop, pbot, wtop, wbot, w2, kg)


def kernel(x, rel_table, a_norm, a_w_in, a_q_gain, a_k_gain, a_sink, a_w_out, kv_norm, kv_w,
           kv_k_gain, cmp_k_pos, cmp_k_w1, cmp_k_w2, cmp_v_pos, cmp_v_w1, cmp_v_w2,
           b_norm, b_w_in, b_q_gain, b_w_out):
    b, s, _ = x.shape
    nq = s // TQ
    n_a = a_w_in.shape[0]
    n_b = b_w_in.shape[0]
    swa_tab, near, wfar, t31, cmpb = _make_tables(rel_table.astype(F32) * LOG2E, nq)
    ovt = _overlap_t(s)

    a_shapes, a_specs = _a_outs(b, s)
    b_shapes, b_specs = _b_outs(b, s)
    xt_shape, xt_spec = a_shapes[0], a_specs[0]

    ws, gains = _a_weights(a_w_in[0], a_q_gain[0], a_k_gain[0])
    consts = [_bcast(a_norm[0])] + ws + gains
    xt, qt, k, vt, gt = _proj_call(
        _first_kernel, "proj_first", b, s, [x] + consts,
        [_nat_spec(D_MODEL)] + _specs_for(consts), a_shapes, a_specs)

    for layer in range(n_a):
        sink_row = jnp.broadcast_to(
            (a_sink[layer].astype(F32) * LOG2E).reshape(N_KV_HEADS, 1, GROUP, 1), (N_KV_HEADS, 1, GROUP, TQ)
        ).reshape(1, N_KV_HEADS * GL)
        ot = _swa_attention(qt, k, vt, gt, swa_tab, sink_row)
        wo = a_w_out[layer].T.astype(BF16)
        if layer + 1 < n_a:
            ws, gains = _a_weights(a_w_in[layer + 1], a_q_gain[layer + 1], a_k_gain[layer + 1])
            consts = [wo, _bcast(a_norm[layer + 1])] + ws + gains
            xt, qt, k, vt, gt = _proj_call(
                _a2a_kernel, "proj_a2a", b, s, [xt, ot] + consts,
                [_tok_spec(D_MODEL), _tok_spec(HQ)] + _specs_for(consts), a_shapes, a_specs)
        else:
            wsb, gb = _b_weights(b_w_in[0], b_q_gain[0])
            consts = ([wo, _bcast(kv_norm), kv_w.T.astype(BF16), _bcast(kv_k_gain[1]), _bcast(kv_k_gain[2]),
                       _bcast(b_norm[0])] + wsb + gb)
            kv_shapes = [jax.ShapeDtypeStruct((b, 2, HKV // PAIR, s, PAIR), F32),
                         jax.ShapeDtypeStruct((b, s, HKV), BF16), jax.ShapeDtypeStruct((b, HKV, s), BF16),
                         jax.ShapeDtypeStruct((b, s, HKV), BF16), jax.ShapeDtypeStruct((b, HKV, s), BF16)]
            kv_specs = [pl.BlockSpec((1, 2, HKV // PAIR, TM, PAIR), lambda b_, t: (b_, 0, 0, t, 0)),
                        _nat_spec(HKV), _tok_spec(HKV), _nat_spec(HKV), _tok_spec(HKV)]
            xt, craw, kslc, vslc, kwin, vwin, qt, gt = _proj_call(
                _a2b_kernel, "proj_a2b", b, s, [xt, ot] + consts,
                [_tok_spec(D_MODEL), _tok_spec(HQ)] + _specs_for(consts),
                [xt_shape] + kv_shapes + b_shapes, [xt_spec] + kv_specs + b_specs)

    cmp_nat, cmp_t = _compress(craw, cmp_k_pos, cmp_k_w1, cmp_k_w2, cmp_v_pos, cmp_v_w1, cmp_v_w2,
                               kv_k_gain[0])
    tabs = (near, wfar, t31, cmpb)
    for layer in range(n_b):
        ot = _nsa_attention(qt, gt, cmp_nat, cmp_t, kslc, vslc, kwin, vwin, tabs, ovt)
        wo = b_w_out[layer].T.astype(BF16)
        if layer + 1 < n_b:
            wsb, gb = _b_weights(b_w_in[layer + 1], b_q_gain[layer + 1])
            consts = [wo, _bcast(b_norm[layer + 1])] + wsb + gb
            xt, qt, gt = _proj_call(
                _b2b_kernel, "proj_b2b", b, s, [xt, ot] + consts,
                [_tok_spec(D_MODEL), _tok_spec(HQ)] + _specs_for(consts),
                [xt_shape] + b_shapes, [xt_spec] + b_specs)
        else:
            out = _proj_call(
                _final_kernel, "proj_final", b, s, [xt, ot, wo],
                [_tok_spec(D_MODEL), _tok_spec(HQ), _const_spec(wo.shape)],
                jax.ShapeDtypeStruct((b, s, D_MODEL), F32), _nat_spec(D_MODEL))
    return out
```

```python
import functools
import math

import numpy as np
import jax
import jax.numpy as jnp
from jax import lax
from jax.experimental import pallas as pl
from jax.experimental.pallas import tpu as pltpu

D_MODEL = 1024
HEAD_DIM = 64
N_HEADS = 16
N_KV_HEADS = 4
GROUP = N_HEADS // N_KV_HEADS
HQ = N_HEADS * HEAD_DIM
HKV = N_KV_HEADS * HEAD_DIM
N_BRANCH = 3
SWA_WINDOW = 128
NSA_WINDOW = 512
CMP_LEN = 32
CMP_STRIDE = 16
CMP_HIDDEN = 256
SEL_BLOCK = 64
SEL_TOP = 8
SEL_FORCE_LOCAL = 2
NUM_BUCKETS = 32
MAX_DISTANCE = 128
EPS = 1e-6
NEG = -1e30
FORCE_BONUS = 1e6

TQ = 128
TM = 512
SWA_TILES = 4
GL = GROUP * TQ
PAIR = 2 * HEAD_DIM
NEAR = 2 * TQ
WIN_FAR = NSA_WINDOW - TQ
FAR = 4 * TQ
ONES_ROWS = 16
AUX_GROUP = 16
LOG2E = math.log2(math.e)
Q_SCALE = HEAD_DIM ** -0.5 * LOG2E
VMEM_LIMIT = 56 * 1024 * 1024

F32 = jnp.float32
BF16 = jnp.bfloat16


def _bucket_np(dist):
    d = np.maximum(dist, 0)
    max_exact = NUM_BUCKETS // 2
    ratio = (np.log(np.maximum(d, 1).astype(np.float32) / np.float32(max_exact))
             / np.float32(math.log(MAX_DISTANCE / max_exact))
             * np.float32(NUM_BUCKETS - max_exact))
    large = np.minimum(max_exact + ratio.astype(np.int32), NUM_BUCKETS - 1)
    return np.where(d < max_exact, d, large).astype(np.int32)


def _dist_vector(table, d_lo, length, hi_valid):
    d = d_lo + np.arange(length)
    v = table.T[:, _bucket_np(d)]
    return jnp.where(jnp.asarray((d >= 0) & (d < hi_valid))[None], v, NEG)


def _skew(v, n, step):
    length = v.shape[1]
    assert step * (n - 1) + TQ <= length
    width = length + step
    reps = -(-(n * width) // length)
    w = jnp.tile(v, (1, reps))[:, :n * width].reshape(v.shape[0], n, width)
    return w[:, :, :TQ]


def _toeplitz(table, n_rows, d_first, step, hi_valid):
    length = step * (n_rows - 1) + TQ
    v = _dist_vector(table, d_first - step * (n_rows - 1), length, hi_valid)
    t = _skew(v, n_rows, step)[:, ::-1, :]
    t = t.reshape(N_KV_HEADS, GROUP, n_rows, TQ).transpose(0, 2, 1, 3)
    return t.reshape(N_KV_HEADS, n_rows, GL).astype(F32)


def _mask_rows(t, n_masked):
    rows = np.arange(t.shape[1])[None, :, None] < n_masked
    return jnp.where(jnp.asarray(rows), NEG, t)


def _make_tables(table, nq):
    big = 1 << 30
    near = _toeplitz(table, NEAR, TQ, 1, big)
    swa = _toeplitz(table, NEAR, TQ, 1, SWA_WINDOW)
    swa = jnp.stack([_mask_rows(swa, TQ), swa])
    t31 = _toeplitz(table, 1, MAX_DISTANCE, 1, big)

    def saturated(rows):
        return jnp.broadcast_to(t31, (N_KV_HEADS, rows, GL))

    wcut = _toeplitz(table, TQ, NSA_WINDOW, 1, NSA_WINDOW)
    per = TQ // CMP_STRIDE
    ncp = nq * per
    off = per * (nq - 1)
    sat_c = -(-(MAX_DISTANCE + CMP_LEN - 1) // CMP_STRIDE)
    assert off >= sat_c
    band = _toeplitz(table, sat_c + per - 1, CMP_STRIDE * (sat_c - 1) - (CMP_LEN - 1), CMP_STRIDE, big)
    cmpb = jnp.concatenate(
        [saturated(off - sat_c + 1), band, jnp.full((N_KV_HEADS, ncp - per, GL), NEG, F32)], axis=1)

    def widen(t):
        t = jnp.swapaxes(t, -3, -2)
        return t.reshape(t.shape[:-2] + (N_KV_HEADS * GL,))

    return widen(swa), widen(near), widen(wcut), widen(t31), widen(cmpb)


def _overlap_t(s):
    ncp = s // CMP_STRIDE
    nsel = s // SEL_BLOCK
    cs = np.arange(ncp)[None, :] * CMP_STRIDE
    ss = np.arange(nsel)[:, None] * SEL_BLOCK
    ov = (cs < ss + SEL_BLOCK) & (cs + CMP_LEN > ss) & (np.arange(ncp)[None, :] < ncp - 1)
    return jnp.asarray(ov.astype(np.float32), dtype=BF16)


def _lanes(gain_b, n):
    return jnp.concatenate([gain_b] * (n // gain_b.shape[1]), axis=1)


def _rms_t(xt, gain_b):
    ms = jnp.mean(xt * xt, axis=0, keepdims=True)
    return (xt * lax.rsqrt(ms + EPS) * _lanes(gain_b, xt.shape[1])).astype(BF16)


def _head_norm_t(a, gain_b):
    outs = []
    for h in range(a.shape[0] // HEAD_DIM):
        blk = a[h * HEAD_DIM:(h + 1) * HEAD_DIM]
        ms = jnp.mean(blk * blk, axis=0, keepdims=True)
        outs.append(blk * lax.rsqrt(ms + EPS) * _lanes(gain_b, a.shape[1]))
    return jnp.concatenate(outs, axis=0)


def _sigmoid_h(hz):
    return 0.5 + 0.5 * jnp.tanh(hz)


def _silu_h(hz):
    return hz + hz * jnp.tanh(hz)


def _silu(z):
    return _silu_h(0.5 * z)


def _dot(a, b):
    return jnp.dot(a, b, preferred_element_type=F32)


def _proj_a(xn, w, qg, kg, q_out, k_out, v_out, g_out):
    pq = _dot(w[:HQ, :], xn)
    pkv = _dot(w[HQ:HQ + 2 * HKV, :], xn)
    q_out[0] = _head_norm_t(pq, qg[...]).astype(BF16)
    pz = _dot(w[HQ + 2 * HKV:, :], xn)
    k_out[0] = _head_norm_t(pkv[:HKV], kg[...]).T.astype(BF16)
    v_out[0] = pkv[HKV:].astype(BF16)
    g_out[0] = _silu_h(pz).astype(BF16)


def _proj_b(xn, w, qg, q_out, g_out):
    n_gate = N_BRANCH * N_HEADS
    pq = _dot(w[:HQ, :], xn)
    sg = _sigmoid_h(_dot(w[HQ:HQ + n_gate, :], xn))

    def gate_rows(pz, c):
        for hd in range(N_HEADS):
            r = c * N_HEADS + hd
            z = pz[hd * HEAD_DIM:(hd + 1) * HEAD_DIM]
            g_out[0, r * HEAD_DIM:(r + 1) * HEAD_DIM, :] = (_silu_h(z) * sg[r:r + 1]).astype(BF16)

    z0 = HQ + n_gate
    pz = _dot(w[z0:z0 + HQ, :], xn)
    q_out[0] = _head_norm_t(pq, qg[...]).astype(BF16)
    for c in range(1, N_BRANCH):
        nxt = _dot(w[z0 + c * HQ:z0 + (c + 1) * HQ, :], xn)
        gate_rows(pz, c - 1)
        pz = nxt
    gate_rows(pz, N_BRANCH - 1)


def _proj_kv(xn, wkv, kg1, kg2, craw_out, kslc_out, vslc_out, kwin_out, vwin_out):
    p = _dot(wkv[...], xn)

    def rows(n):
        return p[n * HKV:(n + 1) * HKV]
    for t in range(2):
        nat = rows(t).T
        for pr in range(HKV // PAIR):
            craw_out[0, t, pr] = nat[:, _pair_cols(pr)]
    kslc_out[0] = _head_norm_t(rows(2), kg1[...]).T.astype(BF16)
    vslc_out[0] = rows(3).astype(BF16)
    kwin_out[0] = _head_norm_t(rows(4), kg2[...]).T.astype(BF16)
    vwin_out[0] = rows(5).astype(BF16)


def _first_kernel(x_ref, ng, w, qg, kg, xt_out, q_out, k_out, v_out, g_out):
    xt = x_ref[0].T
    xt_out[0] = xt
    _proj_a(_rms_t(xt, ng[...]), w, qg, kg, q_out, k_out, v_out, g_out)


def _a2a_kernel(x_ref, o_ref, wo, ng, w, qg, kg, xt_out, q_out, k_out, v_out, g_out):
    xt = x_ref[0] + _dot(wo[...], o_ref[0])
    xt_out[0] = xt
    _proj_a(_rms_t(xt, ng[...]), w, qg, kg, q_out, k_out, v_out, g_out)


def _a2b_kernel(x_ref, o_ref, wo, ngkv, wkv, kg1, kg2, ngb, w, qg,
                xt_out, craw_out, kslc_out, vslc_out, kwin_out, vwin_out, q_out, g_out):
    xt = x_ref[0] + _dot(wo[...], o_ref[0])
    xt_out[0] = xt
    _proj_kv(_rms_t(xt, ngkv[...]), wkv, kg1, kg2, craw_out, kslc_out, vslc_out, kwin_out, vwin_out)
    _proj_b(_rms_t(xt, ngb[...]), w, qg, q_out, g_out)


def _b2b_kernel(x_ref, o_ref, wo, ngb, w, qg, xt_out, q_out, g_out):
    xt = x_ref[0] + _dot(wo[...], o_ref[0])
    xt_out[0] = xt
    _proj_b(_rms_t(xt, ngb[...]), w, qg, q_out, g_out)


def _final_kernel(x_ref, o_ref, wo, x_out):
    x_out[0] = (x_ref[0] + _dot(wo[...], o_ref[0])).T


def _tok_spec(rows):
    return pl.BlockSpec((1, rows, TM), lambda b, t: (b, 0, t))


def _nat_spec(cols):
    return pl.BlockSpec((1, TM, cols), lambda b, t: (b, t, 0))


def _const_spec(shape):
    nd = len(shape)
    return pl.BlockSpec(shape, lambda b, t: (0,) * nd, pipeline_mode=pl.Buffered(1))


def _proj_call(body, name, b, s, ins, in_specs, out_shapes, out_specs):
    return pl.pallas_call(
        body, name=name, grid=(b, s // TM),
        in_specs=in_specs, out_specs=out_specs, out_shape=out_shapes,
        compiler_params=pltpu.CompilerParams(
            dimension_semantics=("parallel", "parallel"), vmem_limit_bytes=VMEM_LIMIT),
    )(*ins)


def _bcast(v, scale=1.0):
    return jnp.broadcast_to((v.astype(F32) * scale)[:, None], (v.shape[0], TQ))


def _a_weights(w_in, q_gain, k_gain):
    halve = np.where(np.arange(w_in.shape[1]) >= HQ + 2 * HKV, 0.5, 1.0).astype(np.float32)
    ws = [(w_in * halve).T.astype(BF16)]
    gains = [_bcast(q_gain, Q_SCALE), _bcast(k_gain)]
    return ws, gains


def _b_weights(w_in, q_gain):
    halve = np.where(np.arange(w_in.shape[1]) >= HQ, 0.5, 1.0).astype(np.float32)
    ws = [(w_in * halve).T.astype(BF16)]
    return ws, [_bcast(q_gain, Q_SCALE)]


def _a_outs(b, s):
    shapes = [jax.ShapeDtypeStruct((b, D_MODEL, s), F32), jax.ShapeDtypeStruct((b, HQ, s), BF16),
              jax.ShapeDtypeStruct((b, s, HKV), BF16), jax.ShapeDtypeStruct((b, HKV, s), BF16),
              jax.ShapeDtypeStruct((b, HQ, s), BF16)]
    specs = [_tok_spec(D_MODEL), _tok_spec(HQ), _nat_spec(HKV), _tok_spec(HKV), _tok_spec(HQ)]
    return shapes, specs


def _b_outs(b, s):
    shapes = [jax.ShapeDtypeStruct((b, HQ, s), BF16), jax.ShapeDtypeStruct((b, N_BRANCH * HQ, s), BF16)]
    specs = [_tok_spec(HQ), _tok_spec(N_BRANCH * HQ)]
    return shapes, specs


def _specs_for(arrs):
    return [_const_spec(a.shape) for a in arrs]


def _head_rows(h, g):
    return slice((h * GROUP + g) * HEAD_DIM, (h * GROUP + g + 1) * HEAD_DIM)


def _kv_rows(h):
    return slice(h * HEAD_DIM, (h + 1) * HEAD_DIM)


def _head_lanes(h):
    return slice(h * GL, (h + 1) * GL)


def _pair_lanes(p):
    return slice(2 * p * GL, 2 * (p + 1) * GL)


def _pair_cols(p):
    return slice(p * PAIR, (p + 1) * PAIR)


def _padded_q(q_ref, h, cols=slice(None)):
    qs = jnp.concatenate([q_ref[0, _head_rows(h, g), cols] for g in range(GROUP)], axis=1)
    zeros = jnp.zeros_like(qs)
    return jnp.concatenate([qs, zeros] if h % 2 == 0 else [zeros, qs], axis=0)


def _with_ones(v):
    return jnp.concatenate([v, jnp.ones((ONES_ROWS, v.shape[1]), BF16)], axis=0)


def _exp2_bf16(x):
    return jnp.exp2(x.astype(BF16))


def _swa_kernel(q_ref, kp_ref, kc_ref, vp_ref, vc_ref, g_ref, bias_ref, sink_ref, o_ref):
    i = pl.program_id(1)
    sink = sink_ref[...]
    for j in range(SWA_TILES):
        cur = slice(j * TQ, (j + 1) * TQ)
        prev = slice((j - 1) * TQ, j * TQ)
        k_prev = (lambda c: kp_ref[0, :, c]) if j == 0 else (lambda c: kc_ref[0, prev, c])
        v_prev = (lambda r: vp_ref[0, r, :]) if j == 0 else (lambda r: vc_ref[0, r, prev])
        bias = bias_ref[jnp.minimum(i, 1)] if j == 0 else bias_ref[1]
        pairs = []
        for p in range(N_KV_HEADS // 2):
            k = jnp.concatenate([k_prev(_pair_cols(p)), kc_ref[0, cur, _pair_cols(p)]], axis=0)
            qp = jnp.concatenate([_padded_q(q_ref, 2 * p, cur), _padded_q(q_ref, 2 * p + 1, cur)], axis=1)
            pairs.append(_dot(k, qp))
        s = jnp.concatenate(pairs, axis=1) + bias
        m = jnp.maximum(jnp.max(s, axis=0, keepdims=True), sink)
        e = _exp2_bf16(s - m)
        e_sink = jnp.exp2(sink - m)
        for h in range(N_KV_HEADS):
            v = jnp.concatenate([v_prev(_kv_rows(h)), vc_ref[0, _kv_rows(h), cur]], axis=1)
            acc = _dot(_with_ones(v), e[:, _head_lanes(h)])
            o = acc[:HEAD_DIM] / (acc[HEAD_DIM:HEAD_DIM + 1] + e_sink[:, _head_lanes(h)])
            for g in range(GROUP):
                rows = _head_rows(h, g)
                o_ref[0, rows, cur] = (o[:, g * TQ:(g + 1) * TQ] * g_ref[0, rows, cur].astype(F32)).astype(BF16)


def _swa_attention(qt, k, vt, gt, swa_tab, sink_row):
    b, _, s = qt.shape
    tw = SWA_TILES * TQ
    prev = lambda i: jnp.maximum(SWA_TILES * i - 1, 0)
    return pl.pallas_call(
        _swa_kernel, name="swa_attention", grid=(b, s // tw),
        in_specs=[
            pl.BlockSpec((1, HQ, tw), lambda b_, i: (b_, 0, i)),
            pl.BlockSpec((1, TQ, HKV), lambda b_, i: (b_, prev(i), 0)),
            pl.BlockSpec((1, tw, HKV), lambda b_, i: (b_, i, 0)),
            pl.BlockSpec((1, HKV, TQ), lambda b_, i: (b_, 0, prev(i))),
            pl.BlockSpec((1, HKV, tw), lambda b_, i: (b_, 0, i)),
            pl.BlockSpec((1, HQ, tw), lambda b_, i: (b_, 0, i)),
            pl.BlockSpec(swa_tab.shape, lambda b_, i: (0, 0, 0)),
            pl.BlockSpec((1, N_KV_HEADS * GL), lambda b_, i: (0, 0)),
        ],
        out_specs=pl.BlockSpec((1, HQ, tw), lambda b_, i: (b_, 0, i)),
        out_shape=jax.ShapeDtypeStruct((b, HQ, s), BF16),
        compiler_params=pltpu.CompilerParams(
            dimension_semantics=("parallel", "arbitrary"), vmem_limit_bytes=VMEM_LIMIT),
    )(qt, k, k, vt, vt, gt, swa_tab, sink_row)


def _nsa_kernel(nq, q_ref, g_ref, kcmp_ref, vcmp_ref, kslc_ref, vslc_ref, kwin_ref, vwin_ref,
                cmpb_ref, near_ref, wcut_ref, t31_ref, ovt_ref, kauxf_ref, kauxn_ref, kauxw_ref, kauxv_ref, o_ref,
                qp_ref, m_ref, acc_ref, selb_ref, part_ref, sc0_ref, sc1_ref):
    i = pl.program_id(1)
    ncp = kcmp_ref.shape[2]
    nsel = ovt_ref.shape[0]
    n_pairs = N_KV_HEADS // 2
    wide = N_KV_HEADS * GL
    cmp_start = pl.multiple_of((TQ // CMP_STRIDE) * (nq - 1 - i), 8)
    prev_rows = pl.ds(pl.multiple_of(jnp.maximum(i - 1, 0) * TQ, TQ), TQ)
    cur_rows = pl.ds(pl.multiple_of(i * TQ, TQ), TQ)
    far_starts = [pl.multiple_of(jnp.maximum(i - back, 0) * TQ, TQ) for back in range(NSA_WINDOW // TQ, 1, -1)]
    n_far_blocks = (TQ // SEL_BLOCK) * (i - 1)

    def near_keys(k_ref, p):
        return jnp.concatenate([k_ref[0, prev_rows, _pair_cols(p)], k_ref[0, cur_rows, _pair_cols(p)]], axis=0)

    def near_values(v_ref, h):
        return jnp.concatenate([v_ref[0, _kv_rows(h), prev_rows], v_ref[0, _kv_rows(h), cur_rows]], axis=1)

    def scores(lhs_of_pair, rows):
        return jnp.concatenate([_dot(lhs_of_pair(p), qp_ref[rows, _pair_lanes(p)]) for p in range(n_pairs)], axis=1)

    for h in range(N_KV_HEADS):
        qp_ref[:PAIR, _head_lanes(h)] = _padded_q(q_ref, h)
    t31 = t31_ref[...]
    hi = t31.astype(BF16).astype(F32)
    grp_row = lax.broadcasted_iota(jnp.int32, (AUX_GROUP, wide), 0)
    const_rows = jnp.where(grp_row == 0, hi, jnp.where(grp_row == 1, t31 - hi, 0.0))
    n_back = NSA_WINDOW // TQ
    old_chunk_rows = jnp.where((grp_row < n_back - 1) & (grp_row < n_back - i), NEG, 0.0)
    prev_chunk_rows = jnp.where((grp_row == 0) & (i == 0), NEG, 0.0)
    zeros_grp = jnp.zeros((AUX_GROUP, wide), F32)
    early = jnp.concatenate(
        [jnp.zeros((nsel, wide), F32), const_rows, zeros_grp, old_chunk_rows, prev_chunk_rows]
        + [zeros_grp] * ((PAIR - nsel) // AUX_GROUP - 4), axis=0)
    qp_ref[PAIR:, :] = early.astype(BF16)

    q_rows = slice(0, PAIR)
    all_rows = slice(0, 2 * PAIR)
    near_tab = near_ref[...]
    bc = cmpb_ref[pl.ds(cmp_start, ncp), :]
    sc = scores(lambda p: kcmp_ref[0, 0, :, _pair_cols(p)], q_rows) + bc
    mc = jnp.max(sc, axis=0, keepdims=True)
    s_near = scores(lambda p: jnp.concatenate([near_keys(kwin_ref, p), kauxv_ref[...]], axis=1), all_rows) + near_tab
    s_far = scores(lambda p: jnp.concatenate(
        [jnp.concatenate([kwin_ref[0, pl.ds(st, TQ), _pair_cols(p)] for st in far_starts], axis=0), kauxw_ref[...]],
        axis=1), all_rows)
    s_far = jnp.concatenate([s_far[:TQ] + wcut_ref[...], s_far[TQ:]], axis=0)
    mw = jnp.maximum(jnp.max(s_near, axis=0, keepdims=True), jnp.max(s_far, axis=0, keepdims=True))

    valid = bc > 0.5 * NEG
    ec = jnp.where(valid, jnp.exp2(sc - mc), 0.0)
    lc = jnp.sum(ec, axis=0, keepdims=True)
    pc = ec / jnp.where(lc > 0.0, lc, 1.0)
    pc_b = pc.astype(BF16)
    o_cmp = [_dot(vcmp_ref[0, 0, _kv_rows(h), :], pc_b[:, _head_lanes(h)]) for h in range(N_KV_HEADS)]

    psum = jnp.concatenate(
        [sum(pc[:, h * GL + g * TQ:h * GL + (g + 1) * TQ] for g in range(GROUP)) for h in range(N_KV_HEADS)], axis=1)
    ovt = ovt_ref[...]
    p1 = psum.astype(BF16)
    r1 = psum - p1.astype(F32)
    p2 = r1.astype(BF16)
    p3 = (r1 - p2.astype(F32)).astype(BF16)
    imp = _dot(ovt, p1) + _dot(ovt, p2) + _dot(ovt, p3)
    lane = lax.broadcasted_iota(jnp.int32, imp.shape, 1)
    pos = i * TQ + lane % TQ
    blk = lax.broadcasted_iota(jnp.int32, imp.shape, 0)
    causal = blk * SEL_BLOCK <= pos
    rel = pos // SEL_BLOCK - blk
    forced = (blk == 0) | ((rel >= 0) & (rel < SEL_FORCE_LOCAL))
    score = jnp.where(causal, imp + jnp.where(forced, FORCE_BONUS, 0.0), NEG)
    groups = [score[r:r + 8] for r in range(0, nsel, 8)]
    counts = [jnp.zeros((8, imp.shape[1]), jnp.int32) for _ in groups]
    row_in_group = lax.broadcasted_iota(jnp.int32, (8, imp.shape[1]), 0)
    for jp in range(nsel):
        row = score[jp:jp + 1, :]
        for gi, grp in enumerate(groups):
            if gi * 8 > jp:
                beats = row >= grp
            elif gi * 8 + 7 < jp:
                beats = row > grp
            else:
                beats = (row > grp) | ((row == grp) & (row_in_group > jp - gi * 8))
            counts[gi] = counts[gi] + beats.astype(jnp.int32)
    sel = jnp.concatenate(counts, axis=0) < min(SEL_TOP, nsel)

    def per_group(x):
        return jnp.concatenate([x[:, h * TQ:(h + 1) * TQ] for h in range(N_KV_HEADS) for _ in range(GROUP)], axis=1)

    selb_ref[...] = per_group(jnp.where(sel, 0.0, NEG))
    sel_far = per_group(jnp.where(sel & (blk < n_far_blocks), 0.0, NEG))
    qp_ref[PAIR:PAIR + nsel, :] = sel_far.astype(BF16)
    first_blk = (TQ // SEL_BLOCK) * (i - 1)
    per_chunk = TQ // SEL_BLOCK
    near_rows = [selb_ref[pl.ds(jnp.maximum(first_blk + u, 0), 1), :] for u in range(NEAR // SEL_BLOCK)]
    near_rows = [jnp.where(i == 0, NEG, r) if u < per_chunk else r for u, r in enumerate(near_rows)]
    near_grp = jnp.concatenate(near_rows + [jnp.zeros((AUX_GROUP - len(near_rows), wide), F32)], axis=0)
    qp_ref[PAIR + nsel + AUX_GROUP:PAIR + nsel + 2 * AUX_GROUP, :] = near_grp.astype(BF16)

    s = scores(lambda p: jnp.concatenate([near_keys(kslc_ref, p), kauxn_ref[...]], axis=1), all_rows) + near_tab
    m = jnp.max(s, axis=0, keepdims=True)

    sc_refs = (sc0_ref, sc1_ref)

    def sweep_scores(step, dst_ref, cols, lanes):
        keys = slice(step * FAR, (step + 1) * FAR)
        lhs = jnp.concatenate([kslc_ref[0, keys, cols], kauxf_ref[keys, :]], axis=1)
        sc_new = _dot(lhs, qp_ref[:, lanes])
        dst_ref[:FAR, lanes] = sc_new
        dst_ref[FAR:FAR + 1, lanes] = jnp.max(sc_new, axis=0, keepdims=True)

    for p in range(n_pairs):
        sweep_scores(0, sc_refs[0], _pair_cols(p), _pair_lanes(p))

    m_ref[...] = m
    e = _exp2_bf16(s - m)
    for h in range(N_KV_HEADS):
        acc_ref[h] = _dot(_with_ones(near_values(vslc_ref, h)), e[:, _head_lanes(h)])

    e_near = _exp2_bf16(s_near - mw)
    e_far = _exp2_bf16(s_far - mw)
    for h in range(N_KV_HEADS):
        far_v = jnp.concatenate([vwin_ref[0, _kv_rows(h), pl.ds(st, TQ)] for st in far_starts], axis=1)
        acc_w = (_dot(_with_ones(near_values(vwin_ref, h)), e_near[:, _head_lanes(h)])
                 + _dot(_with_ones(far_v), e_far[:, _head_lanes(h)]))
        o_win = acc_w[:HEAD_DIM] / acc_w[HEAD_DIM:HEAD_DIM + 1]
        for g in range(GROUP):
            rows = _head_rows(h, g)
            lanes = slice(g * TQ, (g + 1) * TQ)
            part_ref[rows, :] = (o_cmp[h][:, lanes] * g_ref[0, 0, rows, :].astype(F32)
                                 + o_win[:, lanes] * g_ref[0, 2, rows, :].astype(F32))

    def sweep_consume(step, src_ref, lanes, heads):
        s = src_ref[:FAR, lanes]
        m_old = m_ref[:, lanes]
        m_new = jnp.maximum(m_old, src_ref[FAR:FAR + 1, lanes])
        alpha = jnp.exp2(m_old - m_new)
        e = _exp2_bf16(s - m_new)
        for j, h in enumerate(heads):
            sub = slice(j * GL, (j + 1) * GL)
            v = _with_ones(vslc_ref[0, _kv_rows(h), step * FAR:(step + 1) * FAR])
            acc_ref[h] = alpha[:, sub] * acc_ref[h] + _dot(v, e[:, sub])
        m_ref[:, lanes] = m_new

    n_far_chunks = jnp.maximum(i - 1, 0)
    n_steps = (n_far_chunks + FAR // TQ - 1) // (FAR // TQ)
    max_steps = kslc_ref.shape[1] // FAR
    for step in range(max_steps):
        if step + 1 < max_steps:
            @pl.when(step + 1 < n_steps)
            def _(step=step):
                for h in range(N_KV_HEADS):
                    sweep_consume(step, sc_refs[step % 2], _head_lanes(h), (h,))
                    sweep_scores(step + 1, sc_refs[(step + 1) % 2], _pair_cols(h // 2), _head_lanes(h))

        @pl.when(step + 1 == n_steps)
        def _(step=step):
            sweep_consume(step, sc_refs[step % 2], slice(0, wide), tuple(range(N_KV_HEADS)))

    for h in range(N_KV_HEADS):
        o_slc = acc_ref[h, :HEAD_DIM, :] / acc_ref[h, HEAD_DIM:HEAD_DIM + 1, :]
        for g in range(GROUP):
            rows = _head_rows(h, g)
            o = part_ref[rows, :] + o_slc[:, g * TQ:(g + 1) * TQ] * g_ref[0, 1, rows, :].astype(F32)
            o_ref[0, rows, :] = o.astype(BF16)


def _nsa_attention(qt, gt, cmp_nat, cmp_t, kslc, vslc, kwin, vwin, tabs, ovt):
    b, _, s = qt.shape
    nq = s // TQ
    assert s % FAR == 0
    ncp = s // CMP_STRIDE
    nsel = s // SEL_BLOCK
    near, wcut, t31, cmpb = tabs
    g4 = gt.reshape(b, N_BRANCH, HQ, s)
    assert nsel % AUX_GROUP == 0 and nsel + 4 * AUX_GROUP <= PAIR
    cols = np.arange(PAIR)[None, :]
    ones_cols = (cols == nsel) | (cols == nsel + 1)
    as_bf16 = lambda m: jnp.asarray(m.astype(np.float32), BF16)
    kaux_far = as_bf16((cols == np.arange(s)[:, None] // SEL_BLOCK) | ones_cols)
    kaux_near = as_bf16(cols == nsel + AUX_GROUP + np.arange(NEAR)[:, None] // SEL_BLOCK)
    old_rows = np.arange(WIN_FAR)[:, None]
    kaux_wfar = as_bf16((cols == nsel + 2 * AUX_GROUP + old_rows // TQ) | (ones_cols & (old_rows >= TQ)))
    kaux_wnear = as_bf16((cols == nsel + 3 * AUX_GROUP) & (np.arange(NEAR)[:, None] < TQ))
    full = lambda shape: pl.BlockSpec(shape, lambda b_, i: (0,) * len(shape))
    return pl.pallas_call(
        functools.partial(_nsa_kernel, nq), name="nsa_attention", grid=(b, nq),
        in_specs=[
            pl.BlockSpec((1, HQ, TQ), lambda b_, i: (b_, 0, i)),
            pl.BlockSpec((1, N_BRANCH, HQ, TQ), lambda b_, i: (b_, 0, 0, i)),
            pl.BlockSpec((1, 1, ncp, HKV), lambda b_, i: (0, b_, 0, 0)),
            pl.BlockSpec((1, 1, HKV, ncp), lambda b_, i: (1, b_, 0, 0)),
            pl.BlockSpec((1, s, HKV), lambda b_, i: (b_, 0, 0)),
            pl.BlockSpec((1, HKV, s), lambda b_, i: (b_, 0, 0)),
            pl.BlockSpec((1, s, HKV), lambda b_, i: (b_, 0, 0)),
            pl.BlockSpec((1, HKV, s), lambda b_, i: (b_, 0, 0)),
            full(cmpb.shape),
            full(near.shape), full(wcut.shape), full(t31.shape),
            full((nsel, ncp)),
            full((s, PAIR)), full((NEAR, PAIR)), full((WIN_FAR, PAIR)), full((NEAR, PAIR)),
        ],
        out_specs=pl.BlockSpec((1, HQ, TQ), lambda b_, i: (b_, 0, i)),
        out_shape=jax.ShapeDtypeStruct((b, HQ, s), BF16),
        scratch_shapes=[pltpu.VMEM((2 * PAIR, N_KV_HEADS * GL), BF16),
                        pltpu.VMEM((1, N_KV_HEADS * GL), F32),
                        pltpu.VMEM((N_KV_HEADS, HEAD_DIM + ONES_ROWS, GL), F32),
                        pltpu.VMEM((nsel, N_KV_HEADS * GL), F32),
                        pltpu.VMEM((HQ, TQ), F32),
                        pltpu.VMEM((FAR + 8, N_KV_HEADS * GL), F32), pltpu.VMEM((FAR + 8, N_KV_HEADS * GL), F32)],
        compiler_params=pltpu.CompilerParams(
            dimension_semantics=("parallel", "arbitrary"), vmem_limit_bytes=VMEM_LIMIT),
    )(qt, g4, cmp_nat, cmp_t, kslc, vslc, kwin, vwin, cmpb, near, wcut, t31, ovt,
      kaux_far, kaux_near, kaux_wfar, kaux_wnear)


def _compress_kernel(u_ref, ptop_ref, pbot_ref, wtop_ref, wbot_ref, w2_ref, kg_ref, nat_out, t_out):
    t = pl.program_id(0)
    nch = u_ref.shape[3] // CMP_STRIDE
    hidden = []
    for p in range(HKV // PAIR):
        u = jnp.concatenate([u_ref[0, 0, p, pl.ds(j, nch, stride=CMP_STRIDE), :] for j in range(CMP_STRIDE)], axis=1)
        top = (u + ptop_ref[0]).astype(BF16)
        bot = (u + pbot_ref[0]).astype(BF16)
        for e in range(2):
            bm = _dot(bot, wbot_ref[0, e])
            hidden.append(_dot(top, wtop_ref[0, e]) + pltpu.roll(bm, nch - 1, axis=0))
    hid = jnp.concatenate(hidden, axis=1)
    out_t = _dot(_silu(hid).astype(BF16), w2_ref[0]).T
    out_t = jnp.where(t == 0, _head_norm_t(out_t, kg_ref[...]), out_t)
    t_out[0, 0] = out_t.astype(BF16)
    nat_out[0, 0] = out_t.T.astype(BF16)


def _compress(craw, cmp_k_pos, cmp_k_w1, cmp_k_w2, cmp_v_pos, cmp_v_w1, cmp_v_w2, k_gain):
    b, _, n_pairs, s, _ = craw.shape
    nch = s // CMP_STRIDE
    wide = CMP_STRIDE * PAIR

    def padded_w1(w1_half):
        w = w1_half.reshape(CMP_STRIDE, HEAD_DIM, CMP_HIDDEN)
        z = jnp.zeros_like(w)
        both = jnp.stack([jnp.concatenate([w, z], axis=1), jnp.concatenate([z, w], axis=1)])
        return both.reshape(2, wide, CMP_HIDDEN).astype(BF16)

    def pos_row(p_half):
        return jnp.broadcast_to(p_half[:, None, :], (CMP_STRIDE, 2, HEAD_DIM)).reshape(1, wide)

    def big_w2(w2):
        eye = jnp.eye(N_KV_HEADS, dtype=F32)
        return jnp.einsum('nd,hk->hnkd', w2, eye).reshape(N_KV_HEADS * CMP_HIDDEN, HKV).astype(BF16)

    half = CMP_STRIDE * HEAD_DIM
    wtop = jnp.stack([padded_w1(cmp_k_w1[:half]), padded_w1(cmp_v_w1[:half])])
    wbot = jnp.stack([padded_w1(cmp_k_w1[half:]), padded_w1(cmp_v_w1[half:])])
    ptop = jnp.stack([pos_row(cmp_k_pos[:CMP_STRIDE]), pos_row(cmp_v_pos[:CMP_STRIDE])])
    pbot = jnp.stack([pos_row(cmp_k_pos[CMP_STRIDE:]), pos_row(cmp_v_pos[CMP_STRIDE:])])
    w2 = jnp.stack([big_w2(cmp_k_w2), big_w2(cmp_v_w2)])
    kg = jnp.broadcast_to(k_gain.astype(F32)[:, None], (HEAD_DIM, nch))
    sel = lambda shape: pl.BlockSpec((1,) + shape, lambda t, b_: (t,) + (0,) * len(shape))
    return pl.pallas_call(
        _compress_kernel, name="compress", grid=(2, b),
        in_specs=[
            pl.BlockSpec((1, 1, n_pairs, s, PAIR), lambda t, b_: (b_, t, 0, 0, 0)),
            sel((1, wide)), sel((1, wide)),
            sel((2, wide, CMP_HIDDEN)), sel((2, wide, CMP_HIDDEN)),
            sel((N_KV_HEADS * CMP_HIDDEN, HKV)),
            pl.BlockSpec((HEAD_DIM, nch), lambda t, b_: (0, 0)),
        ],
        out_specs=[pl.BlockSpec((1, 1, nch, HKV), lambda t, b_: (t, b_, 0, 0)),
                   pl.BlockSpec((1, 1, HKV, nch), lambda t, b_: (t, b_, 0, 0))],
        out_shape=[jax.ShapeDtypeStruct((2, b, nch, HKV), BF16),
                   jax.ShapeDtypeStruct((2, b, HKV, nch), BF16)],
        compiler_params=pltpu.CompilerParams(
            dimension_semantics=("arbitrary", "arbitrary"), vmem_limit_bytes=VMEM_LIMIT),
    )(craw, ptop, pbot, wtop, wbot, w2, kg)


def kernel(x, rel_table, a_norm, a_w_in, a_q_gain, a_k_gain, a_sink, a_w_out, kv_norm, kv_w,
           kv_k_gain, cmp_k_pos, cmp_k_w1, cmp_k_w2, cmp_v_pos, cmp_v_w1, cmp_v_w2,
           b_norm, b_w_in, b_q_gain, b_w_out):
    b, s, _ = x.shape
    nq = s // TQ
    n_a = a_w_in.shape[0]
    n_b = b_w_in.shape[0]
    swa_tab, near, wcut, t31, cmpb = _make_tables(rel_table.astype(F32) * LOG2E, nq)
    ovt = _overlap_t(s)

    a_shapes, a_specs = _a_outs(b, s)
    b_shapes, b_specs = _b_outs(b, s)
    xt_shape, xt_spec = a_shapes[0], a_specs[0]

    ws, gains = _a_weights(a_w_in[0], a_q_gain[0], a_k_gain[0])
    consts = [_bcast(a_norm[0])] + ws + gains
    xt, qt, k, vt, gt = _proj_call(
        _first_kernel, "proj_first", b, s, [x] + consts,
        [_nat_spec(D_MODEL)] + _specs_for(consts), a_shapes, a_specs)

    for layer in range(n_a):
        sink_row = jnp.broadcast_to(
            (a_sink[layer].astype(F32) * LOG2E).reshape(N_KV_HEADS, 1, GROUP, 1), (N_KV_HEADS, 1, GROUP, TQ)
        ).reshape(1, N_KV_HEADS * GL)
        ot = _swa_attention(qt, k, vt, gt, swa_tab, sink_row)
        wo = a_w_out[layer].T.astype(BF16)
        if layer + 1 < n_a:
            ws, gains = _a_weights(a_w_in[layer + 1], a_q_gain[layer + 1], a_k_gain[layer + 1])
            consts = [wo, _bcast(a_norm[layer + 1])] + ws + gains
            xt, qt, k, vt, gt = _proj_call(
                _a2a_kernel, "proj_a2a", b, s, [xt, ot] + consts,
                [_tok_spec(D_MODEL), _tok_spec(HQ)] + _specs_for(consts), a_shapes, a_specs)
        else:
            wsb, gb = _b_weights(b_w_in[0], b_q_gain[0])
            consts = ([wo, _bcast(kv_norm), kv_w.T.astype(BF16), _bcast(kv_k_gain[1]), _bcast(kv_k_gain[2]),
                       _bcast(b_norm[0])] + wsb + gb)
            kv_shapes = [jax.ShapeDtypeStruct((b, 2, HKV // PAIR, s, PAIR), F32),
                         jax.ShapeDtypeStruct((b, s, HKV), BF16), jax.ShapeDtypeStruct((b, HKV, s), BF16),
                         jax.ShapeDtypeStruct((b, s, HKV), BF16), jax.ShapeDtypeStruct((b, HKV, s), BF16)]
            kv_specs = [pl.BlockSpec((1, 2, HKV // PAIR, TM, PAIR), lambda b_, t: (b_, 0, 0, t, 0)),
                        _nat_spec(HKV), _tok_spec(HKV), _nat_spec(HKV), _tok_spec(HKV)]
            xt, craw, kslc, vslc, kwin, vwin, qt, gt = _proj_call(
                _a2b_kernel, "proj_a2b", b, s, [xt, ot] + consts,
                [_tok_spec(D_MODEL), _tok_spec(HQ)] + _specs_for(consts),
                [xt_shape] + kv_shapes + b_shapes, [xt_spec] + kv_specs + b_specs)

    cmp_nat, cmp_t = _compress(craw, cmp_k_pos, cmp_k_w1, cmp_k_w2, cmp_v_pos, cmp_v_w1, cmp_v_w2,
                               kv_k_gain[0])
    tabs = (near, wcut, t31, cmpb)
    for layer in range(n_b):
        ot = _nsa_attention(qt, gt, cmp_nat, cmp_t, kslc, vslc, kwin, vwin, tabs, ovt)
        wo = b_w_out[layer].T.astype(BF16)
        if layer + 1 < n_b:
            wsb, gb = _b_weights(b_w_in[layer + 1], b_q_gain[layer + 1])
            consts = [wo, _bcast(b_norm[layer + 1])] + wsb + gb
            xt, qt, gt = _proj_call(
                _b2b_kernel, "proj_b2b", b, s, [xt, ot] + consts,
                [_tok_spec(D_MODEL), _tok_spec(HQ)] + _specs_for(consts),
                [xt_shape] + b_shapes, [xt_spec] + b_specs)
        else:
            out = _proj_call(
                _final_kernel, "proj_final", b, s, [xt, ot, wo],
                [_tok_spec(D_MODEL), _tok_spec(HQ), _const_spec(wo.shape)],
                jax.ShapeDtypeStruct((b, s, D_MODEL), F32), _nat_spec(D_MODEL))
    return out
```

```python
import functools
import math

import numpy as np
import jax
import jax.numpy as jnp
from jax import lax
from jax.experimental import pallas as pl
from jax.experimental.pallas import tpu as pltpu

D_MODEL = 1024
HEAD_DIM = 64
N_HEADS = 16
N_KV_HEADS = 4
GROUP = N_HEADS // N_KV_HEADS
HQ = N_HEADS * HEAD_DIM
HKV = N_KV_HEADS * HEAD_DIM
N_BRANCH = 3
SWA_WINDOW = 128
NSA_WINDOW = 512
CMP_LEN = 32
CMP_STRIDE = 16
CMP_HIDDEN = 256
SEL_BLOCK = 64
SEL_TOP = 8
SEL_FORCE_LOCAL = 2
NUM_BUCKETS = 32
MAX_DISTANCE = 128
EPS = 1e-6
NEG = -1e30
FORCE_BONUS = 1e6

TQ = 128
TM = 512
SWA_TILES = 4
GL = GROUP * TQ
PAIR = 2 * HEAD_DIM
NEAR = 2 * TQ
WIN_FAR = NSA_WINDOW - TQ
N_WIN_VARIANTS = NSA_WINDOW // TQ + 1
FAR = 4 * TQ
ONES_ROWS = 16
LOG2E = math.log2(math.e)
Q_SCALE = HEAD_DIM ** -0.5 * LOG2E
VMEM_LIMIT = 56 * 1024 * 1024

F32 = jnp.float32
BF16 = jnp.bfloat16


def _bucket_np(dist):
    d = np.maximum(dist, 0)
    max_exact = NUM_BUCKETS // 2
    ratio = (np.log(np.maximum(d, 1).astype(np.float32) / np.float32(max_exact))
             / np.float32(math.log(MAX_DISTANCE / max_exact))
             * np.float32(NUM_BUCKETS - max_exact))
    large = np.minimum(max_exact + ratio.astype(np.int32), NUM_BUCKETS - 1)
    return np.where(d < max_exact, d, large).astype(np.int32)


def _dist_vector(table, d_lo, length, hi_valid):
    d = d_lo + np.arange(length)
    v = table.T[:, _bucket_np(d)]
    return jnp.where(jnp.asarray((d >= 0) & (d < hi_valid))[None], v, NEG)


def _skew(v, n, step):
    length = v.shape[1]
    assert step * (n - 1) + TQ <= length
    width = length + step
    reps = -(-(n * width) // length)
    w = jnp.tile(v, (1, reps))[:, :n * width].reshape(v.shape[0], n, width)
    return w[:, :, :TQ]


def _toeplitz(table, n_rows, d_first, step, hi_valid):
    length = step * (n_rows - 1) + TQ
    v = _dist_vector(table, d_first - step * (n_rows - 1), length, hi_valid)
    t = _skew(v, n_rows, step)[:, ::-1, :]
    t = t.reshape(N_KV_HEADS, GROUP, n_rows, TQ).transpose(0, 2, 1, 3)
    return t.reshape(N_KV_HEADS, n_rows, GL).astype(F32)


def _mask_rows(t, n_masked):
    rows = np.arange(t.shape[1])[None, :, None] < n_masked
    return jnp.where(jnp.asarray(rows), NEG, t)


def _make_tables(table, nq):
    big = 1 << 30
    near = _toeplitz(table, NEAR, TQ, 1, big)
    near = jnp.stack([_mask_rows(near, TQ), near])
    swa = _toeplitz(table, NEAR, TQ, 1, SWA_WINDOW)
    swa = jnp.stack([_mask_rows(swa, TQ), swa])
    t31 = _toeplitz(table, 1, MAX_DISTANCE, 1, big)

    def saturated(rows):
        return jnp.broadcast_to(t31, (N_KV_HEADS, rows, GL))

    wfar = jnp.concatenate([_toeplitz(table, TQ, NSA_WINDOW, 1, NSA_WINDOW), saturated(WIN_FAR - TQ)], axis=1)
    wfar = jnp.stack([_mask_rows(wfar, min(WIN_FAR, (N_WIN_VARIANTS - 1 - v) * TQ))
                      for v in range(N_WIN_VARIANTS)])
    per = TQ // CMP_STRIDE
    ncp = nq * per
    off = per * (nq - 1)
    sat_c = -(-(MAX_DISTANCE + CMP_LEN - 1) // CMP_STRIDE)
    assert off >= sat_c
    band = _toeplitz(table, sat_c + per - 1, CMP_STRIDE * (sat_c - 1) - (CMP_LEN - 1), CMP_STRIDE, big)
    cmpb = jnp.concatenate(
        [saturated(off - sat_c + 1), band, jnp.full((N_KV_HEADS, ncp - per, GL), NEG, F32)], axis=1)

    def widen(t):
        t = jnp.swapaxes(t, -3, -2)
        return t.reshape(t.shape[:-2] + (N_KV_HEADS * GL,))

    return widen(swa), widen(near), widen(wfar), widen(t31), widen(cmpb)


def _overlap_t(s):
    ncp = s // CMP_STRIDE
    nsel = s // SEL_BLOCK
    cs = np.arange(ncp)[None, :] * CMP_STRIDE
    ss = np.arange(nsel)[:, None] * SEL_BLOCK
    ov = (cs < ss + SEL_BLOCK) & (cs + CMP_LEN > ss) & (np.arange(ncp)[None, :] < ncp - 1)
    return jnp.asarray(ov.astype(np.float32), dtype=BF16)


def _lanes(gain_b, n):
    return jnp.concatenate([gain_b] * (n // gain_b.shape[1]), axis=1)


def _rms_t(xt, gain_b):
    ms = jnp.mean(xt * xt, axis=0, keepdims=True)
    return (xt * lax.rsqrt(ms + EPS) * _lanes(gain_b, xt.shape[1])).astype(BF16)


def _head_norm_t(a, gain_b):
    outs = []
    for h in range(a.shape[0] // HEAD_DIM):
        blk = a[h * HEAD_DIM:(h + 1) * HEAD_DIM]
        ms = jnp.mean(blk * blk, axis=0, keepdims=True)
        outs.append(blk * lax.rsqrt(ms + EPS) * _lanes(gain_b, a.shape[1]))
    return jnp.concatenate(outs, axis=0)


def _sigmoid_h(hz):
    return 0.5 + 0.5 * jnp.tanh(hz)


def _silu_h(hz):
    return hz + hz * jnp.tanh(hz)


def _silu(z):
    return _silu_h(0.5 * z)


def _dot(a, b):
    return jnp.dot(a, b, preferred_element_type=F32)


def _proj_a(xn, w, qg, kg, q_out, k_out, v_out, g_out):
    pq = _dot(w[:HQ, :], xn)
    pkv = _dot(w[HQ:HQ + 2 * HKV, :], xn)
    q_out[0] = _head_norm_t(pq, qg[...]).astype(BF16)
    pz = _dot(w[HQ + 2 * HKV:, :], xn)
    k_out[0] = _head_norm_t(pkv[:HKV], kg[...]).T.astype(BF16)
    v_out[0] = pkv[HKV:].astype(BF16)
    g_out[0] = _silu_h(pz).astype(BF16)


def _proj_b(xn, w, qg, q_out, g_out):
    n_gate = N_BRANCH * N_HEADS
    pq = _dot(w[:HQ, :], xn)
    sg = _sigmoid_h(_dot(w[HQ:HQ + n_gate, :], xn))

    def gate_rows(pz, c):
        for hd in range(N_HEADS):
            r = c * N_HEADS + hd
            z = pz[hd * HEAD_DIM:(hd + 1) * HEAD_DIM]
            g_out[0, r * HEAD_DIM:(r + 1) * HEAD_DIM, :] = (_silu_h(z) * sg[r:r + 1]).astype(BF16)

    z0 = HQ + n_gate
    pz = _dot(w[z0:z0 + HQ, :], xn)
    q_out[0] = _head_norm_t(pq, qg[...]).astype(BF16)
    for c in range(1, N_BRANCH):
        nxt = _dot(w[z0 + c * HQ:z0 + (c + 1) * HQ, :], xn)
        gate_rows(pz, c - 1)
        pz = nxt
    gate_rows(pz, N_BRANCH - 1)


def _proj_kv(xn, wkv, kg1, kg2, craw_out, kslc_out, vslc_out, kwin_out, vwin_out):
    p = _dot(wkv[...], xn)

    def rows(n):
        return p[n * HKV:(n + 1) * HKV]
    for t in range(2):
        nat = rows(t).T
        for pr in range(HKV // PAIR):
            craw_out[0, t, pr] = nat[:, _pair_cols(pr)]
    kslc_out[0] = _head_norm_t(rows(2), kg1[...]).T.astype(BF16)
    vslc_out[0] = rows(3).astype(BF16)
    kwin_out[0] = _head_norm_t(rows(4), kg2[...]).T.astype(BF16)
    vwin_out[0] = rows(5).astype(BF16)


def _first_kernel(x_ref, ng, w, qg, kg, xt_out, q_out, k_out, v_out, g_out):
    sub = TM // 2
    for j in range(TM // sub):
        c = slice(j * sub, (j + 1) * sub)
        xt = x_ref[0, c, :].T
        xt_out[0, :, c] = xt
        _proj_a(_rms_t(xt, ng[...]), w, qg, kg, q_out.at[:, :, c], k_out.at[:, c, :], v_out.at[:, :, c],
                g_out.at[:, :, c])


def _a2a_kernel(x_ref, o_ref, wo, ng, w, qg, kg, xt_out, q_out, k_out, v_out, g_out):
    xt = x_ref[0] + _dot(wo[...], o_ref[0])
    xt_out[0] = xt
    _proj_a(_rms_t(xt, ng[...]), w, qg, kg, q_out, k_out, v_out, g_out)


def _a2b_kernel(x_ref, o_ref, wo, ngkv, wkv, kg1, kg2, ngb, w, qg,
                xt_out, craw_out, kslc_out, vslc_out, kwin_out, vwin_out, q_out, g_out):
    xt = x_ref[0] + _dot(wo[...], o_ref[0])
    xt_out[0] = xt
    _proj_kv(_rms_t(xt, ngkv[...]), wkv, kg1, kg2, craw_out, kslc_out, vslc_out, kwin_out, vwin_out)
    _proj_b(_rms_t(xt, ngb[...]), w, qg, q_out, g_out)


def _b2b_kernel(x_ref, o_ref, wo, ngb, w, qg, xt_out, q_out, g_out):
    xt = x_ref[0] + _dot(wo[...], o_ref[0])
    xt_out[0] = xt
    _proj_b(_rms_t(xt, ngb[...]), w, qg, q_out, g_out)


def _final_kernel(x_ref, o_ref, wo, x_out):
    x_out[0] = (x_ref[0] + _dot(wo[...], o_ref[0])).T


def _tok_spec(rows):
    return pl.BlockSpec((1, rows, TM), lambda b, t: (b, 0, t))


def _nat_spec(cols):
    return pl.BlockSpec((1, TM, cols), lambda b, t: (b, t, 0))


def _const_spec(shape):
    nd = len(shape)
    return pl.BlockSpec(shape, lambda b, t: (0,) * nd, pipeline_mode=pl.Buffered(1))


def _proj_call(body, name, b, s, ins, in_specs, out_shapes, out_specs):
    return pl.pallas_call(
        body, name=name, grid=(b, s // TM),
        in_specs=in_specs, out_specs=out_specs, out_shape=out_shapes,
        compiler_params=pltpu.CompilerParams(
            dimension_semantics=("parallel", "parallel"), vmem_limit_bytes=VMEM_LIMIT),
    )(*ins)


def _bcast(v, scale=1.0):
    return jnp.broadcast_to((v.astype(F32) * scale)[:, None], (v.shape[0], TQ))


def _a_weights(w_in, q_gain, k_gain):
    halve = np.where(np.arange(w_in.shape[1]) >= HQ + 2 * HKV, 0.5, 1.0).astype(np.float32)
    ws = [(w_in * halve).T.astype(BF16)]
    gains = [_bcast(q_gain, Q_SCALE), _bcast(k_gain)]
    return ws, gains


def _b_weights(w_in, q_gain):
    halve = np.where(np.arange(w_in.shape[1]) >= HQ, 0.5, 1.0).astype(np.float32)
    ws = [(w_in * halve).T.astype(BF16)]
    return ws, [_bcast(q_gain, Q_SCALE)]


def _a_outs(b, s):
    shapes = [jax.ShapeDtypeStruct((b, D_MODEL, s), F32), jax.ShapeDtypeStruct((b, HQ, s), BF16),
              jax.ShapeDtypeStruct((b, s, HKV), BF16), jax.ShapeDtypeStruct((b, HKV, s), BF16),
              jax.ShapeDtypeStruct((b, HQ, s), BF16)]
    specs = [_tok_spec(D_MODEL), _tok_spec(HQ), _nat_spec(HKV), _tok_spec(HKV), _tok_spec(HQ)]
    return shapes, specs


def _b_outs(b, s):
    shapes = [jax.ShapeDtypeStruct((b, HQ, s), BF16), jax.ShapeDtypeStruct((b, N_BRANCH * HQ, s), BF16)]
    specs = [_tok_spec(HQ), _tok_spec(N_BRANCH * HQ)]
    return shapes, specs


def _specs_for(arrs):
    return [_const_spec(a.shape) for a in arrs]


def _head_rows(h, g):
    return slice((h * GROUP + g) * HEAD_DIM, (h * GROUP + g + 1) * HEAD_DIM)


def _kv_rows(h):
    return slice(h * HEAD_DIM, (h + 1) * HEAD_DIM)


def _head_lanes(h):
    return slice(h * GL, (h + 1) * GL)


def _pair_lanes(p):
    return slice(2 * p * GL, 2 * (p + 1) * GL)


def _pair_cols(p):
    return slice(p * PAIR, (p + 1) * PAIR)


def _padded_q(q_ref, h, cols=slice(None)):
    qs = jnp.concatenate([q_ref[0, _head_rows(h, g), cols] for g in range(GROUP)], axis=1)
    zeros = jnp.zeros_like(qs)
    return jnp.concatenate([qs, zeros] if h % 2 == 0 else [zeros, qs], axis=0)


def _with_ones(v):
    return jnp.concatenate([v, jnp.ones((ONES_ROWS, v.shape[1]), BF16)], axis=0)


def _exp2_bf16(x):
    return jnp.exp2(x.astype(BF16))


def _swa_kernel(q_ref, kp_ref, kc_ref, vp_ref, vc_ref, g_ref, bias_ref, sink_ref, o_ref):
    i = pl.program_id(1)
    sink = sink_ref[...]
    for j in range(SWA_TILES):
        cur = slice(j * TQ, (j + 1) * TQ)
        prev = slice((j - 1) * TQ, j * TQ)
        k_prev = (lambda c: kp_ref[0, :, c]) if j == 0 else (lambda c: kc_ref[0, prev, c])
        v_prev = (lambda r: vp_ref[0, r, :]) if j == 0 else (lambda r: vc_ref[0, r, prev])
        bias = bias_ref[jnp.minimum(i, 1)] if j == 0 else bias_ref[1]
        pairs = []
        for p in range(N_KV_HEADS // 2):
            k = jnp.concatenate([k_prev(_pair_cols(p)), kc_ref[0, cur, _pair_cols(p)]], axis=0)
            qp = jnp.concatenate([_padded_q(q_ref, 2 * p, cur), _padded_q(q_ref, 2 * p + 1, cur)], axis=1)
            pairs.append(_dot(k, qp))
        s = jnp.concatenate(pairs, axis=1) + bias
        m = jnp.maximum(jnp.max(s, axis=0, keepdims=True), sink)
        e = _exp2_bf16(s - m)
        e_sink = jnp.exp2(sink - m)
        for h in range(N_KV_HEADS):
            v = jnp.concatenate([v_prev(_kv_rows(h)), vc_ref[0, _kv_rows(h), cur]], axis=1)
            acc = _dot(_with_ones(v), e[:, _head_lanes(h)])
            o = acc[:HEAD_DIM] / (acc[HEAD_DIM:HEAD_DIM + 1] + e_sink[:, _head_lanes(h)])
            for g in range(GROUP):
                rows = _head_rows(h, g)
                o_ref[0, rows, cur] = (o[:, g * TQ:(g + 1) * TQ] * g_ref[0, rows, cur].astype(F32)).astype(BF16)


def _swa_attention(qt, k, vt, gt, swa_tab, sink_row):
    b, _, s = qt.shape
    tw = SWA_TILES * TQ
    prev = lambda i: jnp.maximum(SWA_TILES * i - 1, 0)
    return pl.pallas_call(
        _swa_kernel, name="swa_attention", grid=(b, s // tw),
        in_specs=[
            pl.BlockSpec((1, HQ, tw), lambda b_, i: (b_, 0, i)),
            pl.BlockSpec((1, TQ, HKV), lambda b_, i: (b_, prev(i), 0)),
            pl.BlockSpec((1, tw, HKV), lambda b_, i: (b_, i, 0)),
            pl.BlockSpec((1, HKV, TQ), lambda b_, i: (b_, 0, prev(i))),
            pl.BlockSpec((1, HKV, tw), lambda b_, i: (b_, 0, i)),
            pl.BlockSpec((1, HQ, tw), lambda b_, i: (b_, 0, i)),
            pl.BlockSpec(swa_tab.shape, lambda b_, i: (0, 0, 0)),
            pl.BlockSpec((1, N_KV_HEADS * GL), lambda b_, i: (0, 0)),
        ],
        out_specs=pl.BlockSpec((1, HQ, tw), lambda b_, i: (b_, 0, i)),
        out_shape=jax.ShapeDtypeStruct((b, HQ, s), BF16),
        compiler_params=pltpu.CompilerParams(
            dimension_semantics=("parallel", "arbitrary"), vmem_limit_bytes=VMEM_LIMIT),
    )(qt, k, k, vt, vt, gt, swa_tab, sink_row)


def _nsa_kernel(nq, q_ref, g_ref, kcmp_ref, vcmp_ref, kslc_ref, vslc_ref, kwin_ref, vwin_ref,
                cmpb_ref, near_ref, wfar_ref, t31_ref, ovt_ref, kauxf_ref, kauxn_ref, o_ref,
                qp_ref, m_ref, acc_ref, selb_ref, part_ref, sc0_ref, sc1_ref):
    i = pl.program_id(1)
    ncp = kcmp_ref.shape[2]
    nsel = ovt_ref.shape[0]
    n_pairs = N_KV_HEADS // 2
    wide = N_KV_HEADS * GL
    cmp_start = pl.multiple_of((TQ // CMP_STRIDE) * (nq - 1 - i), 8)
    prev_rows = pl.ds(pl.multiple_of(jnp.maximum(i - 1, 0) * TQ, TQ), TQ)
    cur_rows = pl.ds(pl.multiple_of(i * TQ, TQ), TQ)
    far_starts = [pl.multiple_of(jnp.maximum(i - back, 0) * TQ, TQ) for back in range(NSA_WINDOW // TQ, 1, -1)]
    n_far_blocks = (TQ // SEL_BLOCK) * (i - 1)

    def near_keys(k_ref, p):
        return jnp.concatenate([k_ref[0, prev_rows, _pair_cols(p)], k_ref[0, cur_rows, _pair_cols(p)]], axis=0)

    def near_values(v_ref, h):
        return jnp.concatenate([v_ref[0, _kv_rows(h), prev_rows], v_ref[0, _kv_rows(h), cur_rows]], axis=1)

    def scores(lhs_of_pair, rows):
        return jnp.concatenate([_dot(lhs_of_pair(p), qp_ref[rows, _pair_lanes(p)]) for p in range(n_pairs)], axis=1)

    for h in range(N_KV_HEADS):
        qp_ref[:PAIR, _head_lanes(h)] = _padded_q(q_ref, h)
    q_rows = slice(0, PAIR)
    near_tab = near_ref[0]
    bc = cmpb_ref[pl.ds(cmp_start, ncp), :]
    sc = scores(lambda p: kcmp_ref[0, 0, :, _pair_cols(p)], q_rows) + bc
    mc = jnp.max(sc, axis=0, keepdims=True)
    s_near = scores(lambda p: near_keys(kwin_ref, p), q_rows) + near_tab
    s_far = scores(lambda p: jnp.concatenate([kwin_ref[0, pl.ds(st, TQ), _pair_cols(p)] for st in far_starts], axis=0),
                   q_rows) + wfar_ref[0]
    mw = jnp.maximum(jnp.max(s_near, axis=0, keepdims=True), jnp.max(s_far, axis=0, keepdims=True))

    valid = bc > 0.5 * NEG
    ec = jnp.where(valid, jnp.exp2(sc - mc), 0.0)
    lc = jnp.sum(ec, axis=0, keepdims=True)
    pc = ec / jnp.where(lc > 0.0, lc, 1.0)
    pc_b = pc.astype(BF16)
    o_cmp = [_dot(vcmp_ref[0, 0, _kv_rows(h), :], pc_b[:, _head_lanes(h)]) for h in range(N_KV_HEADS)]

    psum = jnp.concatenate(
        [sum(pc[:, h * GL + g * TQ:h * GL + (g + 1) * TQ] for g in range(GROUP)) for h in range(N_KV_HEADS)], axis=1)
    ovt = ovt_ref[...]
    p1 = psum.astype(BF16)
    r1 = psum - p1.astype(F32)
    p2 = r1.astype(BF16)
    p3 = (r1 - p2.astype(F32)).astype(BF16)
    imp = _dot(ovt, p1) + _dot(ovt, p2) + _dot(ovt, p3)
    lane = lax.broadcasted_iota(jnp.int32, imp.shape, 1)
    pos = i * TQ + lane % TQ
    blk = lax.broadcasted_iota(jnp.int32, imp.shape, 0)
    causal = blk * SEL_BLOCK <= pos
    rel = pos // SEL_BLOCK - blk
    forced = (blk == 0) | ((rel >= 0) & (rel < SEL_FORCE_LOCAL))
    score = jnp.where(causal, imp + jnp.where(forced, FORCE_BONUS, 0.0), NEG)
    groups = [score[r:r + 8] for r in range(0, nsel, 8)]
    counts = [jnp.zeros((8, imp.shape[1]), jnp.int32) for _ in groups]
    row_in_group = lax.broadcasted_iota(jnp.int32, (8, imp.shape[1]), 0)
    for jp in range(nsel):
        row = score[jp:jp + 1, :]
        for gi, grp in enumerate(groups):
            if gi * 8 > jp:
                beats = row >= grp
            elif gi * 8 + 7 < jp:
                beats = row > grp
            else:
                beats = (row > grp) | ((row == grp) & (row_in_group > jp - gi * 8))
            counts[gi] = counts[gi] + beats.astype(jnp.int32)
    sel = jnp.concatenate(counts, axis=0) < min(SEL_TOP, nsel)

    def per_group(x):
        return jnp.concatenate([x[:, h * TQ:(h + 1) * TQ] for h in range(N_KV_HEADS) for _ in range(GROUP)], axis=1)

    selb_ref[...] = per_group(jnp.where(sel, 0.0, NEG))
    sel_far = per_group(jnp.where(sel & (blk < n_far_blocks), 0.0, NEG))
    t31 = t31_ref[...]
    hi = t31.astype(BF16).astype(F32)
    row8 = lax.broadcasted_iota(jnp.int32, (8, wide), 0)
    const_rows = jnp.where(row8 == 0, hi, jnp.where(row8 == 1, t31 - hi, 0.0))
    first_blk = (TQ // SEL_BLOCK) * (i - 1)
    near_rows = [selb_ref[pl.ds(jnp.maximum(first_blk + u, 0), 1), :] for u in range(NEAR // SEL_BLOCK)]
    aux = jnp.concatenate(
        [sel_far, const_rows] + near_rows + [jnp.zeros((PAIR - nsel - 8 - NEAR // SEL_BLOCK, wide), F32)], axis=0)
    qp_ref[PAIR:, :] = aux.astype(BF16)

    all_rows = slice(0, 2 * PAIR)
    s = scores(lambda p: jnp.concatenate([near_keys(kslc_ref, p), kauxn_ref[...]], axis=1), all_rows) + near_tab
    m = jnp.max(s, axis=0, keepdims=True)

    sc_refs = (sc0_ref, sc1_ref)

    def sweep_scores(step, dst_ref, cols, lanes):
        keys = slice(step * FAR, (step + 1) * FAR)
        lhs = jnp.concatenate([kslc_ref[0, keys, cols], kauxf_ref[keys, :]], axis=1)
        sc_new = _dot(lhs, qp_ref[:, lanes])
        dst_ref[:FAR, lanes] = sc_new
        dst_ref[FAR:FAR + 1, lanes] = jnp.max(sc_new, axis=0, keepdims=True)

    for p in range(n_pairs):
        sweep_scores(0, sc_refs[0], _pair_cols(p), _pair_lanes(p))

    m_ref[...] = m
    e = _exp2_bf16(s - m)
    for h in range(N_KV_HEADS):
        acc_ref[h] = _dot(_with_ones(near_values(vslc_ref, h)), e[:, _head_lanes(h)])

    e_near = _exp2_bf16(s_near - mw)
    e_far = _exp2_bf16(s_far - mw)
    for h in range(N_KV_HEADS):
        far_v = jnp.concatenate([vwin_ref[0, _kv_rows(h), pl.ds(st, TQ)] for st in far_starts], axis=1)
        acc_w = (_dot(_with_ones(near_values(vwin_ref, h)), e_near[:, _head_lanes(h)])
                 + _dot(_with_ones(far_v), e_far[:, _head_lanes(h)]))
        o_win = acc_w[:HEAD_DIM] / acc_w[HEAD_DIM:HEAD_DIM + 1]
        for g in range(GROUP):
            rows = _head_rows(h, g)
            lanes = slice(g * TQ, (g + 1) * TQ)
            part_ref[rows, :] = (o_cmp[h][:, lanes] * g_ref[0, 0, rows, :].astype(F32)
                                 + o_win[:, lanes] * g_ref[0, 2, rows, :].astype(F32))

    def sweep_consume(step, src_ref, lanes, heads, n_keys=FAR):
        s = src_ref[:n_keys, lanes]
        m_old = m_ref[:, lanes]
        m_new = jnp.maximum(m_old, src_ref[FAR:FAR + 1, lanes])
        alpha = jnp.exp2(m_old - m_new)
        e = _exp2_bf16(s - m_new)
        for j, h in enumerate(heads):
            sub = slice(j * GL, (j + 1) * GL)
            v = _with_ones(vslc_ref[0, _kv_rows(h), step * FAR:step * FAR + n_keys])
            acc_ref[h] = alpha[:, sub] * acc_ref[h] + _dot(v, e[:, sub])
        m_ref[:, lanes] = m_new

    n_far_chunks = jnp.maximum(i - 1, 0)
    n_steps = (n_far_chunks + FAR // TQ - 1) // (FAR // TQ)
    max_steps = kslc_ref.shape[1] // FAR
    for step in range(max_steps):
        if step + 1 < max_steps:
            @pl.when(step + 1 < n_steps)
            def _(step=step):
                for h in range(N_KV_HEADS):
                    sweep_consume(step, sc_refs[step % 2], _head_lanes(h), (h,))
                    sweep_scores(step + 1, sc_refs[(step + 1) % 2], _pair_cols(h // 2), _head_lanes(h))

        for chunks in range(1, FAR // TQ + 1):
            @pl.when(n_far_chunks == step * (FAR // TQ) + chunks)
            def _(step=step, chunks=chunks):
                sweep_consume(step, sc_refs[step % 2], slice(0, wide), tuple(range(N_KV_HEADS)), chunks * TQ)

    for h in range(N_KV_HEADS):
        o_slc = acc_ref[h, :HEAD_DIM, :] / acc_ref[h, HEAD_DIM:HEAD_DIM + 1, :]
        for g in range(GROUP):
            rows = _head_rows(h, g)
            o = part_ref[rows, :] + o_slc[:, g * TQ:(g + 1) * TQ] * g_ref[0, 1, rows, :].astype(F32)
            o_ref[0, rows, :] = o.astype(BF16)


def _nsa_attention(qt, gt, cmp_nat, cmp_t, kslc, vslc, kwin, vwin, tabs, ovt):
    b, _, s = qt.shape
    nq = s // TQ
    assert s % FAR == 0
    ncp = s // CMP_STRIDE
    nsel = s // SEL_BLOCK
    near, wfar, t31, cmpb = tabs
    g4 = gt.reshape(b, N_BRANCH, HQ, s)
    assert nsel + 8 + NEAR // SEL_BLOCK <= PAIR
    cols = np.arange(PAIR)[None, :]
    key_blk = np.arange(s)[:, None] // SEL_BLOCK
    kaux_far = jnp.asarray(((cols == key_blk) | (cols == nsel) | (cols == nsel + 1)).astype(np.float32), BF16)
    near_blk = np.arange(NEAR)[:, None] // SEL_BLOCK
    kaux_near = jnp.asarray((cols == nsel + 8 + near_blk).astype(np.float32), BF16)
    full = lambda shape: pl.BlockSpec(shape, lambda b_, i: (0,) * len(shape))
    return pl.pallas_call(
        functools.partial(_nsa_kernel, nq), name="nsa_attention", grid=(b, nq),
        in_specs=[
            pl.BlockSpec((1, HQ, TQ), lambda b_, i: (b_, 0, i)),
            pl.BlockSpec((1, N_BRANCH, HQ, TQ), lambda b_, i: (b_, 0, 0, i)),
            pl.BlockSpec((1, 1, ncp, HKV), lambda b_, i: (0, b_, 0, 0)),
            pl.BlockSpec((1, 1, HKV, ncp), lambda b_, i: (1, b_, 0, 0)),
            pl.BlockSpec((1, s, HKV), lambda b_, i: (b_, 0, 0)),
            pl.BlockSpec((1, HKV, s), lambda b_, i: (b_, 0, 0)),
            pl.BlockSpec((1, s, HKV), lambda b_, i: (b_, 0, 0)),
            pl.BlockSpec((1, HKV, s), lambda b_, i: (b_, 0, 0)),
            full(cmpb.shape),
            pl.BlockSpec((1, NEAR, N_KV_HEADS * GL), lambda b_, i: (jnp.minimum(i, 1), 0, 0)),
            pl.BlockSpec((1, WIN_FAR, N_KV_HEADS * GL),
                         lambda b_, i: (jnp.minimum(i, N_WIN_VARIANTS - 1), 0, 0)),
            full(t31.shape),
            full((nsel, ncp)),
            full((s, PAIR)), full((NEAR, PAIR)),
        ],
        out_specs=pl.BlockSpec((1, HQ, TQ), lambda b_, i: (b_, 0, i)),
        out_shape=jax.ShapeDtypeStruct((b, HQ, s), BF16),
        scratch_shapes=[pltpu.VMEM((2 * PAIR, N_KV_HEADS * GL), BF16),
                        pltpu.VMEM((1, N_KV_HEADS * GL), F32),
                        pltpu.VMEM((N_KV_HEADS, HEAD_DIM + ONES_ROWS, GL), F32),
                        pltpu.VMEM((nsel, N_KV_HEADS * GL), F32),
                        pltpu.VMEM((HQ, TQ), F32),
                        pltpu.VMEM((FAR + 8, N_KV_HEADS * GL), F32), pltpu.VMEM((FAR + 8, N_KV_HEADS * GL), F32)],
        compiler_params=pltpu.CompilerParams(
            dimension_semantics=("parallel", "arbitrary"), vmem_limit_bytes=VMEM_LIMIT),
    )(qt, g4, cmp_nat, cmp_t, kslc, vslc, kwin, vwin, cmpb, near, wfar, t31, ovt, kaux_far, kaux_near)


def _compress_kernel(u_ref, ptop_ref, pbot_ref, wtop_ref, wbot_ref, w2_ref, kg_ref, nat_out, t_out):
    t = pl.program_id(0)
    nch = u_ref.shape[3] // CMP_STRIDE
    hidden = []
    for p in range(HKV // PAIR):
        u = jnp.concatenate([u_ref[0, 0, p, pl.ds(j, nch, stride=CMP_STRIDE), :] for j in range(CMP_STRIDE)], axis=1)
        top = (u + ptop_ref[0]).astype(BF16)
        bot = (u + pbot_ref[0]).astype(BF16)
        for e in range(2):
            bm = _dot(bot, wbot_ref[0, e])
            hidden.append(_dot(top, wtop_ref[0, e]) + pltpu.roll(bm, nch - 1, axis=0))
    hid = jnp.concatenate(hidden, axis=1)
    out_t = _dot(_silu(hid).astype(BF16), w2_ref[0]).T
    out_t = jnp.where(t == 0, _head_norm_t(out_t, kg_ref[...]), out_t)
    t_out[0, 0] = out_t.astype(BF16)
    nat_out[0, 0] = out_t.T.astype(BF16)


def _compress(craw, cmp_k_pos, cmp_k_w1, cmp_k_w2, cmp_v_pos, cmp_v_w1, cmp_v_w2, k_gain):
    b, _, n_pairs, s, _ = craw.shape
    nch = s // CMP_STRIDE
    wide = CMP_STRIDE * PAIR

    def padded_w1(w1_half):
        w = w1_half.reshape(CMP_STRIDE, HEAD_DIM, CMP_HIDDEN)
        z = jnp.zeros_like(w)
        both = jnp.stack([jnp.concatenate([w, z], axis=1), jnp.concatenate([z, w], axis=1)])
        return both.reshape(2, wide, CMP_HIDDEN).astype(BF16)

    def pos_row(p_half):
        return jnp.broadcast_to(p_half[:, None, :], (CMP_STRIDE, 2, HEAD_DIM)).reshape(1, wide)

    def big_w2(w2):
        eye = jnp.eye(N_KV_HEADS, dtype=F32)
        return jnp.einsum('nd,hk->hnkd', w2, eye).reshape(N_KV_HEADS * CMP_HIDDEN, HKV).astype(BF16)

    half = CMP_STRIDE * HEAD_DIM
    wtop = jnp.stack([padded_w1(cmp_k_w1[:half]), padded_w1(cmp_v_w1[:half])])
    wbot = jnp.stack([padded_w1(cmp_k_w1[half:]), padded_w1(cmp_v_w1[half:])])
    ptop = jnp.stack([pos_row(cmp_k_pos[:CMP_STRIDE]), pos_row(cmp_v_pos[:CMP_STRIDE])])
    pbot = jnp.stack([pos_row(cmp_k_pos[CMP_STRIDE:]), pos_row(cmp_v_pos[CMP_STRIDE:])])
    w2 = jnp.stack([big_w2(cmp_k_w2), big_w2(cmp_v_w2)])
    kg = jnp.broadcast_to(k_gain.astype(F32)[:, None], (HEAD_DIM, nch))
    sel = lambda shape: pl.BlockSpec((1,) + shape, lambda t, b_: (t,) + (0,) * len(shape))
    return pl.pallas_call(
        _compress_kernel, name="compress", grid=(2, b),
        in_specs=[
            pl.BlockSpec((1, 1, n_pairs, s, PAIR), lambda t, b_: (b_, t, 0, 0, 0)),
            sel((1, wide)), sel((1, wide)),
            sel((2, wide, CMP_HIDDEN)), sel((2, wide, CMP_HIDDEN)),
            sel((N_KV_HEADS * CMP_HIDDEN, HKV)),
            pl.BlockSpec((HEAD_DIM, nch), lambda t, b_: (0, 0)),
        ],
        out_specs=[pl.BlockSpec((1, 1, nch, HKV), lambda t, b_: (t, b_, 0, 0)),
                   pl.BlockSpec((1, 1, HKV, nch), lambda t, b_: (t, b_, 0, 0))],
        out_shape=[jax.ShapeDtypeStruct((2, b, nch, HKV), BF16),
                   jax.ShapeDtypeStruct((2, b, HKV, nch), BF16)],
        compiler_params=pltpu.CompilerParams(
            dimension_semantics=("arbitrary", "arbitrary"), vmem_limit_bytes=VMEM_LIMIT),
    )(craw, ptop, pbot, wtop, wbot, w2, kg)


def kernel(x, rel_table, a_norm, a_w_in, a_q_gain, a_k_gain, a_sink, a_w_out, kv_norm, kv_w,
           kv_k_gain, cmp_k_pos, cmp_k_w1, cmp_k_w2, cmp_v_pos, cmp_v_w1, cmp_v_w2,
           b_norm, b_w_in, b_q_gain, b_w_out):
    b, s, _ = x.shape
    nq = s // TQ
    n_a = a_w_in.shape[0]
    n_b = b_w_in.shape[0]
    swa_tab, near, wfar, t31, cmpb = _make_tables(rel_table.astype(F32) * LOG2E, nq)
    ovt = _overlap_t(s)

    a_shapes, a_specs = _a_outs(b, s)
    b_shapes, b_specs = _b_outs(b, s)
    xt_shape, xt_spec = a_shapes[0], a_specs[0]

    ws, gains = _a_weights(a_w_in[0], a_q_gain[0], a_k_gain[0])
    consts = [_bcast(a_norm[0])] + ws + gains
    xt, qt, k, vt, gt = _proj_call(
        _first_kernel, "proj_first", b, s, [x] + consts,
        [_nat_spec(D_MODEL)] + _specs_for(consts), a_shapes, a_specs)

    for layer in range(n_a):
        sink_row = jnp.broadcast_to(
            (a_sink[layer].astype(F32) * LOG2E).reshape(N_KV_HEADS, 1, GROUP, 1), (N_KV_HEADS, 1, GROUP, TQ)
        ).reshape(1, N_KV_HEADS * GL)
        ot = _swa_attention(qt, k, vt, gt, swa_tab, sink_row)
        wo = a_w_out[layer].T.astype(BF16)
        if layer + 1 < n_a:
            ws, gains = _a_weights(a_w_in[layer + 1], a_q_gain[layer + 1], a_k_gain[layer + 1])
            consts = [wo, _bcast(a_norm[layer + 1])] + ws + gains
            xt, qt, k, vt, gt = _proj_call(
                _a2a_kernel, "proj_a2a", b, s, [xt, ot] + consts,
                [_tok_spec(D_MODEL), _tok_spec(HQ)] + _specs_for(consts), a_shapes, a_specs)
        else:
            wsb, gb = _b_weights(b_w_in[0], b_q_gain[0])
            consts = ([wo, _bcast(kv_norm), kv_w.T.astype(BF16), _bcast(kv_k_gain[1]), _bcast(kv_k_gain[2]),
                       _bcast(b_norm[0])] + wsb + gb)
            kv_shapes = [jax.ShapeDtypeStruct((b, 2, HKV // PAIR, s, PAIR), F32),
                         jax.ShapeDtypeStruct((b, s, HKV), BF16), jax.ShapeDtypeStruct((b, HKV, s), BF16),
                         jax.ShapeDtypeStruct((b, s, HKV), BF16), jax.ShapeDtypeStruct((b, HKV, s), BF16)]
            kv_specs = [pl.BlockSpec((1, 2, HKV // PAIR, TM, PAIR), lambda b_, t: (b_, 0, 0, t, 0)),
                        _nat_spec(HKV), _tok_spec(HKV), _nat_spec(HKV), _tok_spec(HKV)]
            xt, craw, kslc, vslc, kwin, vwin, qt, gt = _proj_call(
                _a2b_kernel, "proj_a2b", b, s, [xt, ot] + consts,
                [_tok_spec(D_MODEL), _tok_spec(HQ)] + _specs_for(consts),
                [xt_shape] + kv_shapes + b_shapes, [xt_spec] + kv_specs + b_specs)

    cmp_nat, cmp_t = _compress(craw, cmp_k_pos, cmp_k_w1, cmp_k_w2, cmp_v_pos, cmp_v_w1, cmp_v_w2,
                               kv_k_gain[0])
    tabs = (near, wfar, t31, cmpb)
    for layer in range(n_b):
        ot = _nsa_attention(qt, gt, cmp_nat, cmp_t, kslc, vslc, kwin, vwin, tabs, ovt)
        wo = b_w_out[layer].T.astype(BF16)
        if layer + 1 < n_b:
            wsb, gb = _b_weights(b_w_in[layer + 1], b_q_gain[layer + 1])
            consts = [wo, _bcast(b_norm[layer + 1])] + wsb + gb
            xt, qt, gt = _proj_call(
                _b2b_kernel, "proj_b2b", b, s, [xt, ot] + consts,
                [_tok_spec(D_MODEL), _tok_spec(HQ)] + _specs_for(consts),
                [xt_shape] + b_shapes, [xt_spec] + b_specs)
        else:
            out = _proj_call(
                _final_kernel, "proj_final", b, s, [xt, ot, wo],
                [_tok_spec(D_MODEL), _tok_spec(HQ), _const_spec(wo.shape)],
                jax.ShapeDtypeStruct((b, s, D_MODEL), F32), _nat_spec(D_MODEL))
    return out
```

```python
import functools
import math

import numpy as np
import jax
import jax.numpy as jnp
from jax import lax
from jax.experimental import pallas as pl
from jax.experimental.pallas import tpu as pltpu

D_MODEL = 1024
HEAD_DIM = 64
N_HEADS = 16
N_KV_HEADS = 4
GROUP = N_HEADS // N_KV_HEADS
HQ = N_HEADS * HEAD_DIM
HKV = N_KV_HEADS * HEAD_DIM
N_BRANCH = 3
SWA_WINDOW = 128
NSA_WINDOW = 512
CMP_LEN = 32
CMP_STRIDE = 16
CMP_HIDDEN = 256
SEL_BLOCK = 64
SEL_TOP = 8
SEL_FORCE_LOCAL = 2
NUM_BUCKETS = 32
MAX_DISTANCE = 128
EPS = 1e-6
NEG = -1e30
FORCE_BONUS = 1e6

TQ = 128
TM = 512
SWA_TILES = 8
NSA_TILES = 2
GL = GROUP * TQ
PAIR = 2 * HEAD_DIM
NEAR = 2 * TQ
WIN_FAR = NSA_WINDOW - TQ
N_WIN_VARIANTS = NSA_WINDOW // TQ + 1
FAR = 4 * TQ
ONES_ROWS = 16
LOG2E = math.log2(math.e)
Q_SCALE = HEAD_DIM ** -0.5 * LOG2E
VMEM_LIMIT = 56 * 1024 * 1024

F32 = jnp.float32
BF16 = jnp.bfloat16


def _bucket_np(dist):
    d = np.maximum(dist, 0)
    max_exact = NUM_BUCKETS // 2
    ratio = (np.log(np.maximum(d, 1).astype(np.float32) / np.float32(max_exact))
             / np.float32(math.log(MAX_DISTANCE / max_exact))
             * np.float32(NUM_BUCKETS - max_exact))
    large = np.minimum(max_exact + ratio.astype(np.int32), NUM_BUCKETS - 1)
    return np.where(d < max_exact, d, large).astype(np.int32)


def _dist_vector(table, d_lo, length, hi_valid):
    d = d_lo + np.arange(length)
    v = table.T[:, _bucket_np(d)]
    return jnp.where(jnp.asarray((d >= 0) & (d < hi_valid))[None], v, NEG)


def _skew(v, n, step):
    length = v.shape[1]
    assert step * (n - 1) + TQ <= length
    width = length + step
    reps = -(-(n * width) // length)
    w = jnp.tile(v, (1, reps))[:, :n * width].reshape(v.shape[0], n, width)
    return w[:, :, :TQ]


def _toeplitz(table, n_rows, d_first, step, hi_valid):
    length = step * (n_rows - 1) + TQ
    v = _dist_vector(table, d_first - step * (n_rows - 1), length, hi_valid)
    t = _skew(v, n_rows, step)[:, ::-1, :]
    t = t.reshape(N_KV_HEADS, GROUP, n_rows, TQ).transpose(0, 2, 1, 3)
    return t.reshape(N_KV_HEADS, n_rows, GL).astype(F32)


def _mask_rows(t, n_masked):
    rows = np.arange(t.shape[1])[None, :, None] < n_masked
    return jnp.where(jnp.asarray(rows), NEG, t)


def _make_tables(table, nq):
    big = 1 << 30
    near = _toeplitz(table, NEAR, TQ, 1, big)
    in_window = (TQ - np.arange(NEAR)[:, None] + np.arange(GL)[None, :] % TQ) < SWA_WINDOW
    swa = jnp.where(jnp.asarray(in_window)[None], near, NEG)
    near = jnp.stack([_mask_rows(near, TQ), near])
    swa = jnp.stack([_mask_rows(swa, TQ), swa])
    t31 = _toeplitz(table, 1, MAX_DISTANCE, 1, big)

    def saturated(rows):
        return jnp.broadcast_to(t31, (N_KV_HEADS, rows, GL))

    wfar = jnp.concatenate([_toeplitz(table, TQ, NSA_WINDOW, 1, NSA_WINDOW), saturated(WIN_FAR - TQ)], axis=1)
    wfar = jnp.stack([_mask_rows(wfar, min(WIN_FAR, (N_WIN_VARIANTS - 1 - v) * TQ))
                      for v in range(N_WIN_VARIANTS)])
    per = TQ // CMP_STRIDE
    ncp = nq * per
    off = per * (nq - 1)
    sat_c = -(-(MAX_DISTANCE + CMP_LEN - 1) // CMP_STRIDE)
    assert off >= sat_c
    band = _toeplitz(table, sat_c + per - 1, CMP_STRIDE * (sat_c - 1) - (CMP_LEN - 1), CMP_STRIDE, big)
    cmpb = jnp.concatenate(
        [saturated(off - sat_c + 1), band, jnp.full((N_KV_HEADS, ncp - per, GL), NEG, F32)], axis=1)

    def widen(t):
        t = jnp.swapaxes(t, -3, -2)
        return t.reshape(t.shape[:-2] + (N_KV_HEADS * GL,))

    return widen(swa), widen(near), widen(wfar), widen(t31), widen(cmpb)


def _overlap_t(s):
    ncp = s // CMP_STRIDE
    nsel = s // SEL_BLOCK
    cs = np.arange(ncp)[None, :] * CMP_STRIDE
    ss = np.arange(nsel)[:, None] * SEL_BLOCK
    ov = (cs < ss + SEL_BLOCK) & (cs + CMP_LEN > ss) & (np.arange(ncp)[None, :] < ncp - 1)
    return jnp.asarray(ov.astype(np.float32), dtype=BF16)


def _lanes(gain_b, n):
    return jnp.concatenate([gain_b] * (n // gain_b.shape[1]), axis=1)


def _rms_t(xt, gain_b):
    ms = jnp.mean(xt * xt, axis=0, keepdims=True)
    return (xt * lax.rsqrt(ms + EPS) * _lanes(gain_b, xt.shape[1])).astype(BF16)


def _head_norm_t(a, gain_b):
    outs = []
    for h in range(a.shape[0] // HEAD_DIM):
        blk = a[h * HEAD_DIM:(h + 1) * HEAD_DIM]
        ms = jnp.mean(blk * blk, axis=0, keepdims=True)
        outs.append(blk * lax.rsqrt(ms + EPS) * _lanes(gain_b, a.shape[1]))
    return jnp.concatenate(outs, axis=0)


def _sigmoid_h(hz):
    return 0.5 + 0.5 * jnp.tanh(hz)


def _silu_h(hz):
    return hz + hz * jnp.tanh(hz)


def _silu(z):
    return _silu_h(0.5 * z)


def _dot(a, b):
    return jnp.dot(a, b, preferred_element_type=F32)


def _proj_a(xn, w, qg, kg, q_out, k_out, v_out, g_out):
    pq = _dot(w[:HQ, :], xn)
    pkv = _dot(w[HQ:HQ + 2 * HKV, :], xn)
    q_out[0] = _head_norm_t(pq, qg[...]).astype(BF16)
    pz = _dot(w[HQ + 2 * HKV:, :], xn)
    k_out[0] = _head_norm_t(pkv[:HKV], kg[...]).T.astype(BF16)
    v_out[0] = pkv[HKV:].astype(BF16)
    g_out[0] = _silu_h(pz).astype(BF16)


def _proj_b(xn, w, qg, q_out, g_out):
    n_gate = N_BRANCH * N_HEADS
    pq = _dot(w[:HQ, :], xn)
    sg = _sigmoid_h(_dot(w[HQ:HQ + n_gate, :], xn))

    def gate_rows(pz, c):
        for hd in range(N_HEADS):
            r = c * N_HEADS + hd
            z = pz[hd * HEAD_DIM:(hd + 1) * HEAD_DIM]
            g_out[0, r * HEAD_DIM:(r + 1) * HEAD_DIM, :] = (_silu_h(z) * sg[r:r + 1]).astype(BF16)

    z0 = HQ + n_gate
    pz = _dot(w[z0:z0 + HQ, :], xn)
    q_out[0] = _head_norm_t(pq, qg[...]).astype(BF16)
    for c in range(1, N_BRANCH):
        nxt = _dot(w[z0 + c * HQ:z0 + (c + 1) * HQ, :], xn)
        gate_rows(pz, c - 1)
        pz = nxt
    gate_rows(pz, N_BRANCH - 1)


def _proj_kv(xn, wkv, kg1, kg2, craw_out, kslc_out, vslc_out, kwin_out, vwin_out):
    p = _dot(wkv[...], xn)

    def rows(n):
        return p[n * HKV:(n + 1) * HKV]
    for t in range(2):
        nat = rows(t).T
        for pr in range(HKV // PAIR):
            craw_out[0, t, pr] = nat[:, _pair_cols(pr)]
    kslc_out[0] = _head_norm_t(rows(2), kg1[...]).T.astype(BF16)
    vslc_out[0] = rows(3).astype(BF16)
    kwin_out[0] = _head_norm_t(rows(4), kg2[...]).T.astype(BF16)
    vwin_out[0] = rows(5).astype(BF16)


def _first_kernel(x_ref, ng, w, qg, kg, xt_out, q_out, k_out, v_out, g_out):
    sub = TM // 2
    for j in range(TM // sub):
        c = slice(j * sub, (j + 1) * sub)
        xt = x_ref[0, c, :].T
        xt_out[0, :, c] = xt
        _proj_a(_rms_t(xt, ng[...]), w, qg, kg, q_out.at[:, :, c], k_out.at[:, c, :], v_out.at[:, :, c],
                g_out.at[:, :, c])


def _a2a_kernel(x_ref, o_ref, wo, ng, w, qg, kg, xt_out, q_out, k_out, v_out, g_out):
    xt = x_ref[0] + _dot(wo[...], o_ref[0])
    xt_out[0] = xt
    _proj_a(_rms_t(xt, ng[...]), w, qg, kg, q_out, k_out, v_out, g_out)


def _a2b_kernel(x_ref, o_ref, wo, ngkv, wkv, kg1, kg2, ngb, w, qg,
                xt_out, craw_out, kslc_out, vslc_out, kwin_out, vwin_out, q_out, g_out):
    xt = x_ref[0] + _dot(wo[...], o_ref[0])
    xt_out[0] = xt
    _proj_kv(_rms_t(xt, ngkv[...]), wkv, kg1, kg2, craw_out, kslc_out, vslc_out, kwin_out, vwin_out)
    _proj_b(_rms_t(xt, ngb[...]), w, qg, q_out, g_out)


def _b2b_kernel(x_ref, o_ref, wo, ngb, w, qg, xt_out, q_out, g_out):
    xt = x_ref[0] + _dot(wo[...], o_ref[0])
    xt_out[0] = xt
    _proj_b(_rms_t(xt, ngb[...]), w, qg, q_out, g_out)


def _final_kernel(x_ref, o_ref, wo, x_out):
    x_out[0] = (x_ref[0] + _dot(wo[...], o_ref[0])).T


def _tok_spec(rows):
    return pl.BlockSpec((1, rows, TM), lambda b, t: (b, 0, t))


def _nat_spec(cols):
    return pl.BlockSpec((1, TM, cols), lambda b, t: (b, t, 0))


def _const_spec(shape):
    nd = len(shape)
    return pl.BlockSpec(shape, lambda b, t: (0,) * nd, pipeline_mode=pl.Buffered(1))


def _proj_call(body, name, b, s, ins, in_specs, out_shapes, out_specs):
    return pl.pallas_call(
        body, name=name, grid=(b, s // TM),
        in_specs=in_specs, out_specs=out_specs, out_shape=out_shapes,
        compiler_params=pltpu.CompilerParams(
            dimension_semantics=("parallel", "parallel"), vmem_limit_bytes=VMEM_LIMIT),
    )(*ins)


def _bcast(v, scale=1.0):
    return jnp.broadcast_to((v.astype(F32) * scale)[:, None], (v.shape[0], TQ))


def _a_weights(w_in, q_gain, k_gain):
    halve = np.where(np.arange(w_in.shape[1]) >= HQ + 2 * HKV, 0.5, 1.0).astype(np.float32)
    ws = [(w_in * halve).T.astype(BF16)]
    gains = [_bcast(q_gain, Q_SCALE), _bcast(k_gain)]
    return ws, gains


def _b_weights(w_in, q_gain):
    halve = np.where(np.arange(w_in.shape[1]) >= HQ, 0.5, 1.0).astype(np.float32)
    ws = [(w_in * halve).T.astype(BF16)]
    return ws, [_bcast(q_gain, Q_SCALE)]


def _a_outs(b, s):
    shapes = [jax.ShapeDtypeStruct((b, D_MODEL, s), F32), jax.ShapeDtypeStruct((b, HQ, s), BF16),
              jax.ShapeDtypeStruct((b, s, HKV), BF16), jax.ShapeDtypeStruct((b, HKV, s), BF16),
              jax.ShapeDtypeStruct((b, HQ, s), BF16)]
    specs = [_tok_spec(D_MODEL), _tok_spec(HQ), _nat_spec(HKV), _tok_spec(HKV), _tok_spec(HQ)]
    return shapes, specs


def _b_outs(b, s):
    shapes = [jax.ShapeDtypeStruct((b, HQ, s), BF16), jax.ShapeDtypeStruct((b, N_BRANCH * HQ, s), BF16)]
    specs = [_tok_spec(HQ), _tok_spec(N_BRANCH * HQ)]
    return shapes, specs


def _specs_for(arrs):
    return [_const_spec(a.shape) for a in arrs]


def _head_rows(h, g):
    return slice((h * GROUP + g) * HEAD_DIM, (h * GROUP + g + 1) * HEAD_DIM)


def _kv_rows(h):
    return slice(h * HEAD_DIM, (h + 1) * HEAD_DIM)


def _head_lanes(h):
    return slice(h * GL, (h + 1) * GL)


def _pair_lanes(p):
    return slice(2 * p * GL, 2 * (p + 1) * GL)


def _pair_cols(p):
    return slice(p * PAIR, (p + 1) * PAIR)


def _padded_q(q_ref, h, cols=slice(None)):
    qs = jnp.concatenate([q_ref[0, _head_rows(h, g), cols] for g in range(GROUP)], axis=1)
    zeros = jnp.zeros_like(qs)
    return jnp.concatenate([qs, zeros] if h % 2 == 0 else [zeros, qs], axis=0)


def _with_ones(v):
    return jnp.concatenate([v, jnp.ones((ONES_ROWS, v.shape[1]), BF16)], axis=0)


def _exp2_bf16(x):
    return jnp.exp2(x.astype(BF16))


def _swa_kernel(q_ref, kp_ref, kc_ref, vp_ref, vc_ref, g_ref, bias_ref, sink_ref, o_ref):
    i = pl.program_id(1)
    sink = sink_ref[...]
    for j in range(SWA_TILES):
        cur = slice(j * TQ, (j + 1) * TQ)
        prev = slice((j - 1) * TQ, j * TQ)
        k_prev = (lambda c: kp_ref[0, :, c]) if j == 0 else (lambda c: kc_ref[0, prev, c])
        v_prev = (lambda r: vp_ref[0, r, :]) if j == 0 else (lambda r: vc_ref[0, r, prev])
        bias = bias_ref[jnp.minimum(i, 1)] if j == 0 else bias_ref[1]
        pairs = []
        for p in range(N_KV_HEADS // 2):
            k = jnp.concatenate([k_prev(_pair_cols(p)), kc_ref[0, cur, _pair_cols(p)]], axis=0)
            qp = jnp.concatenate([_padded_q(q_ref, 2 * p, cur), _padded_q(q_ref, 2 * p + 1, cur)], axis=1)
            pairs.append(_dot(k, qp))
        s = jnp.concatenate(pairs, axis=1) + bias
        m = jnp.maximum(jnp.max(s, axis=0, keepdims=True), sink)
        e = _exp2_bf16(s - m)
        e_sink = jnp.exp2(sink - m)
        for h in range(N_KV_HEADS):
            v = jnp.concatenate([v_prev(_kv_rows(h)), vc_ref[0, _kv_rows(h), cur]], axis=1)
            acc = _dot(_with_ones(v), e[:, _head_lanes(h)])
            o = acc[:HEAD_DIM] / (acc[HEAD_DIM:HEAD_DIM + 1] + e_sink[:, _head_lanes(h)])
            for g in range(GROUP):
                rows = _head_rows(h, g)
                o_ref[0, rows, cur] = (o[:, g * TQ:(g + 1) * TQ] * g_ref[0, rows, cur].astype(F32)).astype(BF16)


def _swa_attention(qt, k, vt, gt, swa_tab, sink_row):
    b, _, s = qt.shape
    tw = SWA_TILES * TQ
    prev = lambda i: jnp.maximum(SWA_TILES * i - 1, 0)
    return pl.pallas_call(
        _swa_kernel, name="swa_attention", grid=(b, s // tw),
        in_specs=[
            pl.BlockSpec((1, HQ, tw), lambda b_, i: (b_, 0, i)),
            pl.BlockSpec((1, TQ, HKV), lambda b_, i: (b_, prev(i), 0)),
            pl.BlockSpec((1, tw, HKV), lambda b_, i: (b_, i, 0)),
            pl.BlockSpec((1, HKV, TQ), lambda b_, i: (b_, 0, prev(i))),
            pl.BlockSpec((1, HKV, tw), lambda b_, i: (b_, 0, i)),
            pl.BlockSpec((1, HQ, tw), lambda b_, i: (b_, 0, i)),
            pl.BlockSpec(swa_tab.shape, lambda b_, i: (0, 0, 0)),
            pl.BlockSpec((1, N_KV_HEADS * GL), lambda b_, i: (0, 0)),
        ],
        out_specs=pl.BlockSpec((1, HQ, tw), lambda b_, i: (b_, 0, i)),
        out_shape=jax.ShapeDtypeStruct((b, HQ, s), BF16),
        compiler_params=pltpu.CompilerParams(
            dimension_semantics=("parallel", "arbitrary"), vmem_limit_bytes=VMEM_LIMIT),
    )(qt, k, k, vt, vt, gt, swa_tab, sink_row)


def _nsa_kernel(nq, *refs):
    for t in range(NSA_TILES):
        _nsa_tile(nq, pl.program_id(1) * NSA_TILES + t, slice(t * TQ, (t + 1) * TQ), *refs)


def _nsa_tile(nq, i, cols, q_ref, g_ref, kcmp_ref, vcmp_ref, kslc_ref, vslc_ref, kwin_ref, vwin_ref,
              cmpb_ref, near_ref, wfar_ref, t31_ref, ovt_ref, kauxf_ref, kauxn_ref, o_ref,
              qp_ref, m_ref, acc_ref, selb_ref, part_ref, sc0_ref, sc1_ref):
    ncp = kcmp_ref.shape[2]
    nsel = ovt_ref.shape[0]
    n_pairs = N_KV_HEADS // 2
    wide = N_KV_HEADS * GL
    cmp_start = pl.multiple_of((TQ // CMP_STRIDE) * (nq - 1 - i), 8)
    prev_rows = pl.ds(pl.multiple_of(jnp.maximum(i - 1, 0) * TQ, TQ), TQ)
    cur_rows = pl.ds(pl.multiple_of(i * TQ, TQ), TQ)
    far_starts = [pl.multiple_of(jnp.maximum(i - back, 0) * TQ, TQ) for back in range(NSA_WINDOW // TQ, 1, -1)]
    n_far_blocks = (TQ // SEL_BLOCK) * (i - 1)

    def near_keys(k_ref, p):
        return jnp.concatenate([k_ref[0, prev_rows, _pair_cols(p)], k_ref[0, cur_rows, _pair_cols(p)]], axis=0)

    def near_values(v_ref, h):
        return jnp.concatenate([v_ref[0, _kv_rows(h), prev_rows], v_ref[0, _kv_rows(h), cur_rows]], axis=1)

    def scores(lhs_of_pair, rows):
        return jnp.concatenate([_dot(lhs_of_pair(p), qp_ref[rows, _pair_lanes(p)]) for p in range(n_pairs)], axis=1)

    for h in range(N_KV_HEADS):
        qp_ref[:PAIR, _head_lanes(h)] = _padded_q(q_ref, h, cols)
    q_rows = slice(0, PAIR)
    near_tab = near_ref[jnp.minimum(i, 1)]
    bc = cmpb_ref[pl.ds(cmp_start, ncp), :]
    sc = scores(lambda p: kcmp_ref[0, 0, :, _pair_cols(p)], q_rows) + bc
    mc = jnp.max(sc, axis=0, keepdims=True)
    s_near = scores(lambda p: near_keys(kwin_ref, p), q_rows) + near_tab
    s_far = scores(lambda p: jnp.concatenate([kwin_ref[0, pl.ds(st, TQ), _pair_cols(p)] for st in far_starts], axis=0),
                   q_rows) + wfar_ref[jnp.minimum(i, N_WIN_VARIANTS - 1)]
    mw = jnp.maximum(jnp.max(s_near, axis=0, keepdims=True), jnp.max(s_far, axis=0, keepdims=True))

    valid = bc > 0.5 * NEG
    ec = jnp.where(valid, jnp.exp2(sc - mc), 0.0)
    lc = jnp.sum(ec, axis=0, keepdims=True)
    pc = ec / jnp.where(lc > 0.0, lc, 1.0)
    pc_b = pc.astype(BF16)
    o_cmp = [_dot(vcmp_ref[0, 0, _kv_rows(h), :], pc_b[:, _head_lanes(h)]) for h in range(N_KV_HEADS)]

    psum = jnp.concatenate(
        [sum(pc[:, h * GL + g * TQ:h * GL + (g + 1) * TQ] for g in range(GROUP)) for h in range(N_KV_HEADS)], axis=1)
    ovt = ovt_ref[...]
    p1 = psum.astype(BF16)
    r1 = psum - p1.astype(F32)
    p2 = r1.astype(BF16)
    p3 = (r1 - p2.astype(F32)).astype(BF16)
    imp = _dot(ovt, p1) + _dot(ovt, p2) + _dot(ovt, p3)
    lane = lax.broadcasted_iota(jnp.int32, imp.shape, 1)
    pos = i * TQ + lane % TQ
    blk = lax.broadcasted_iota(jnp.int32, imp.shape, 0)
    causal = blk * SEL_BLOCK <= pos
    rel = pos // SEL_BLOCK - blk
    forced = (blk == 0) | ((rel >= 0) & (rel < SEL_FORCE_LOCAL))
    score = jnp.where(causal, imp + jnp.where(forced, FORCE_BONUS, 0.0), NEG)
    groups = [score[r:r + 8] for r in range(0, nsel, 8)]
    counts = [jnp.zeros((8, imp.shape[1]), jnp.int32) for _ in groups]
    row_in_group = lax.broadcasted_iota(jnp.int32, (8, imp.shape[1]), 0)
    for jp in range(nsel):
        row = score[jp:jp + 1, :]
        for gi, grp in enumerate(groups):
            if gi * 8 > jp:
                beats = row >= grp
            elif gi * 8 + 7 < jp:
                beats = row > grp
            else:
                beats = (row > grp) | ((row == grp) & (row_in_group > jp - gi * 8))
            counts[gi] = counts[gi] + beats.astype(jnp.int32)
    sel = jnp.concatenate(counts, axis=0) < min(SEL_TOP, nsel)

    def per_group(x):
        return jnp.concatenate([x[:, h * TQ:(h + 1) * TQ] for h in range(N_KV_HEADS) for _ in range(GROUP)], axis=1)

    selb_ref[...] = per_group(jnp.where(sel, 0.0, NEG))
    sel_far = per_group(jnp.where(sel & (blk < n_far_blocks), 0.0, NEG))
    t31 = t31_ref[...]
    hi = t31.astype(BF16).astype(F32)
    row8 = lax.broadcasted_iota(jnp.int32, (8, wide), 0)
    const_rows = jnp.where(row8 == 0, hi, jnp.where(row8 == 1, t31 - hi, 0.0))
    first_blk = (TQ // SEL_BLOCK) * (i - 1)
    near_rows = [selb_ref[pl.ds(jnp.maximum(first_blk + u, 0), 1), :] for u in range(NEAR // SEL_BLOCK)]
    aux = jnp.concatenate(
        [sel_far, const_rows] + near_rows + [jnp.zeros((PAIR - nsel - 8 - NEAR // SEL_BLOCK, wide), F32)], axis=0)
    qp_ref[PAIR:, :] = aux.astype(BF16)

    all_rows = slice(0, 2 * PAIR)
    s = scores(lambda p: jnp.concatenate([near_keys(kslc_ref, p), kauxn_ref[...]], axis=1), all_rows) + near_tab
    m = jnp.max(s, axis=0, keepdims=True)

    sc_refs = (sc0_ref, sc1_ref)

    def sweep_scores(step, dst_ref, cols, lanes):
        keys = slice(step * FAR, (step + 1) * FAR)
        lhs = jnp.concatenate([kslc_ref[0, keys, cols], kauxf_ref[keys, :]], axis=1)
        sc_new = _dot(lhs, qp_ref[:, lanes])
        dst_ref[:FAR, lanes] = sc_new
        dst_ref[FAR:FAR + 1, lanes] = jnp.max(sc_new, axis=0, keepdims=True)

    for p in range(n_pairs):
        sweep_scores(0, sc_refs[0], _pair_cols(p), _pair_lanes(p))

    m_ref[...] = m
    e = _exp2_bf16(s - m)
    for h in range(N_KV_HEADS):
        acc_ref[h] = _dot(_with_ones(near_values(vslc_ref, h)), e[:, _head_lanes(h)])

    e_near = _exp2_bf16(s_near - mw)
    e_far = _exp2_bf16(s_far - mw)
    for h in range(N_KV_HEADS):
        far_v = jnp.concatenate([vwin_ref[0, _kv_rows(h), pl.ds(st, TQ)] for st in far_starts], axis=1)
        acc_w = (_dot(_with_ones(near_values(vwin_ref, h)), e_near[:, _head_lanes(h)])
                 + _dot(_with_ones(far_v), e_far[:, _head_lanes(h)]))
        o_win = acc_w[:HEAD_DIM] / acc_w[HEAD_DIM:HEAD_DIM + 1]
        for g in range(GROUP):
            rows = _head_rows(h, g)
            lanes = slice(g * TQ, (g + 1) * TQ)
            part_ref[rows, :] = (o_cmp[h][:, lanes] * g_ref[0, 0, rows, cols].astype(F32)
                                 + o_win[:, lanes] * g_ref[0, 2, rows, cols].astype(F32))

    def sweep_consume(step, src_ref, lanes, heads, n_keys=FAR):
        s = src_ref[:n_keys, lanes]
        m_old = m_ref[:, lanes]
        m_new = jnp.maximum(m_old, src_ref[FAR:FAR + 1, lanes])
        alpha = jnp.exp2(m_old - m_new)
        e = _exp2_bf16(s - m_new)
        for j, h in enumerate(heads):
            sub = slice(j * GL, (j + 1) * GL)
            v = _with_ones(vslc_ref[0, _kv_rows(h), step * FAR:step * FAR + n_keys])
            acc_ref[h] = alpha[:, sub] * acc_ref[h] + _dot(v, e[:, sub])
        m_ref[:, lanes] = m_new

    n_far_chunks = jnp.maximum(i - 1, 0)
    n_steps = (n_far_chunks + FAR // TQ - 1) // (FAR // TQ)
    max_steps = kslc_ref.shape[1] // FAR
    for step in range(max_steps):
        if step + 1 < max_steps:
            @pl.when(step + 1 < n_steps)
            def _(step=step):
                for h in range(N_KV_HEADS):
                    sweep_consume(step, sc_refs[step % 2], _head_lanes(h), (h,))
                    sweep_scores(step + 1, sc_refs[(step + 1) % 2], _pair_cols(h // 2), _head_lanes(h))

        for chunks in range(1, FAR // TQ + 1):
            @pl.when(n_far_chunks == step * (FAR // TQ) + chunks)
            def _(step=step, chunks=chunks):
                sweep_consume(step, sc_refs[step % 2], slice(0, wide), tuple(range(N_KV_HEADS)), chunks * TQ)

    for h in range(N_KV_HEADS):
        o_slc = acc_ref[h, :HEAD_DIM, :] / acc_ref[h, HEAD_DIM:HEAD_DIM + 1, :]
        for g in range(GROUP):
            rows = _head_rows(h, g)
            o = part_ref[rows, :] + o_slc[:, g * TQ:(g + 1) * TQ] * g_ref[0, 1, rows, cols].astype(F32)
            o_ref[0, rows, cols] = o.astype(BF16)


def _nsa_attention(qt, gt, cmp_nat, cmp_t, kslc, vslc, kwin, vwin, tabs, ovt):
    b, _, s = qt.shape
    nq = s // TQ
    assert s % FAR == 0
    ncp = s // CMP_STRIDE
    nsel = s // SEL_BLOCK
    near, wfar, t31, cmpb = tabs
    g4 = gt.reshape(b, N_BRANCH, HQ, s)
    assert nsel + 8 + NEAR // SEL_BLOCK <= PAIR
    cols = np.arange(PAIR)[None, :]
    key_blk = np.arange(s)[:, None] // SEL_BLOCK
    kaux_far = jnp.asarray(((cols == key_blk) | (cols == nsel) | (cols == nsel + 1)).astype(np.float32), BF16)
    near_blk = np.arange(NEAR)[:, None] // SEL_BLOCK
    kaux_near = jnp.asarray((cols == nsel + 8 + near_blk).astype(np.float32), BF16)
    full = lambda shape: pl.BlockSpec(shape, lambda b_, i: (0,) * len(shape), pipeline_mode=pl.Buffered(1))
    tw = NSA_TILES * TQ
    return pl.pallas_call(
        functools.partial(_nsa_kernel, nq), name="nsa_attention", grid=(b, s // tw),
        in_specs=[
            pl.BlockSpec((1, HQ, tw), lambda b_, i: (b_, 0, i)),
            pl.BlockSpec((1, N_BRANCH, HQ, tw), lambda b_, i: (b_, 0, 0, i)),
            pl.BlockSpec((1, 1, ncp, HKV), lambda b_, i: (0, b_, 0, 0)),
            pl.BlockSpec((1, 1, HKV, ncp), lambda b_, i: (1, b_, 0, 0)),
            pl.BlockSpec((1, s, HKV), lambda b_, i: (b_, 0, 0)),
            pl.BlockSpec((1, HKV, s), lambda b_, i: (b_, 0, 0)),
            pl.BlockSpec((1, s, HKV), lambda b_, i: (b_, 0, 0)),
            pl.BlockSpec((1, HKV, s), lambda b_, i: (b_, 0, 0)),
            full(cmpb.shape),
            full(near.shape), full(wfar.shape), full(t31.shape),
            full((nsel, ncp)),
            full((s, PAIR)), full((NEAR, PAIR)),
        ],
        out_specs=pl.BlockSpec((1, HQ, tw), lambda b_, i: (b_, 0, i)),
        out_shape=jax.ShapeDtypeStruct((b, HQ, s), BF16),
        scratch_shapes=[pltpu.VMEM((2 * PAIR, N_KV_HEADS * GL), BF16),
                        pltpu.VMEM((1, N_KV_HEADS * GL), F32),
                        pltpu.VMEM((N_KV_HEADS, HEAD_DIM + ONES_ROWS, GL), F32),
                        pltpu.VMEM((nsel, N_KV_HEADS * GL), F32),
                        pltpu.VMEM((HQ, TQ), F32),
                        pltpu.VMEM((FAR + 8, N_KV_HEADS * GL), F32), pltpu.VMEM((FAR + 8, N_KV_HEADS * GL), F32)],
        compiler_params=pltpu.CompilerParams(
            dimension_semantics=("parallel", "arbitrary"), vmem_limit_bytes=VMEM_LIMIT),
    )(qt, g4, cmp_nat, cmp_t, kslc, vslc, kwin, vwin, cmpb, near, wfar, t31, ovt, kaux_far, kaux_near)


def _compress_kernel(u_ref, ptop_ref, pbot_ref, wtop_ref, wbot_ref, w2_ref, kg_ref, nat_out, t_out):
    t = pl.program_id(0)
    nch = u_ref.shape[3] // CMP_STRIDE
    hidden = []
    for p in range(HKV // PAIR):
        u = jnp.concatenate([u_ref[0, 0, p, pl.ds(j, nch, stride=CMP_STRIDE), :] for j in range(CMP_STRIDE)], axis=1)
        top = (u + ptop_ref[0]).astype(BF16)
        bot = (u + pbot_ref[0]).astype(BF16)
        for e in range(2):
            bm = _dot(bot, wbot_ref[0, e])
            hidden.append(_dot(top, wtop_ref[0, e]) + pltpu.roll(bm, nch - 1, axis=0))
    hid = jnp.concatenate(hidden, axis=1)
    out_t = _dot(_silu(hid).astype(BF16), w2_ref[0]).T
    out_t = jnp.where(t == 0, _head_norm_t(out_t, kg_ref[...]), out_t)
    t_out[0, 0] = out_t.astype(BF16)
    nat_out[0, 0] = out_t.T.astype(BF16)


def _compress(craw, cmp_k_pos, cmp_k_w1, cmp_k_w2, cmp_v_pos, cmp_v_w1, cmp_v_w2, k_gain):
    b, _, n_pairs, s, _ = craw.shape
    nch = s // CMP_STRIDE
    wide = CMP_STRIDE * PAIR

    def padded_w1(w1_half):
        w = w1_half.reshape(CMP_STRIDE, HEAD_DIM, CMP_HIDDEN)
        z = jnp.zeros_like(w)
        both = jnp.stack([jnp.concatenate([w, z], axis=1), jnp.concatenate([z, w], axis=1)])
        return both.reshape(2, wide, CMP_HIDDEN).astype(BF16)

    def pos_row(p_half):
        return jnp.broadcast_to(p_half[:, None, :], (CMP_STRIDE, 2, HEAD_DIM)).reshape(1, wide)

    def big_w2(w2):
        eye = jnp.eye(N_KV_HEADS, dtype=F32)
        return jnp.einsum('nd,hk->hnkd', w2, eye).reshape(N_KV_HEADS * CMP_HIDDEN, HKV).astype(BF16)

    half = CMP_STRIDE * HEAD_DIM
    wtop = jnp.stack([padded_w1(cmp_k_w1[:half]), padded_w1(cmp_v_w1[:half])])
    wbot = jnp.stack([padded_w1(cmp_k_w1[half:]), padded_w1(cmp_v_w1[half:])])
    ptop = jnp.stack([pos_row(cmp_k_pos[:CMP_STRIDE]), pos_row(cmp_v_pos[:CMP_STRIDE])])
    pbot = jnp.stack([pos_row(cmp_k_pos[CMP_STRIDE:]), pos_row(cmp_v_pos[CMP_STRIDE:])])
    w2 = jnp.stack([big_w2(cmp_k_w2), big_w2(cmp_v_w2)])
    kg = jnp.broadcast_to(k_gain.astype(F32)[:, None], (HEAD_DIM, nch))
    sel = lambda shape: pl.BlockSpec((1,) + shape, lambda t, b_: (t,) + (0,) * len(shape))
    return pl.pallas_call(
        _compress_kernel, name="compress", grid=(2, b),
        in_specs=[
            pl.BlockSpec((1, 1, n_pairs, s, PAIR), lambda t, b_: (b_, t, 0, 0, 0)),
            sel((1, wide)), sel((1, wide)),
            sel((2, wide, CMP_HIDDEN)), sel((2, wide, CMP_HIDDEN)),
            sel((N_KV_HEADS * CMP_HIDDEN, HKV)),
            pl.BlockSpec((HEAD_DIM, nch), lambda t, b_: (0, 0)),
        ],
        out_specs=[pl.BlockSpec((1, 1, nch, HKV), lambda t, b_: (t, b_, 0, 0)),
                   pl.BlockSpec((1, 1, HKV, nch), lambda t, b_: (t, b_, 0, 0))],
        out_shape=[jax.ShapeDtypeStruct((2, b, nch, HKV), BF16),
                   jax.ShapeDtypeStruct((2, b, HKV, nch), BF16)],
        compiler_params=pltpu.CompilerParams(
            dimension_semantics=("arbitrary", "arbitrary"), vmem_limit_bytes=VMEM_LIMIT),
    )(craw, ptop, pbot, wtop, wbot, w2, kg)


def kernel(x, rel_table, a_norm, a_w_in, a_q_gain, a_k_gain, a_sink, a_w_out, kv_norm, kv_w,
           kv_k_gain, cmp_k_pos, cmp_k_w1, cmp_k_w2, cmp_v_pos, cmp_v_w1, cmp_v_w2,
           b_norm, b_w_in, b_q_gain, b_w_out):
    b, s, _ = x.shape
    nq = s // TQ
    n_a = a_w_in.shape[0]
    n_b = b_w_in.shape[0]
    swa_tab, near, wfar, t31, cmpb = _make_tables(rel_table.astype(F32) * LOG2E, nq)
    ovt = _overlap_t(s)

    a_shapes, a_specs = _a_outs(b, s)
    b_shapes, b_specs = _b_outs(b, s)
    xt_shape, xt_spec = a_shapes[0], a_specs[0]

    ws, gains = _a_weights(a_w_in[0], a_q_gain[0], a_k_gain[0])
    consts = [_bcast(a_norm[0])] + ws + gains
    xt, qt, k, vt, gt = _proj_call(
        _first_kernel, "proj_first", b, s, [x] + consts,
        [_nat_spec(D_MODEL)] + _specs_for(consts), a_shapes, a_specs)

    for layer in range(n_a):
        sink_row = jnp.broadcast_to(
            (a_sink[layer].astype(F32) * LOG2E).reshape(N_KV_HEADS, 1, GROUP, 1), (N_KV_HEADS, 1, GROUP, TQ)
        ).reshape(1, N_KV_HEADS * GL)
        ot = _swa_attention(qt, k, vt, gt, swa_tab, sink_row)
        wo = a_w_out[layer].T.astype(BF16)
        if layer + 1 < n_a:
            ws, gains = _a_weights(a_w_in[layer + 1], a_q_gain[layer + 1], a_k_gain[layer + 1])
            consts = [wo, _bcast(a_norm[layer + 1])] + ws + gains
            xt, qt, k, vt, gt = _proj_call(
                _a2a_kernel, "proj_a2a", b, s, [xt, ot] + consts,
                [_tok_spec(D_MODEL), _tok_spec(HQ)] + _specs_for(consts), a_shapes, a_specs)
        else:
            wsb, gb = _b_weights(b_w_in[0], b_q_gain[0])
            consts = ([wo, _bcast(kv_norm), kv_w.T.astype(BF16), _bcast(kv_k_gain[1]), _bcast(kv_k_gain[2]),
                       _bcast(b_norm[0])] + wsb + gb)
            kv_shapes = [jax.ShapeDtypeStruct((b, 2, HKV // PAIR, s, PAIR), F32),
                         jax.ShapeDtypeStruct((b, s, HKV), BF16), jax.ShapeDtypeStruct((b, HKV, s), BF16),
                         jax.ShapeDtypeStruct((b, s, HKV), BF16), jax.ShapeDtypeStruct((b, HKV, s), BF16)]
            kv_specs = [pl.BlockSpec((1, 2, HKV // PAIR, TM, PAIR), lambda b_, t: (b_, 0, 0, t, 0)),
                        _nat_spec(HKV), _tok_spec(HKV), _nat_spec(HKV), _tok_spec(HKV)]
            xt, craw, kslc, vslc, kwin, vwin, qt, gt = _proj_call(
                _a2b_kernel, "proj_a2b", b, s, [xt, ot] + consts,
                [_tok_spec(D_MODEL), _tok_spec(HQ)] + _specs_for(consts),
                [xt_shape] + kv_shapes + b_shapes, [xt_spec] + kv_specs + b_specs)

    cmp_nat, cmp_t = _compress(craw, cmp_k_pos, cmp_k_w1, cmp_k_w2, cmp_v_pos, cmp_v_w1, cmp_v_w2,
                               kv_k_gain[0])
    tabs = (near, wfar, t31, cmpb)
    for layer in range(n_b):
        ot = _nsa_attention(qt, gt, cmp_nat, cmp_t, kslc, vslc, kwin, vwin, tabs, ovt)
        wo = b_w_out[layer].T.astype(BF16)
        if layer + 1 < n_b:
            wsb, gb = _b_weights(b_w_in[layer + 1], b_q_gain[layer + 1])
            consts = [wo, _bcast(b_norm[layer + 1])] + wsb + gb
            xt, qt, gt = _proj_call(
                _b2b_kernel, "proj_b2b", b, s, [xt, ot] + consts,
                [_tok_spec(D_MODEL), _tok_spec(HQ)] + _specs_for(consts),
                [xt_shape] + b_shapes, [xt_spec] + b_specs)
        else:
            out = _proj_call(
                _final_kernel, "proj_final", b, s, [xt, ot, wo],
                [_tok_spec(D_MODEL), _tok_spec(HQ), _const_spec(wo.shape)],
                jax.ShapeDtypeStruct((b, s, D_MODEL), F32), _nat_spec(D_MODEL))
    return out
```

```python
import functools
import math

import numpy as np
import jax
import jax.numpy as jnp
from jax import lax
from jax.experimental import pallas as pl
from jax.experimental.pallas import tpu as pltpu

D_MODEL = 1024
HEAD_DIM = 64
N_HEADS = 16
N_KV_HEADS = 4
GROUP = N_HEADS // N_KV_HEADS
HQ = N_HEADS * HEAD_DIM
HKV = N_KV_HEADS * HEAD_DIM
N_BRANCH = 3
SWA_WINDOW = 128
NSA_WINDOW = 512
CMP_LEN = 32
CMP_STRIDE = 16
CMP_HIDDEN = 256
SEL_BLOCK = 64
SEL_TOP = 8
SEL_FORCE_LOCAL = 2
NUM_BUCKETS = 32
MAX_DISTANCE = 128
EPS = 1e-6
NEG = -1e30
FORCE_BONUS = 1e6

TQ = 128
TM = 512
TM_FINAL = 1024
SWA_TILES = 8
GL = GROUP * TQ
PAIR = 2 * HEAD_DIM
NEAR = 2 * TQ
WIN_FAR = NSA_WINDOW - TQ
N_WIN_VARIANTS = NSA_WINDOW // TQ + 1
FAR = 4 * TQ
ONES_ROWS = 16
LOG2E = math.log2(math.e)
Q_SCALE = HEAD_DIM ** -0.5 * LOG2E
VMEM_LIMIT = 56 * 1024 * 1024

F32 = jnp.float32
BF16 = jnp.bfloat16


def _bucket_np(dist):
    d = np.maximum(dist, 0)
    max_exact = NUM_BUCKETS // 2
    ratio = (np.log(np.maximum(d, 1).astype(np.float32) / np.float32(max_exact))
             / np.float32(math.log(MAX_DISTANCE / max_exact))
             * np.float32(NUM_BUCKETS - max_exact))
    large = np.minimum(max_exact + ratio.astype(np.int32), NUM_BUCKETS - 1)
    return np.where(d < max_exact, d, large).astype(np.int32)


def _dist_vector(table, d_lo, length, hi_valid):
    d = d_lo + np.arange(length)
    v = table.T[:, _bucket_np(d)]
    return jnp.where(jnp.asarray((d >= 0) & (d < hi_valid))[None], v, NEG)


def _toeplitz_kernel(specs, *refs):
    n = len(specs)
    for (rows, step), v_ref, o_ref in zip(specs, refs[:n], refs[n:]):
        length = v_ref.shape[2]
        x = jnp.broadcast_to(v_ref[0], (rows, length))
        shift = (length - step * (rows - 1)) % length
        o_ref[...] = pltpu.roll(x, shift, 1, stride=step, stride_axis=0)[:, :TQ]


def _toeplitz_tables(table, requests):
    specs, vecs = [], []
    for n_rows, d_first, step, hi_valid in requests:
        rows = -(-n_rows // 8) * 8
        length = -(-(step * (rows - 1) + TQ) // TQ) * TQ
        d_lo = d_first - step * (rows - 1)
        vecs.append(_dist_vector(table, d_lo, length, hi_valid)[:, None, :])
        specs.append((rows, step))
    outs = pl.pallas_call(
        functools.partial(_toeplitz_kernel, tuple(specs)), name="bias_tables", grid=(N_HEADS,),
        in_specs=[pl.BlockSpec((1, 1, v.shape[2]), lambda h: (h, 0, 0)) for v in vecs],
        out_specs=[pl.BlockSpec((rows, TQ), lambda h: (0, h)) for rows, _ in specs],
        out_shape=[jax.ShapeDtypeStruct((rows, N_HEADS * TQ), F32) for rows, _ in specs],
    )(*vecs)
    return [o[:req[0]] for o, req in zip(outs, requests)]


def _mask_rows(t, n_masked):
    rows = np.arange(t.shape[0])[:, None] < n_masked
    return jnp.where(jnp.asarray(rows), NEG, t)


def _make_tables(table, nq):
    big = 1 << 30
    per = TQ // CMP_STRIDE
    ncp = nq * per
    off = per * (nq - 1)
    sat_c = -(-(MAX_DISTANCE + CMP_LEN - 1) // CMP_STRIDE)
    assert off >= sat_c
    wide = N_HEADS * TQ
    near, wcut, band = _toeplitz_tables(table, [
        (NEAR, TQ, 1, big),
        (TQ, NSA_WINDOW, 1, NSA_WINDOW),
        (sat_c + per - 1, CMP_STRIDE * (sat_c - 1) - (CMP_LEN - 1), CMP_STRIDE, big)])
    in_window = (TQ - np.arange(NEAR)[:, None] + np.arange(wide)[None, :] % TQ) < SWA_WINDOW
    swa = jnp.where(jnp.asarray(in_window), near, NEG)
    near = jnp.stack([_mask_rows(near, TQ), near])
    swa = jnp.stack([_mask_rows(swa, TQ), swa])
    t31 = jnp.broadcast_to(table[NUM_BUCKETS - 1][:, None], (N_HEADS, TQ)).reshape(1, wide)

    def saturated(rows):
        return jnp.broadcast_to(t31, (rows, wide))

    wfar = jnp.concatenate([wcut, saturated(WIN_FAR - TQ)], axis=0)
    wfar = jnp.stack([_mask_rows(wfar, min(WIN_FAR, (N_WIN_VARIANTS - 1 - v) * TQ))
                      for v in range(N_WIN_VARIANTS)])
    cmpb = jnp.concatenate([saturated(off - sat_c + 1), band, jnp.full((ncp - per, wide), NEG, F32)], axis=0)
    return swa, near, wfar, t31, cmpb


def _overlap_t(s):
    ncp = s // CMP_STRIDE
    nsel = s // SEL_BLOCK
    cs = np.arange(ncp)[None, :] * CMP_STRIDE
    ss = np.arange(nsel)[:, None] * SEL_BLOCK
    ov = (cs < ss + SEL_BLOCK) & (cs + CMP_LEN > ss) & (np.arange(ncp)[None, :] < ncp - 1)
    return jnp.asarray(ov.astype(np.float32), dtype=BF16)


def _lanes(gain_b, n):
    return jnp.concatenate([gain_b] * (n // gain_b.shape[1]), axis=1)


def _rms_t(xt, gain_b):
    ms = jnp.mean(xt * xt, axis=0, keepdims=True)
    return (xt * lax.rsqrt(ms + EPS) * _lanes(gain_b, xt.shape[1])).astype(BF16)


def _head_norm_t(a, gain_b):
    outs = []
    for h in range(a.shape[0] // HEAD_DIM):
        blk = a[h * HEAD_DIM:(h + 1) * HEAD_DIM]
        ms = jnp.mean(blk * blk, axis=0, keepdims=True)
        outs.append(blk * lax.rsqrt(ms + EPS) * _lanes(gain_b, a.shape[1]))
    return jnp.concatenate(outs, axis=0)


def _sigmoid_h(hz):
    return 0.5 + 0.5 * jnp.tanh(hz)


def _silu_h(hz):
    return hz + hz * jnp.tanh(hz)


def _silu(z):
    return _silu_h(0.5 * z)


def _dot(a, b):
    return jnp.dot(a, b, preferred_element_type=F32)


def _proj_a(xn, w, qg, kg, q_out, k_out, v_out, g_out):
    pq = _dot(w[:HQ, :], xn)
    pkv = _dot(w[HQ:HQ + 2 * HKV, :], xn)
    q_out[0] = _head_norm_t(pq, qg[...]).astype(BF16)
    pz = _dot(w[HQ + 2 * HKV:, :], xn)
    k_out[0] = _head_norm_t(pkv[:HKV], kg[...]).T.astype(BF16)
    v_out[0] = pkv[HKV:].astype(BF16)
    g_out[0] = _silu_h(pz).astype(BF16)


def _proj_b(xn, w, qg, q_out, g_out):
    n_gate = N_BRANCH * N_HEADS
    pq = _dot(w[:HQ, :], xn)
    sg = _sigmoid_h(_dot(w[HQ:HQ + n_gate, :], xn))

    def gate_rows(pz, c):
        for hd in range(N_HEADS):
            r = c * N_HEADS + hd
            z = pz[hd * HEAD_DIM:(hd + 1) * HEAD_DIM]
            g_out[0, r * HEAD_DIM:(r + 1) * HEAD_DIM, :] = (_silu_h(z) * sg[r:r + 1]).astype(BF16)

    z0 = HQ + n_gate
    pz = _dot(w[z0:z0 + HQ, :], xn)
    q_out[0] = _head_norm_t(pq, qg[...]).astype(BF16)
    for c in range(1, N_BRANCH):
        nxt = _dot(w[z0 + c * HQ:z0 + (c + 1) * HQ, :], xn)
        gate_rows(pz, c - 1)
        pz = nxt
    gate_rows(pz, N_BRANCH - 1)


def _proj_kv(xn, wkv, kg1, kg2, craw_out, kslc_out, vslc_out, kwin_out, vwin_out):
    p = _dot(wkv[...], xn)

    def rows(n):
        return p[n * HKV:(n + 1) * HKV]
    for t in range(2):
        nat = rows(t).T
        for pr in range(HKV // PAIR):
            craw_out[0, t, pr] = nat[:, _pair_cols(pr)]
    kslc_out[0] = _head_norm_t(rows(2), kg1[...]).T.astype(BF16)
    vslc_out[0] = rows(3).astype(BF16)
    kwin_out[0] = _head_norm_t(rows(4), kg2[...]).T.astype(BF16)
    vwin_out[0] = rows(5).astype(BF16)


def _first_kernel(x_ref, ng, w, qg, kg, xt_out, q_out, k_out, v_out, g_out):
    sub = TM // 2
    for j in range(TM // sub):
        c = slice(j * sub, (j + 1) * sub)
        xt = x_ref[0, c, :].T
        xt_out[0, :, c] = xt
        _proj_a(_rms_t(xt, ng[...]), w, qg, kg, q_out.at[:, :, c], k_out.at[:, c, :], v_out.at[:, :, c],
                g_out.at[:, :, c])


def _a2a_kernel(x_ref, o_ref, wo, ng, w, qg, kg, xt_out, q_out, k_out, v_out, g_out):
    xt = x_ref[0] + _dot(wo[...], o_ref[0])
    xt_out[0] = xt
    _proj_a(_rms_t(xt, ng[...]), w, qg, kg, q_out, k_out, v_out, g_out)


def _a2b_kernel(x_ref, o_ref, wo, ngkv, wkv, kg1, kg2, ngb, w, qg,
                xt_out, craw_out, kslc_out, vslc_out, kwin_out, vwin_out, q_out, g_out):
    xt = x_ref[0] + _dot(wo[...], o_ref[0])
    xt_out[0] = xt
    _proj_kv(_rms_t(xt, ngkv[...]), wkv, kg1, kg2, craw_out, kslc_out, vslc_out, kwin_out, vwin_out)
    _proj_b(_rms_t(xt, ngb[...]), w, qg, q_out, g_out)


def _b2b_kernel(x_ref, o_ref, wo, ngb, w, qg, xt_out, q_out, g_out):
    xt = x_ref[0] + _dot(wo[...], o_ref[0])
    xt_out[0] = xt
    _proj_b(_rms_t(xt, ngb[...]), w, qg, q_out, g_out)


def _final_kernel(x_ref, o_ref, wo, x_out):
    x_out[0] = (x_ref[0] + _dot(wo[...], o_ref[0])).T


def _tok_spec(rows, tm=TM):
    return pl.BlockSpec((1, rows, tm), lambda b, t: (b, 0, t))


def _nat_spec(cols, tm=TM):
    return pl.BlockSpec((1, tm, cols), lambda b, t: (b, t, 0))


def _const_spec(shape):
    nd = len(shape)
    return pl.BlockSpec(shape, lambda b, t: (0,) * nd, pipeline_mode=pl.Buffered(1))


def _proj_call(body, name, b, s, ins, in_specs, out_shapes, out_specs, tm=TM):
    return pl.pallas_call(
        body, name=name, grid=(b, s // tm),
        in_specs=in_specs, out_specs=out_specs, out_shape=out_shapes,
        compiler_params=pltpu.CompilerParams(
            dimension_semantics=("parallel", "parallel"), vmem_limit_bytes=VMEM_LIMIT),
    )(*ins)


def _bcast(v, scale=1.0):
    return jnp.broadcast_to((v.astype(F32) * scale)[:, None], (v.shape[0], TQ))


def _a_weights(w_in, q_gain, k_gain):
    halve = np.where(np.arange(w_in.shape[1]) >= HQ + 2 * HKV, 0.5, 1.0).astype(np.float32)
    ws = [(w_in * halve).T.astype(BF16)]
    gains = [_bcast(q_gain, Q_SCALE), _bcast(k_gain)]
    return ws, gains


def _b_weights(w_in, q_gain):
    halve = np.where(np.arange(w_in.shape[1]) >= HQ, 0.5, 1.0).astype(np.float32)
    ws = [(w_in * halve).T.astype(BF16)]
    return ws, [_bcast(q_gain, Q_SCALE)]


def _a_outs(b, s):
    shapes = [jax.ShapeDtypeStruct((b, D_MODEL, s), F32), jax.ShapeDtypeStruct((b, HQ, s), BF16),
              jax.ShapeDtypeStruct((b, s, HKV), BF16), jax.ShapeDtypeStruct((b, HKV, s), BF16),
              jax.ShapeDtypeStruct((b, HQ, s), BF16)]
    specs = [_tok_spec(D_MODEL), _tok_spec(HQ), _nat_spec(HKV), _tok_spec(HKV), _tok_spec(HQ)]
    return shapes, specs


def _b_outs(b, s):
    shapes = [jax.ShapeDtypeStruct((b, HQ, s), BF16), jax.ShapeDtypeStruct((b, N_BRANCH * HQ, s), BF16)]
    specs = [_tok_spec(HQ), _tok_spec(N_BRANCH * HQ)]
    return shapes, specs


def _specs_for(arrs):
    return [_const_spec(a.shape) for a in arrs]


def _head_rows(h, g):
    return slice((h * GROUP + g) * HEAD_DIM, (h * GROUP + g + 1) * HEAD_DIM)


def _kv_rows(h):
    return slice(h * HEAD_DIM, (h + 1) * HEAD_DIM)


def _head_lanes(h):
    return slice(h * GL, (h + 1) * GL)


def _pair_lanes(p):
    return slice(2 * p * GL, 2 * (p + 1) * GL)


def _pair_cols(p):
    return slice(p * PAIR, (p + 1) * PAIR)


def _padded_q(q_ref, h, cols=slice(None)):
    qs = jnp.concatenate([q_ref[0, _head_rows(h, g), cols] for g in range(GROUP)], axis=1)
    zeros = jnp.zeros_like(qs)
    return jnp.concatenate([qs, zeros] if h % 2 == 0 else [zeros, qs], axis=0)


def _with_ones(v):
    return jnp.concatenate([v, jnp.ones((ONES_ROWS, v.shape[1]), BF16)], axis=0)


def _exp2_bf16(x):
    return jnp.exp2(x.astype(BF16))


def _swa_kernel(q_ref, kp_ref, kc_ref, vp_ref, vc_ref, g_ref, bias_ref, sink_ref, o_ref):
    i = pl.program_id(1)
    sink = sink_ref[...]
    for j in range(SWA_TILES):
        cur = slice(j * TQ, (j + 1) * TQ)
        prev = slice((j - 1) * TQ, j * TQ)
        k_prev = (lambda c: kp_ref[0, :, c]) if j == 0 else (lambda c: kc_ref[0, prev, c])
        v_prev = (lambda r: vp_ref[0, r, :]) if j == 0 else (lambda r: vc_ref[0, r, prev])
        bias = bias_ref[jnp.minimum(i, 1)] if j == 0 else bias_ref[1]
        pairs = []
        for p in range(N_KV_HEADS // 2):
            k = jnp.concatenate([k_prev(_pair_cols(p)), kc_ref[0, cur, _pair_cols(p)]], axis=0)
            qp = jnp.concatenate([_padded_q(q_ref, 2 * p, cur), _padded_q(q_ref, 2 * p + 1, cur)], axis=1)
            pairs.append(_dot(k, qp))
        s = jnp.concatenate(pairs, axis=1) + bias
        m = jnp.maximum(jnp.max(s, axis=0, keepdims=True), sink)
        e = _exp2_bf16(s - m)
        e_sink = jnp.exp2(sink - m)
        for h in range(N_KV_HEADS):
            v = jnp.concatenate([v_prev(_kv_rows(h)), vc_ref[0, _kv_rows(h), cur]], axis=1)
            acc = _dot(_with_ones(v), e[:, _head_lanes(h)])
            o = acc[:HEAD_DIM] / (acc[HEAD_DIM:HEAD_DIM + 1] + e_sink[:, _head_lanes(h)])
            for g in range(GROUP):
                rows = _head_rows(h, g)
                o_ref[0, rows, cur] = (o[:, g * TQ:(g + 1) * TQ] * g_ref[0, rows, cur].astype(F32)).astype(BF16)


def _swa_attention(qt, k, vt, gt, swa_tab, sink_row):
    b, _, s = qt.shape
    tw = SWA_TILES * TQ
    prev = lambda i: jnp.maximum(SWA_TILES * i - 1, 0)
    return pl.pallas_call(
        _swa_kernel, name="swa_attention", grid=(b, s // tw),
        in_specs=[
            pl.BlockSpec((1, HQ, tw), lambda b_, i: (b_, 0, i)),
            pl.BlockSpec((1, TQ, HKV), lambda b_, i: (b_, prev(i), 0)),
            pl.BlockSpec((1, tw, HKV), lambda b_, i: (b_, i, 0)),
            pl.BlockSpec((1, HKV, TQ), lambda b_, i: (b_, 0, prev(i))),
            pl.BlockSpec((1, HKV, tw), lambda b_, i: (b_, 0, i)),
            pl.BlockSpec((1, HQ, tw), lambda b_, i: (b_, 0, i)),
            pl.BlockSpec(swa_tab.shape, lambda b_, i: (0, 0, 0)),
            pl.BlockSpec((1, N_KV_HEADS * GL), lambda b_, i: (0, 0)),
        ],
        out_specs=pl.BlockSpec((1, HQ, tw), lambda b_, i: (b_, 0, i)),
        out_shape=jax.ShapeDtypeStruct((b, HQ, s), BF16),
        compiler_params=pltpu.CompilerParams(
            dimension_semantics=("parallel", "arbitrary"), vmem_limit_bytes=VMEM_LIMIT),
    )(qt, k, k, vt, vt, gt, swa_tab, sink_row)


def _nsa_kernel(nq, q_ref, g_ref, kcmp_ref, vcmp_ref, kslc_ref, vslc_ref, kwin_ref, vwin_ref,
                cmpb_ref, near_ref, wfar_ref, t31_ref, ovt_ref, kauxf_ref, kauxn_ref, o_ref,
                qp_ref, m_ref, acc_ref, selb_ref, part_ref, sc0_ref, sc1_ref):
    i = pl.program_id(1)
    ncp = kcmp_ref.shape[2]
    nsel = ovt_ref.shape[0]
    n_pairs = N_KV_HEADS // 2
    wide = N_KV_HEADS * GL
    cmp_start = pl.multiple_of((TQ // CMP_STRIDE) * (nq - 1 - i), 8)
    prev_rows = pl.ds(pl.multiple_of(jnp.maximum(i - 1, 0) * TQ, TQ), TQ)
    cur_rows = pl.ds(pl.multiple_of(i * TQ, TQ), TQ)
    far_starts = [pl.multiple_of(jnp.maximum(i - back, 0) * TQ, TQ) for back in range(NSA_WINDOW // TQ, 1, -1)]
    n_far_blocks = (TQ // SEL_BLOCK) * (i - 1)

    def near_keys(k_ref, p):
        return jnp.concatenate([k_ref[0, prev_rows, _pair_cols(p)], k_ref[0, cur_rows, _pair_cols(p)]], axis=0)

    def near_values(v_ref, h):
        return jnp.concatenate([v_ref[0, _kv_rows(h), prev_rows], v_ref[0, _kv_rows(h), cur_rows]], axis=1)

    def scores(lhs_of_pair, rows):
        return jnp.concatenate([_dot(lhs_of_pair(p), qp_ref[rows, _pair_lanes(p)]) for p in range(n_pairs)], axis=1)

    for h in range(N_KV_HEADS):
        qp_ref[:PAIR, _head_lanes(h)] = _padded_q(q_ref, h)
    q_rows = slice(0, PAIR)
    near_tab = near_ref[0]
    bc = cmpb_ref[pl.ds(cmp_start, ncp), :]
    sc = scores(lambda p: kcmp_ref[0, 0, :, _pair_cols(p)], q_rows) + bc
    mc = jnp.max(sc, axis=0, keepdims=True)
    s_near = scores(lambda p: near_keys(kwin_ref, p), q_rows) + near_tab
    s_far = scores(lambda p: jnp.concatenate([kwin_ref[0, pl.ds(st, TQ), _pair_cols(p)] for st in far_starts], axis=0),
                   q_rows) + wfar_ref[0]
    mw = jnp.maximum(jnp.max(s_near, axis=0, keepdims=True), jnp.max(s_far, axis=0, keepdims=True))

    valid = bc > 0.5 * NEG
    ec = jnp.where(valid, jnp.exp2(sc - mc), 0.0)
    lc = jnp.sum(ec, axis=0, keepdims=True)
    pc = ec / jnp.where(lc > 0.0, lc, 1.0)
    pc_b = pc.astype(BF16)
    o_cmp = [_dot(vcmp_ref[0, 0, _kv_rows(h), :], pc_b[:, _head_lanes(h)]) for h in range(N_KV_HEADS)]

    psum = jnp.concatenate(
        [sum(pc[:, h * GL + g * TQ:h * GL + (g + 1) * TQ] for g in range(GROUP)) for h in range(N_KV_HEADS)], axis=1)
    ovt = ovt_ref[...]
    p1 = psum.astype(BF16)
    r1 = psum - p1.astype(F32)
    p2 = r1.astype(BF16)
    p3 = (r1 - p2.astype(F32)).astype(BF16)
    imp = _dot(ovt, p1) + _dot(ovt, p2) + _dot(ovt, p3)
    lane = lax.broadcasted_iota(jnp.int32, imp.shape, 1)
    pos = i * TQ + lane % TQ
    blk = lax.broadcasted_iota(jnp.int32, imp.shape, 0)
    causal = blk * SEL_BLOCK <= pos
    rel = pos // SEL_BLOCK - blk
    forced = (blk == 0) | ((rel >= 0) & (rel < SEL_FORCE_LOCAL))
    score = jnp.where(causal, imp + jnp.where(forced, FORCE_BONUS, 0.0), NEG)
    groups = [score[r:r + 8] for r in range(0, nsel, 8)]
    counts = [jnp.zeros((8, imp.shape[1]), jnp.int32) for _ in groups]
    row_in_group = lax.broadcasted_iota(jnp.int32, (8, imp.shape[1]), 0)
    for jp in range(nsel):
        row = score[jp:jp + 1, :]
        for gi, grp in enumerate(groups):
            if gi * 8 > jp:
                beats = row >= grp
            elif gi * 8 + 7 < jp:
                beats = row > grp
            else:
                beats = (row > grp) | ((row == grp) & (row_in_group > jp - gi * 8))
            counts[gi] = counts[gi] + beats.astype(jnp.int32)
    sel = jnp.concatenate(counts, axis=0) < min(SEL_TOP, nsel)

    def per_group(x):
        return jnp.concatenate([x[:, h * TQ:(h + 1) * TQ] for h in range(N_KV_HEADS) for _ in range(GROUP)], axis=1)

    selb_ref[...] = per_group(jnp.where(sel, 0.0, NEG))
    sel_far = per_group(jnp.where(sel & (blk < n_far_blocks), 0.0, NEG))
    t31 = t31_ref[...]
    hi = t31.astype(BF16).astype(F32)
    row8 = lax.broadcasted_iota(jnp.int32, (8, wide), 0)
    const_rows = jnp.where(row8 == 0, hi, jnp.where(row8 == 1, t31 - hi, 0.0))
    first_blk = (TQ // SEL_BLOCK) * (i - 1)
    near_rows = [selb_ref[pl.ds(jnp.maximum(first_blk + u, 0), 1), :] for u in range(NEAR // SEL_BLOCK)]
    aux = jnp.concatenate(
        [sel_far, const_rows] + near_rows + [jnp.zeros((PAIR - nsel - 8 - NEAR // SEL_BLOCK, wide), F32)], axis=0)
    qp_ref[PAIR:, :] = aux.astype(BF16)

    all_rows = slice(0, 2 * PAIR)
    s = scores(lambda p: jnp.concatenate([near_keys(kslc_ref, p), kauxn_ref[...]], axis=1), all_rows) + near_tab
    m = jnp.max(s, axis=0, keepdims=True)

    sc_refs = (sc0_ref, sc1_ref)

    def sweep_scores(step, dst_ref, cols, lanes):
        keys = slice(step * FAR, (step + 1) * FAR)
        lhs = jnp.concatenate([kslc_ref[0, keys, cols], kauxf_ref[keys, :]], axis=1)
        sc_new = _dot(lhs, qp_ref[:, lanes])
        dst_ref[:FAR, lanes] = sc_new
        dst_ref[FAR:FAR + 1, lanes] = jnp.max(sc_new, axis=0, keepdims=True)

    for p in range(n_pairs):
        sweep_scores(0, sc_refs[0], _pair_cols(p), _pair_lanes(p))

    m_ref[...] = m
    e = _exp2_bf16(s - m)
    for h in range(N_KV_HEADS):
        acc_ref[h] = _dot(_with_ones(near_values(vslc_ref, h)), e[:, _head_lanes(h)])

    e_near = _exp2_bf16(s_near - mw)
    e_far = _exp2_bf16(s_far - mw)
    for h in range(N_KV_HEADS):
        far_v = jnp.concatenate([vwin_ref[0, _kv_rows(h), pl.ds(st, TQ)] for st in far_starts], axis=1)
        acc_w = (_dot(_with_ones(near_values(vwin_ref, h)), e_near[:, _head_lanes(h)])
                 + _dot(_with_ones(far_v), e_far[:, _head_lanes(h)]))
        o_win = acc_w[:HEAD_DIM] / acc_w[HEAD_DIM:HEAD_DIM + 1]
        for g in range(GROUP):
            rows = _head_rows(h, g)
            lanes = slice(g * TQ, (g + 1) * TQ)
            part_ref[rows, :] = (o_cmp[h][:, lanes] * g_ref[0, 0, rows, :].astype(F32)
                                 + o_win[:, lanes] * g_ref[0, 2, rows, :].astype(F32))

    def sweep_consume(step, src_ref, lanes, heads, n_keys=FAR):
        s = src_ref[:n_keys, lanes]
        m_old = m_ref[:, lanes]
        m_new = jnp.maximum(m_old, src_ref[FAR:FAR + 1, lanes])
        alpha = jnp.exp2(m_old - m_new)
        e = _exp2_bf16(s - m_new)
        for j, h in enumerate(heads):
            sub = slice(j * GL, (j + 1) * GL)
            v = _with_ones(vslc_ref[0, _kv_rows(h), step * FAR:step * FAR + n_keys])
            acc_ref[h] = alpha[:, sub] * acc_ref[h] + _dot(v, e[:, sub])
        m_ref[:, lanes] = m_new

    n_far_chunks = jnp.maximum(i - 1, 0)
    n_steps = (n_far_chunks + FAR // TQ - 1) // (FAR // TQ)
    max_steps = kslc_ref.shape[1] // FAR
    for step in range(max_steps):
        if step + 1 < max_steps:
            @pl.when(step + 1 < n_steps)
            def _(step=step):
                for h in range(N_KV_HEADS):
                    sweep_consume(step, sc_refs[step % 2], _head_lanes(h), (h,))
                    sweep_scores(step + 1, sc_refs[(step + 1) % 2], _pair_cols(h // 2), _head_lanes(h))

        for chunks in range(1, FAR // TQ + 1):
            @pl.when(n_far_chunks == step * (FAR // TQ) + chunks)
            def _(step=step, chunks=chunks):
                sweep_consume(step, sc_refs[step % 2], slice(0, wide), tuple(range(N_KV_HEADS)), chunks * TQ)

    for h in range(N_KV_HEADS):
        o_slc = acc_ref[h, :HEAD_DIM, :] / acc_ref[h, HEAD_DIM:HEAD_DIM + 1, :]
        for g in range(GROUP):
            rows = _head_rows(h, g)
            o = part_ref[rows, :] + o_slc[:, g * TQ:(g + 1) * TQ] * g_ref[0, 1, rows, :].astype(F32)
            o_ref[0, rows, :] = o.astype(BF16)


def _nsa_attention(qt, gt, cmp_nat, cmp_t, kslc, vslc, kwin, vwin, tabs, ovt):
    b, _, s = qt.shape
    nq = s // TQ
    assert s % FAR == 0
    ncp = s // CMP_STRIDE
    nsel = s // SEL_BLOCK
    near, wfar, t31, cmpb = tabs
    g4 = gt.reshape(b, N_BRANCH, HQ, s)
    assert nsel + 8 + NEAR // SEL_BLOCK <= PAIR
    cols = np.arange(PAIR)[None, :]
    key_blk = np.arange(s)[:, None] // SEL_BLOCK
    kaux_far = jnp.asarray(((cols == key_blk) | (cols == nsel) | (cols == nsel + 1)).astype(np.float32), BF16)
    near_blk = np.arange(NEAR)[:, None] // SEL_BLOCK
    kaux_near = jnp.asarray((cols == nsel + 8 + near_blk).astype(np.float32), BF16)
    full = lambda shape: pl.BlockSpec(shape, lambda b_, i: (0,) * len(shape))
    return pl.pallas_call(
        functools.partial(_nsa_kernel, nq), name="nsa_attention", grid=(b, nq),
        in_specs=[
            pl.BlockSpec((1, HQ, TQ), lambda b_, i: (b_, 0, i)),
            pl.BlockSpec((1, N_BRANCH, HQ, TQ), lambda b_, i: (b_, 0, 0, i)),
            pl.BlockSpec((1, 1, ncp, HKV), lambda b_, i: (0, b_, 0, 0)),
            pl.BlockSpec((1, 1, HKV, ncp), lambda b_, i: (1, b_, 0, 0)),
            pl.BlockSpec((1, s, HKV), lambda b_, i: (b_, 0, 0)),
            pl.BlockSpec((1, HKV, s), lambda b_, i: (b_, 0, 0)),
            pl.BlockSpec((1, s, HKV), lambda b_, i: (b_, 0, 0)),
            pl.BlockSpec((1, HKV, s), lambda b_, i: (b_, 0, 0)),
            full(cmpb.shape),
            pl.BlockSpec((1, NEAR, N_KV_HEADS * GL), lambda b_, i: (jnp.minimum(i, 1), 0, 0)),
            pl.BlockSpec((1, WIN_FAR, N_KV_HEADS * GL),
                         lambda b_, i: (jnp.minimum(i, N_WIN_VARIANTS - 1), 0, 0)),
            full(t31.shape),
            full((nsel, ncp)),
            full((s, PAIR)), full((NEAR, PAIR)),
        ],
        out_specs=pl.BlockSpec((1, HQ, TQ), lambda b_, i: (b_, 0, i)),
        out_shape=jax.ShapeDtypeStruct((b, HQ, s), BF16),
        scratch_shapes=[pltpu.VMEM((2 * PAIR, N_KV_HEADS * GL), BF16),
                        pltpu.VMEM((1, N_KV_HEADS * GL), F32),
                        pltpu.VMEM((N_KV_HEADS, HEAD_DIM + ONES_ROWS, GL), F32),
                        pltpu.VMEM((nsel, N_KV_HEADS * GL), F32),
                        pltpu.VMEM((HQ, TQ), F32),
                        pltpu.VMEM((FAR + 8, N_KV_HEADS * GL), F32), pltpu.VMEM((FAR + 8, N_KV_HEADS * GL), F32)],
        compiler_params=pltpu.CompilerParams(
            dimension_semantics=("parallel", "arbitrary"), vmem_limit_bytes=VMEM_LIMIT),
    )(qt, g4, cmp_nat, cmp_t, kslc, vslc, kwin, vwin, cmpb, near, wfar, t31, ovt, kaux_far, kaux_near)


def _compress_kernel(u_ref, ptop_ref, pbot_ref, wtop_ref, wbot_ref, w2_ref, kg_ref, nat_out, t_out):
    t = pl.program_id(0)
    nch = u_ref.shape[3] // CMP_STRIDE
    hidden = []
    for p in range(HKV // PAIR):
        u = jnp.concatenate([u_ref[0, 0, p, pl.ds(j, nch, stride=CMP_STRIDE), :] for j in range(CMP_STRIDE)], axis=1)
        top = (u + ptop_ref[0]).astype(BF16)
        bot = (u + pbot_ref[0]).astype(BF16)
        for e in range(2):
            bm = _dot(bot, wbot_ref[0, e])
            hidden.append(_dot(top, wtop_ref[0, e]) + pltpu.roll(bm, nch - 1, axis=0))
    hid = jnp.concatenate(hidden, axis=1)
    out_t = _dot(_silu(hid).astype(BF16), w2_ref[0]).T
    out_t = jnp.where(t == 0, _head_norm_t(out_t, kg_ref[...]), out_t)
    t_out[0, 0] = out_t.astype(BF16)
    nat_out[0, 0] = out_t.T.astype(BF16)


def _compress(craw, cmp_k_pos, cmp_k_w1, cmp_k_w2, cmp_v_pos, cmp_v_w1, cmp_v_w2, k_gain):
    b, _, n_pairs, s, _ = craw.shape
    nch = s // CMP_STRIDE
    wide = CMP_STRIDE * PAIR

    def padded_w1(w1_half):
        w = w1_half.reshape(CMP_STRIDE, HEAD_DIM, CMP_HIDDEN)
        z = jnp.zeros_like(w)
        both = jnp.stack([jnp.concatenate([w, z], axis=1), jnp.concatenate([z, w], axis=1)])
        return both.reshape(2, wide, CMP_HIDDEN).astype(BF16)

    def pos_row(p_half):
        return jnp.broadcast_to(p_half[:, None, :], (CMP_STRIDE, 2, HEAD_DIM)).reshape(1, wide)

    def big_w2(w2):
        eye = jnp.eye(N_KV_HEADS, dtype=F32)
        return jnp.einsum('nd,hk->hnkd', w2, eye).reshape(N_KV_HEADS * CMP_HIDDEN, HKV).astype(BF16)

    half = CMP_STRIDE * HEAD_DIM
    wtop = jnp.stack([padded_w1(cmp_k_w1[:half]), padded_w1(cmp_v_w1[:half])])
    wbot = jnp.stack([padded_w1(cmp_k_w1[half:]), padded_w1(cmp_v_w1[half:])])
    ptop = jnp.stack([pos_row(cmp_k_pos[:CMP_STRIDE]), pos_row(cmp_v_pos[:CMP_STRIDE])])
    pbot = jnp.stack([pos_row(cmp_k_pos[CMP_STRIDE:]), pos_row(cmp_v_pos[CMP_STRIDE:])])
    w2 = jnp.stack([big_w2(cmp_k_w2), big_w2(cmp_v_w2)])
    kg = jnp.broadcast_to(k_gain.astype(F32)[:, None], (HEAD_DIM, nch))
    sel = lambda shape: pl.BlockSpec((1,) + shape, lambda t, b_: (t,) + (0,) * len(shape))
    return pl.pallas_call(
        _compress_kernel, name="compress", grid=(2, b),
        in_specs=[
            pl.BlockSpec((1, 1, n_pairs, s, PAIR), lambda t, b_: (b_, t, 0, 0, 0)),
            sel((1, wide)), sel((1, wide)),
            sel((2, wide, CMP_HIDDEN)), sel((2, wide, CMP_HIDDEN)),
            sel((N_KV_HEADS * CMP_HIDDEN, HKV)),
            pl.BlockSpec((HEAD_DIM, nch), lambda t, b_: (0, 0)),
        ],
        out_specs=[pl.BlockSpec((1, 1, nch, HKV), lambda t, b_: (t, b_, 0, 0)),
                   pl.BlockSpec((1, 1, HKV, nch), lambda t, b_: (t, b_, 0, 0))],
        out_shape=[jax.ShapeDtypeStruct((2, b, nch, HKV), BF16),
                   jax.ShapeDtypeStruct((2, b, HKV, nch), BF16)],
        compiler_params=pltpu.CompilerParams(
            dimension_semantics=("arbitrary", "arbitrary"), vmem_limit_bytes=VMEM_LIMIT),
    )(craw, ptop, pbot, wtop, wbot, w2, kg)


def kernel(x, rel_table, a_norm, a_w_in, a_q_gain, a_k_gain, a_sink, a_w_out, kv_norm, kv_w,
           kv_k_gain, cmp_k_pos, cmp_k_w1, cmp_k_w2, cmp_v_pos, cmp_v_w1, cmp_v_w2,
           b_norm, b_w_in, b_q_gain, b_w_out):
    b, s, _ = x.shape
    nq = s // TQ
    n_a = a_w_in.shape[0]
    n_b = b_w_in.shape[0]
    swa_tab, near, wfar, t31, cmpb = _make_tables(rel_table.astype(F32) * LOG2E, nq)
    ovt = _overlap_t(s)

    a_shapes, a_specs = _a_outs(b, s)
    b_shapes, b_specs = _b_outs(b, s)
    xt_shape, xt_spec = a_shapes[0], a_specs[0]

    ws, gains = _a_weights(a_w_in[0], a_q_gain[0], a_k_gain[0])
    consts = [_bcast(a_norm[0])] + ws + gains
    xt, qt, k, vt, gt = _proj_call(
        _first_kernel, "proj_first", b, s, [x] + consts,
        [_nat_spec(D_MODEL)] + _specs_for(consts), a_shapes, a_specs)

    for layer in range(n_a):
        sink_row = jnp.broadcast_to(
            (a_sink[layer].astype(F32) * LOG2E).reshape(N_KV_HEADS, 1, GROUP, 1), (N_KV_HEADS, 1, GROUP, TQ)
        ).reshape(1, N_KV_HEADS * GL)
        ot = _swa_attention(qt, k, vt, gt, swa_tab, sink_row)
        wo = a_w_out[layer].T.astype(BF16)
        if layer + 1 < n_a:
            ws, gains = _a_weights(a_w_in[layer + 1], a_q_gain[layer + 1], a_k_gain[layer + 1])
            consts = [wo, _bcast(a_norm[layer + 1])] + ws + gains
            xt, qt, k, vt, gt = _proj_call(
                _a2a_kernel, "proj_a2a", b, s, [xt, ot] + consts,
                [_tok_spec(D_MODEL), _tok_spec(HQ)] + _specs_for(consts), a_shapes, a_specs)
        else:
            wsb, gb = _b_weights(b_w_in[0], b_q_gain[0])
            consts = ([wo, _bcast(kv_norm), kv_w.T.astype(BF16), _bcast(kv_k_gain[1]), _bcast(kv_k_gain[2]),
                       _bcast(b_norm[0])] + wsb + gb)
            kv_shapes = [jax.ShapeDtypeStruct((b, 2, HKV // PAIR, s, PAIR), F32),
                         jax.ShapeDtypeStruct((b, s, HKV), BF16), jax.ShapeDtypeStruct((b, HKV, s), BF16),
                         jax.ShapeDtypeStruct((b, s, HKV), BF16), jax.ShapeDtypeStruct((b, HKV, s), BF16)]
            kv_specs = [pl.BlockSpec((1, 2, HKV // PAIR, TM, PAIR), lambda b_, t: (b_, 0, 0, t, 0)),
                        _nat_spec(HKV), _tok_spec(HKV), _nat_spec(HKV), _tok_spec(HKV)]
            xt, craw, kslc, vslc, kwin, vwin, qt, gt = _proj_call(
                _a2b_kernel, "proj_a2b", b, s, [xt, ot] + consts,
                [_tok_spec(D_MODEL), _tok_spec(HQ)] + _specs_for(consts),
                [xt_shape] + kv_shapes + b_shapes, [xt_spec] + kv_specs + b_specs)

    cmp_nat, cmp_t = _compress(craw, cmp_k_pos, cmp_k_w1, cmp_k_w2, cmp_v_pos, cmp_v_w1, cmp_v_w2,
                               kv_k_gain[0])
    tabs = (near, wfar, t31, cmpb)
    for layer in range(n_b):
        ot = _nsa_attention(qt, gt, cmp_nat, cmp_t, kslc, vslc, kwin, vwin, tabs, ovt)
        wo = b_w_out[layer].T.astype(BF16)
        if layer + 1 < n_b:
            wsb, gb = _b_weights(b_w_in[layer + 1], b_q_gain[layer + 1])
            consts = [wo, _bcast(b_norm[layer + 1])] + wsb + gb
            xt, qt, gt = _proj_call(
                _b2b_kernel, "proj_b2b", b, s, [xt, ot] + consts,
                [_tok_spec(D_MODEL), _tok_spec(HQ)] + _specs_for(consts),
                [xt_shape] + b_shapes, [xt_spec] + b_specs)
        else:
            out = _proj_call(
                _final_kernel, "proj_final", b, s, [xt, ot, wo],
                [_tok_spec(D_MODEL, TM_FINAL), _tok_spec(HQ, TM_FINAL), _const_spec(wo.shape)],
                jax.ShapeDtypeStruct((b, s, D_MODEL), F32), _nat_spec(D_MODEL, TM_FINAL), TM_FINAL)
    return out
```

```python
import functools
import math

import numpy as np
import jax
import jax.numpy as jnp
from jax import lax
from jax.experimental import pallas as pl
from jax.experimental.pallas import tpu as pltpu

D_MODEL = 1024
HEAD_DIM = 64
N_HEADS = 16
N_KV_HEADS = 4
GROUP = N_HEADS // N_KV_HEADS
HQ = N_HEADS * HEAD_DIM
HKV = N_KV_HEADS * HEAD_DIM
N_BRANCH = 3
SWA_WINDOW = 128
NSA_WINDOW = 512
CMP_LEN = 32
CMP_STRIDE = 16
CMP_HIDDEN = 256
SEL_BLOCK = 64
SEL_TOP = 8
SEL_FORCE_LOCAL = 2
NUM_BUCKETS = 32
MAX_DISTANCE = 128
EPS = 1e-6
NEG = -1e30
FORCE_BONUS = 1e6

TQ = 128
TM = 512
TM_FINAL = 1024
SWA_TILES = 8
GL = GROUP * TQ
PAIR = 2 * HEAD_DIM
NEAR = 2 * TQ
WIN_FAR = NSA_WINDOW - TQ
N_WIN_VARIANTS = NSA_WINDOW // TQ + 1
FAR = 4 * TQ
ONES_ROWS = 16
LOG2E = math.log2(math.e)
Q_SCALE = HEAD_DIM ** -0.5 * LOG2E
VMEM_LIMIT = 56 * 1024 * 1024

F32 = jnp.float32
BF16 = jnp.bfloat16


def _bucket_np(dist):
    d = np.maximum(dist, 0)
    max_exact = NUM_BUCKETS // 2
    ratio = (np.log(np.maximum(d, 1).astype(np.float32) / np.float32(max_exact))
             / np.float32(math.log(MAX_DISTANCE / max_exact))
             * np.float32(NUM_BUCKETS - max_exact))
    large = np.minimum(max_exact + ratio.astype(np.int32), NUM_BUCKETS - 1)
    return np.where(d < max_exact, d, large).astype(np.int32)


def _dist_vector(table, d_lo, length, hi_valid):
    d = d_lo + np.arange(length)
    v = table.T[:, _bucket_np(d)]
    return jnp.where(jnp.asarray((d >= 0) & (d < hi_valid))[None], v, NEG)


def _toeplitz_kernel(specs, *refs):
    n = len(specs)
    for (rows, step), v_ref, o_ref in zip(specs, refs[:n], refs[n:]):
        length = v_ref.shape[2]
        x = jnp.broadcast_to(v_ref[0], (rows, length))
        shift = (length - step * (rows - 1)) % length
        o_ref[...] = pltpu.roll(x, shift, 1, stride=step, stride_axis=0)[:, :TQ]


def _toeplitz_tables(table, requests):
    specs, vecs = [], []
    for n_rows, d_first, step, hi_valid in requests:
        rows = -(-n_rows // 8) * 8
        length = -(-(step * (rows - 1) + TQ) // TQ) * TQ
        d_lo = d_first - step * (rows - 1)
        vecs.append(_dist_vector(table, d_lo, length, hi_valid)[:, None, :])
        specs.append((rows, step))
    outs = pl.pallas_call(
        functools.partial(_toeplitz_kernel, tuple(specs)), name="bias_tables", grid=(N_HEADS,),
        in_specs=[pl.BlockSpec((1, 1, v.shape[2]), lambda h: (h, 0, 0)) for v in vecs],
        out_specs=[pl.BlockSpec((rows, TQ), lambda h: (0, h)) for rows, _ in specs],
        out_shape=[jax.ShapeDtypeStruct((rows, N_HEADS * TQ), F32) for rows, _ in specs],
    )(*vecs)
    return [o[:req[0]] for o, req in zip(outs, requests)]


def _mask_rows(t, n_masked):
    rows = np.arange(t.shape[0])[:, None] < n_masked
    return jnp.where(jnp.asarray(rows), NEG, t)


def _make_tables(table, nq):
    big = 1 << 30
    per = TQ // CMP_STRIDE
    ncp = nq * per
    off = per * (nq - 1)
    sat_c = -(-(MAX_DISTANCE + CMP_LEN - 1) // CMP_STRIDE)
    assert off >= sat_c
    wide = N_HEADS * TQ
    near, wcut, band = _toeplitz_tables(table, [
        (NEAR, TQ, 1, big),
        (TQ, NSA_WINDOW, 1, NSA_WINDOW),
        (sat_c + per - 1, CMP_STRIDE * (sat_c - 1) - (CMP_LEN - 1), CMP_STRIDE, big)])
    in_window = (TQ - np.arange(NEAR)[:, None] + np.arange(wide)[None, :] % TQ) < SWA_WINDOW
    swa = jnp.where(jnp.asarray(in_window), near, NEG)
    near = jnp.stack([_mask_rows(near, TQ), near])
    swa = jnp.stack([_mask_rows(swa, TQ), swa])
    t31 = jnp.broadcast_to(table[NUM_BUCKETS - 1][:, None], (N_HEADS, TQ)).reshape(1, wide)

    def saturated(rows):
        return jnp.broadcast_to(t31, (rows, wide))

    wfar = jnp.concatenate([wcut, saturated(WIN_FAR - TQ)], axis=0)
    wfar = jnp.stack([_mask_rows(wfar, min(WIN_FAR, (N_WIN_VARIANTS - 1 - v) * TQ))
                      for v in range(N_WIN_VARIANTS)])
    cmpb = jnp.concatenate([saturated(off - sat_c + 1), band, jnp.full((ncp - per, wide), NEG, F32)], axis=0)
    return swa, near, wfar, t31, cmpb


def _overlap_t(s):
    ncp = s // CMP_STRIDE
    nsel = s // SEL_BLOCK
    cs = np.arange(ncp)[None, :] * CMP_STRIDE
    ss = np.arange(nsel)[:, None] * SEL_BLOCK
    ov = (cs < ss + SEL_BLOCK) & (cs + CMP_LEN > ss) & (np.arange(ncp)[None, :] < ncp - 1)
    return jnp.asarray(ov.astype(np.float32), dtype=BF16)


def _lanes(gain_b, n):
    return jnp.concatenate([gain_b] * (n // gain_b.shape[1]), axis=1)


def _rms_t(xt, gain_b):
    ms = jnp.mean(xt * xt, axis=0, keepdims=True)
    return (xt * lax.rsqrt(ms + EPS) * _lanes(gain_b, xt.shape[1])).astype(BF16)


def _head_norm_t(a, gain_b):
    outs = []
    for h in range(a.shape[0] // HEAD_DIM):
        blk = a[h * HEAD_DIM:(h + 1) * HEAD_DIM]
        ms = jnp.mean(blk * blk, axis=0, keepdims=True)
        outs.append(blk * lax.rsqrt(ms + EPS) * _lanes(gain_b, a.shape[1]))
    return jnp.concatenate(outs, axis=0)


def _sigmoid_h(hz):
    return 0.5 + 0.5 * jnp.tanh(hz)


def _silu_h(hz):
    return hz + hz * jnp.tanh(hz)


def _silu(z):
    return _silu_h(0.5 * z)


def _dot(a, b):
    return jnp.dot(a, b, preferred_element_type=F32)


def _proj_a(xn, w, qg, kg, q_out, k_out, v_out, g_out):
    pq = _dot(w[:HQ, :], xn)
    pkv = _dot(w[HQ:HQ + 2 * HKV, :], xn)
    q_out[0] = _head_norm_t(pq, qg[...]).astype(BF16)
    pz = _dot(w[HQ + 2 * HKV:, :], xn)
    k_out[0] = _head_norm_t(pkv[:HKV], kg[...]).T.astype(BF16)
    v_out[0] = pkv[HKV:].astype(BF16)
    g_out[0] = _silu_h(pz).astype(BF16)


def _proj_b(xn, w, qg, q_out, g_out):
    n_gate = N_BRANCH * N_HEADS
    pq = _dot(w[:HQ, :], xn)
    sg = _sigmoid_h(_dot(w[HQ:HQ + n_gate, :], xn))

    def gate_rows(pz, c):
        for hd in range(N_HEADS):
            r = c * N_HEADS + hd
            z = pz[hd * HEAD_DIM:(hd + 1) * HEAD_DIM]
            g_out[0, r * HEAD_DIM:(r + 1) * HEAD_DIM, :] = (_silu_h(z) * sg[r:r + 1]).astype(BF16)

    z0 = HQ + n_gate
    pz = _dot(w[z0:z0 + HQ, :], xn)
    q_out[0] = _head_norm_t(pq, qg[...]).astype(BF16)
    for c in range(1, N_BRANCH):
        nxt = _dot(w[z0 + c * HQ:z0 + (c + 1) * HQ, :], xn)
        gate_rows(pz, c - 1)
        pz = nxt
    gate_rows(pz, N_BRANCH - 1)


def _proj_kv(xn, wkv, kg1, kg2, craw_out, kslc_out, vslc_out, kwin_out, vwin_out):
    p = _dot(wkv[...], xn)

    def rows(n):
        return p[n * HKV:(n + 1) * HKV]
    for t in range(2):
        nat = rows(t).T
        for pr in range(HKV // PAIR):
            craw_out[0, t, pr] = nat[:, _pair_cols(pr)]
    kslc_out[0] = _head_norm_t(rows(2), kg1[...]).T.astype(BF16)
    vslc_out[0] = rows(3).astype(BF16)
    kwin_out[0] = _head_norm_t(rows(4), kg2[...]).T.astype(BF16)
    vwin_out[0] = rows(5).astype(BF16)


def _first_kernel(x_ref, ng, w, qg, kg, xt_out, q_out, k_out, v_out, g_out):
    sub = TM // 2
    for j in range(TM // sub):
        c = slice(j * sub, (j + 1) * sub)
        xt = x_ref[0, c, :].T
        xt_out[0, :, c] = xt
        _proj_a(_rms_t(xt, ng[...]), w, qg, kg, q_out.at[:, :, c], k_out.at[:, c, :], v_out.at[:, :, c],
                g_out.at[:, :, c])


def _a2a_kernel(x_ref, o_ref, wo, ng, w, qg, kg, xt_out, q_out, k_out, v_out, g_out):
    xt = x_ref[0] + _dot(wo[...], o_ref[0])
    xt_out[0] = xt
    _proj_a(_rms_t(xt, ng[...]), w, qg, kg, q_out, k_out, v_out, g_out)


def _a2b_kernel(x_ref, o_ref, wo, ngkv, wkv, kg1, kg2, ngb, w, qg,
                xt_out, craw_out, kslc_out, vslc_out, kwin_out, vwin_out, q_out, g_out):
    xt = x_ref[0] + _dot(wo[...], o_ref[0])
    xt_out[0] = xt
    _proj_kv(_rms_t(xt, ngkv[...]), wkv, kg1, kg2, craw_out, kslc_out, vslc_out, kwin_out, vwin_out)
    _proj_b(_rms_t(xt, ngb[...]), w, qg, q_out, g_out)


def _b2b_kernel(x_ref, o_ref, wo, ngb, w, qg, xt_out, q_out, g_out):
    xt = x_ref[0] + _dot(wo[...], o_ref[0])
    xt_out[0] = xt
    _proj_b(_rms_t(xt, ngb[...]), w, qg, q_out, g_out)


def _final_kernel(x_ref, o_ref, wo, x_out):
    x_out[0] = (x_ref[0] + _dot(wo[...], o_ref[0])).T


def _tok_spec(rows, tm=TM):
    return pl.BlockSpec((1, rows, tm), lambda b, t: (b, 0, t))


def _nat_spec(cols, tm=TM):
    return pl.BlockSpec((1, tm, cols), lambda b, t: (b, t, 0))


def _const_spec(shape):
    nd = len(shape)
    return pl.BlockSpec(shape, lambda b, t: (0,) * nd, pipeline_mode=pl.Buffered(1))


def _proj_call(body, name, b, s, ins, in_specs, out_shapes, out_specs, tm=TM):
    return pl.pallas_call(
        body, name=name, grid=(b, s // tm),
        in_specs=in_specs, out_specs=out_specs, out_shape=out_shapes,
        compiler_params=pltpu.CompilerParams(
            dimension_semantics=("parallel", "parallel"), vmem_limit_bytes=VMEM_LIMIT),
    )(*ins)


def _bcast(v, scale=1.0):
    return jnp.broadcast_to((v.astype(F32) * scale)[:, None], (v.shape[0], TQ))


def _a_weights(w_in, q_gain, k_gain):
    halve = np.where(np.arange(w_in.shape[1]) >= HQ + 2 * HKV, 0.5, 1.0).astype(np.float32)
    ws = [(w_in * halve).T.astype(BF16)]
    gains = [_bcast(q_gain, Q_SCALE), _bcast(k_gain)]
    return ws, gains


def _b_weights(w_in, q_gain):
    halve = np.where(np.arange(w_in.shape[1]) >= HQ, 0.5, 1.0).astype(np.float32)
    ws = [(w_in * halve).T.astype(BF16)]
    return ws, [_bcast(q_gain, Q_SCALE)]


def _a_outs(b, s):
    shapes = [jax.ShapeDtypeStruct((b, D_MODEL, s), F32), jax.ShapeDtypeStruct((b, HQ, s), BF16),
              jax.ShapeDtypeStruct((b, s, HKV), BF16), jax.ShapeDtypeStruct((b, HKV, s), BF16),
              jax.ShapeDtypeStruct((b, HQ, s), BF16)]
    specs = [_tok_spec(D_MODEL), _tok_spec(HQ), _nat_spec(HKV), _tok_spec(HKV), _tok_spec(HQ)]
    return shapes, specs


def _b_outs(b, s):
    shapes = [jax.ShapeDtypeStruct((b, HQ, s), BF16), jax.ShapeDtypeStruct((b, N_BRANCH * HQ, s), BF16)]
    specs = [_tok_spec(HQ), _tok_spec(N_BRANCH * HQ)]
    return shapes, specs


def _specs_for(arrs):
    return [_const_spec(a.shape) for a in arrs]


def _head_rows(h, g):
    return slice((h * GROUP + g) * HEAD_DIM, (h * GROUP + g + 1) * HEAD_DIM)


def _kv_rows(h):
    return slice(h * HEAD_DIM, (h + 1) * HEAD_DIM)


def _head_lanes(h):
    return slice(h * GL, (h + 1) * GL)


def _pair_lanes(p):
    return slice(2 * p * GL, 2 * (p + 1) * GL)


def _pair_cols(p):
    return slice(p * PAIR, (p + 1) * PAIR)


def _padded_q(q_ref, h, cols=slice(None)):
    qs = jnp.concatenate([q_ref[0, _head_rows(h, g), cols] for g in range(GROUP)], axis=1)
    zeros = jnp.zeros_like(qs)
    return jnp.concatenate([qs, zeros] if h % 2 == 0 else [zeros, qs], axis=0)


def _with_ones(v):
    return jnp.concatenate([v, jnp.ones((ONES_ROWS, v.shape[1]), BF16)], axis=0)


def _exp2_bf16(x):
    return jnp.exp2(x.astype(BF16))


def _swa_kernel(q_ref, kp_ref, kc_ref, vp_ref, vc_ref, g_ref, bias_ref, sink_ref, o_ref):
    i = pl.program_id(1)
    sink = sink_ref[...]
    for j in range(SWA_TILES):
        cur = slice(j * TQ, (j + 1) * TQ)
        prev = slice((j - 1) * TQ, j * TQ)
        k_prev = (lambda c: kp_ref[0, :, c]) if j == 0 else (lambda c: kc_ref[0, prev, c])
        v_prev = (lambda r: vp_ref[0, r, :]) if j == 0 else (lambda r: vc_ref[0, r, prev])
        bias = bias_ref[jnp.minimum(i, 1)] if j == 0 else bias_ref[1]
        pairs = []
        for p in range(N_KV_HEADS // 2):
            k = jnp.concatenate([k_prev(_pair_cols(p)), kc_ref[0, cur, _pair_cols(p)]], axis=0)
            qp = jnp.concatenate([_padded_q(q_ref, 2 * p, cur), _padded_q(q_ref, 2 * p + 1, cur)], axis=1)
            pairs.append(_dot(k, qp))
        s = jnp.concatenate(pairs, axis=1) + bias
        m = jnp.maximum(jnp.max(s, axis=0, keepdims=True), sink)
        e = _exp2_bf16(s - m)
        e_sink = jnp.exp2(sink - m)
        for h in range(N_KV_HEADS):
            v = jnp.concatenate([v_prev(_kv_rows(h)), vc_ref[0, _kv_rows(h), cur]], axis=1)
            acc = _dot(_with_ones(v), e[:, _head_lanes(h)])
            o = acc[:HEAD_DIM] / (acc[HEAD_DIM:HEAD_DIM + 1] + e_sink[:, _head_lanes(h)])
            for g in range(GROUP):
                rows = _head_rows(h, g)
                o_ref[0, rows, cur] = (o[:, g * TQ:(g + 1) * TQ] * g_ref[0, rows, cur].astype(F32)).astype(BF16)


def _swa_attention(qt, k, vt, gt, swa_tab, sink_row):
    b, _, s = qt.shape
    tw = SWA_TILES * TQ
    prev = lambda i: jnp.maximum(SWA_TILES * i - 1, 0)
    return pl.pallas_call(
        _swa_kernel, name="swa_attention", grid=(b, s // tw),
        in_specs=[
            pl.BlockSpec((1, HQ, tw), lambda b_, i: (b_, 0, i)),
            pl.BlockSpec((1, TQ, HKV), lambda b_, i: (b_, prev(i), 0)),
            pl.BlockSpec((1, tw, HKV), lambda b_, i: (b_, i, 0)),
            pl.BlockSpec((1, HKV, TQ), lambda b_, i: (b_, 0, prev(i))),
            pl.BlockSpec((1, HKV, tw), lambda b_, i: (b_, 0, i)),
            pl.BlockSpec((1, HQ, tw), lambda b_, i: (b_, 0, i)),
            pl.BlockSpec(memory_space=pltpu.VMEM), pl.BlockSpec(memory_space=pltpu.VMEM),
        ],
        out_specs=pl.BlockSpec((1, HQ, tw), lambda b_, i: (b_, 0, i)),
        out_shape=jax.ShapeDtypeStruct((b, HQ, s), BF16),
        compiler_params=pltpu.CompilerParams(
            dimension_semantics=("parallel", "arbitrary"), vmem_limit_bytes=VMEM_LIMIT),
    )(qt, k, k, vt, vt, gt, swa_tab, sink_row)


def _nsa_kernel(nq, q_ref, g_ref, kcmp_ref, vcmp_ref, kslc_ref, vslc_ref, kwin_ref, vwin_ref,
                cmpb_ref, near_ref, wfar_ref, t31_ref, ovt_ref, kauxf_ref, kauxn_ref, o_ref,
                qp_ref, m_ref, acc_ref, selb_ref, part_ref, sc0_ref, sc1_ref):
    i = pl.program_id(1)
    ncp = kcmp_ref.shape[2]
    nsel = ovt_ref.shape[0]
    n_pairs = N_KV_HEADS // 2
    wide = N_KV_HEADS * GL
    cmp_start = pl.multiple_of((TQ // CMP_STRIDE) * (nq - 1 - i), 8)
    prev_rows = pl.ds(pl.multiple_of(jnp.maximum(i - 1, 0) * TQ, TQ), TQ)
    cur_rows = pl.ds(pl.multiple_of(i * TQ, TQ), TQ)
    far_starts = [pl.multiple_of(jnp.maximum(i - back, 0) * TQ, TQ) for back in range(NSA_WINDOW // TQ, 1, -1)]
    n_far_blocks = (TQ // SEL_BLOCK) * (i - 1)

    def near_keys(k_ref, p):
        return jnp.concatenate([k_ref[0, prev_rows, _pair_cols(p)], k_ref[0, cur_rows, _pair_cols(p)]], axis=0)

    def near_values(v_ref, h):
        return jnp.concatenate([v_ref[0, _kv_rows(h), prev_rows], v_ref[0, _kv_rows(h), cur_rows]], axis=1)

    def scores(lhs_of_pair, rows):
        return jnp.concatenate([_dot(lhs_of_pair(p), qp_ref[rows, _pair_lanes(p)]) for p in range(n_pairs)], axis=1)

    for h in range(N_KV_HEADS):
        qp_ref[:PAIR, _head_lanes(h)] = _padded_q(q_ref, h)
    q_rows = slice(0, PAIR)
    near_tab = near_ref[0]
    bc = cmpb_ref[pl.ds(cmp_start, ncp), :]
    sc = scores(lambda p: kcmp_ref[0, 0, :, _pair_cols(p)], q_rows) + bc
    mc = jnp.max(sc, axis=0, keepdims=True)
    s_near = scores(lambda p: near_keys(kwin_ref, p), q_rows) + near_tab
    s_far = scores(lambda p: jnp.concatenate([kwin_ref[0, pl.ds(st, TQ), _pair_cols(p)] for st in far_starts], axis=0),
                   q_rows) + wfar_ref[0]
    mw = jnp.maximum(jnp.max(s_near, axis=0, keepdims=True), jnp.max(s_far, axis=0, keepdims=True))

    valid = bc > 0.5 * NEG
    ec = jnp.where(valid, jnp.exp2(sc - mc), 0.0)
    lc = jnp.sum(ec, axis=0, keepdims=True)
    pc = ec / jnp.where(lc > 0.0, lc, 1.0)
    pc_b = pc.astype(BF16)
    o_cmp = [_dot(vcmp_ref[0, 0, _kv_rows(h), :], pc_b[:, _head_lanes(h)]) for h in range(N_KV_HEADS)]

    psum = jnp.concatenate(
        [sum(pc[:, h * GL + g * TQ:h * GL + (g + 1) * TQ] for g in range(GROUP)) for h in range(N_KV_HEADS)], axis=1)
    ovt = ovt_ref[...]
    p1 = psum.astype(BF16)
    r1 = psum - p1.astype(F32)
    p2 = r1.astype(BF16)
    p3 = (r1 - p2.astype(F32)).astype(BF16)
    imp = _dot(ovt, p1) + _dot(ovt, p2) + _dot(ovt, p3)
    lane = lax.broadcasted_iota(jnp.int32, imp.shape, 1)
    pos = i * TQ + lane % TQ
    blk = lax.broadcasted_iota(jnp.int32, imp.shape, 0)
    causal = blk * SEL_BLOCK <= pos
    rel = pos // SEL_BLOCK - blk
    forced = (blk == 0) | ((rel >= 0) & (rel < SEL_FORCE_LOCAL))
    score = jnp.where(causal, imp + jnp.where(forced, FORCE_BONUS, 0.0), NEG)
    groups = [score[r:r + 8] for r in range(0, nsel, 8)]
    counts = [jnp.zeros((8, imp.shape[1]), jnp.int32) for _ in groups]
    row_in_group = lax.broadcasted_iota(jnp.int32, (8, imp.shape[1]), 0)
    for jp in range(nsel):
        row = score[jp:jp + 1, :]
        for gi, grp in enumerate(groups):
            if gi * 8 > jp:
                beats = row >= grp
            elif gi * 8 + 7 < jp:
                beats = row > grp
            else:
                beats = (row > grp) | ((row == grp) & (row_in_group > jp - gi * 8))
            counts[gi] = counts[gi] + beats.astype(jnp.int32)
    sel = jnp.concatenate(counts, axis=0) < min(SEL_TOP, nsel)

    def per_group(x):
        return jnp.concatenate([x[:, h * TQ:(h + 1) * TQ] for h in range(N_KV_HEADS) for _ in range(GROUP)], axis=1)

    selb_ref[...] = per_group(jnp.where(sel, 0.0, NEG))
    sel_far = per_group(jnp.where(sel & (blk < n_far_blocks), 0.0, NEG))
    t31 = t31_ref[...]
    hi = t31.astype(BF16).astype(F32)
    row8 = lax.broadcasted_iota(jnp.int32, (8, wide), 0)
    const_rows = jnp.where(row8 == 0, hi, jnp.where(row8 == 1, t31 - hi, 0.0))
    first_blk = (TQ // SEL_BLOCK) * (i - 1)
    near_rows = [selb_ref[pl.ds(jnp.maximum(first_blk + u, 0), 1), :] for u in range(NEAR // SEL_BLOCK)]
    aux = jnp.concatenate(
        [sel_far, const_rows] + near_rows + [jnp.zeros((PAIR - nsel - 8 - NEAR // SEL_BLOCK, wide), F32)], axis=0)
    qp_ref[PAIR:, :] = aux.astype(BF16)

    all_rows = slice(0, 2 * PAIR)
    s = scores(lambda p: jnp.concatenate([near_keys(kslc_ref, p), kauxn_ref[...]], axis=1), all_rows) + near_tab
    m = jnp.max(s, axis=0, keepdims=True)

    sc_refs = (sc0_ref, sc1_ref)

    def sweep_scores(step, dst_ref, cols, lanes):
        keys = slice(step * FAR, (step + 1) * FAR)
        lhs = jnp.concatenate([kslc_ref[0, keys, cols], kauxf_ref[keys, :]], axis=1)
        sc_new = _dot(lhs, qp_ref[:, lanes])
        dst_ref[:FAR, lanes] = sc_new
        dst_ref[FAR:FAR + 1, lanes] = jnp.max(sc_new, axis=0, keepdims=True)

    for p in range(n_pairs):
        sweep_scores(0, sc_refs[0], _pair_cols(p), _pair_lanes(p))

    m_ref[...] = m
    e = _exp2_bf16(s - m)
    for h in range(N_KV_HEADS):
        acc_ref[h] = _dot(_with_ones(near_values(vslc_ref, h)), e[:, _head_lanes(h)])

    e_near = _exp2_bf16(s_near - mw)
    e_far = _exp2_bf16(s_far - mw)
    for h in range(N_KV_HEADS):
        far_v = jnp.concatenate([vwin_ref[0, _kv_rows(h), pl.ds(st, TQ)] for st in far_starts], axis=1)
        acc_w = (_dot(_with_ones(near_values(vwin_ref, h)), e_near[:, _head_lanes(h)])
                 + _dot(_with_ones(far_v), e_far[:, _head_lanes(h)]))
        o_win = acc_w[:HEAD_DIM] / acc_w[HEAD_DIM:HEAD_DIM + 1]
        for g in range(GROUP):
            rows = _head_rows(h, g)
            lanes = slice(g * TQ, (g + 1) * TQ)
            part_ref[rows, :] = (o_cmp[h][:, lanes] * g_ref[0, 0, rows, :].astype(F32)
                                 + o_win[:, lanes] * g_ref[0, 2, rows, :].astype(F32))

    def sweep_consume(step, src_ref, lanes, heads, n_keys=FAR):
        s = src_ref[:n_keys, lanes]
        m_old = m_ref[:, lanes]
        m_new = jnp.maximum(m_old, src_ref[FAR:FAR + 1, lanes])
        alpha = jnp.exp2(m_old - m_new)
        e = _exp2_bf16(s - m_new)
        for j, h in enumerate(heads):
            sub = slice(j * GL, (j + 1) * GL)
            v = _with_ones(vslc_ref[0, _kv_rows(h), step * FAR:step * FAR + n_keys])
            acc_ref[h] = alpha[:, sub] * acc_ref[h] + _dot(v, e[:, sub])
        m_ref[:, lanes] = m_new

    n_far_chunks = jnp.maximum(i - 1, 0)
    n_steps = (n_far_chunks + FAR // TQ - 1) // (FAR // TQ)
    max_steps = kslc_ref.shape[1] // FAR
    for step in range(max_steps):
        if step + 1 < max_steps:
            @pl.when(step + 1 < n_steps)
            def _(step=step):
                for h in range(N_KV_HEADS):
                    sweep_consume(step, sc_refs[step % 2], _head_lanes(h), (h,))
                    sweep_scores(step + 1, sc_refs[(step + 1) % 2], _pair_cols(h // 2), _head_lanes(h))

        for chunks in range(1, FAR // TQ + 1):
            @pl.when(n_far_chunks == step * (FAR // TQ) + chunks)
            def _(step=step, chunks=chunks):
                sweep_consume(step, sc_refs[step % 2], slice(0, wide), tuple(range(N_KV_HEADS)), chunks * TQ)

    for h in range(N_KV_HEADS):
        o_slc = acc_ref[h, :HEAD_DIM, :] / acc_ref[h, HEAD_DIM:HEAD_DIM + 1, :]
        for g in range(GROUP):
            rows = _head_rows(h, g)
            o = part_ref[rows, :] + o_slc[:, g * TQ:(g + 1) * TQ] * g_ref[0, 1, rows, :].astype(F32)
            o_ref[0, rows, :] = o.astype(BF16)


def _nsa_attention(qt, gt, cmp_nat, cmp_t, kslc, vslc, kwin, vwin, tabs, ovt):
    b, _, s = qt.shape
    nq = s // TQ
    assert s % FAR == 0
    ncp = s // CMP_STRIDE
    nsel = s // SEL_BLOCK
    near, wfar, t31, cmpb = tabs
    g4 = gt.reshape(b, N_BRANCH, HQ, s)
    assert nsel + 8 + NEAR // SEL_BLOCK <= PAIR
    cols = np.arange(PAIR)[None, :]
    key_blk = np.arange(s)[:, None] // SEL_BLOCK
    kaux_far = jnp.asarray(((cols == key_blk) | (cols == nsel) | (cols == nsel + 1)).astype(np.float32), BF16)
    near_blk = np.arange(NEAR)[:, None] // SEL_BLOCK
    kaux_near = jnp.asarray((cols == nsel + 8 + near_blk).astype(np.float32), BF16)
    resident = pl.BlockSpec(memory_space=pltpu.VMEM)
    return pl.pallas_call(
        functools.partial(_nsa_kernel, nq), name="nsa_attention", grid=(b, nq),
        in_specs=[
            pl.BlockSpec((1, HQ, TQ), lambda b_, i: (b_, 0, i)),
            pl.BlockSpec((1, N_BRANCH, HQ, TQ), lambda b_, i: (b_, 0, 0, i)),
            pl.BlockSpec((1, 1, ncp, HKV), lambda b_, i: (0, b_, 0, 0)),
            pl.BlockSpec((1, 1, HKV, ncp), lambda b_, i: (1, b_, 0, 0)),
            pl.BlockSpec((1, s, HKV), lambda b_, i: (b_, 0, 0)),
            pl.BlockSpec((1, HKV, s), lambda b_, i: (b_, 0, 0)),
            pl.BlockSpec((1, s, HKV), lambda b_, i: (b_, 0, 0)),
            pl.BlockSpec((1, HKV, s), lambda b_, i: (b_, 0, 0)),
            resident,
            pl.BlockSpec((1, NEAR, N_KV_HEADS * GL), lambda b_, i: (jnp.minimum(i, 1), 0, 0)),
            pl.BlockSpec((1, WIN_FAR, N_KV_HEADS * GL),
                         lambda b_, i: (jnp.minimum(i, N_WIN_VARIANTS - 1), 0, 0)),
            resident, resident, resident, resident,
        ],
        out_specs=pl.BlockSpec((1, HQ, TQ), lambda b_, i: (b_, 0, i)),
        out_shape=jax.ShapeDtypeStruct((b, HQ, s), BF16),
        scratch_shapes=[pltpu.VMEM((2 * PAIR, N_KV_HEADS * GL), BF16),
                        pltpu.VMEM((1, N_KV_HEADS * GL), F32),
                        pltpu.VMEM((N_KV_HEADS, HEAD_DIM + ONES_ROWS, GL), F32),
                        pltpu.VMEM((nsel, N_KV_HEADS * GL), F32),
                        pltpu.VMEM((HQ, TQ), F32),
                        pltpu.VMEM((FAR + 8, N_KV_HEADS * GL), F32), pltpu.VMEM((FAR + 8, N_KV_HEADS * GL), F32)],
        compiler_params=pltpu.CompilerParams(
            dimension_semantics=("parallel", "arbitrary"), vmem_limit_bytes=VMEM_LIMIT),
    )(qt, g4, cmp_nat, cmp_t, kslc, vslc, kwin, vwin, cmpb, near, wfar, t31, ovt, kaux_far, kaux_near)


def _compress_kernel(u_ref, ptop_ref, pbot_ref, wtop_ref, wbot_ref, w2_ref, kg_ref, nat_out, t_out):
    t = pl.program_id(0)
    nch = u_ref.shape[3] // CMP_STRIDE
    hidden = []
    for p in range(HKV // PAIR):
        u = jnp.concatenate([u_ref[0, 0, p, pl.ds(j, nch, stride=CMP_STRIDE), :] for j in range(CMP_STRIDE)], axis=1)
        top = (u + ptop_ref[0]).astype(BF16)
        bot = (u + pbot_ref[0]).astype(BF16)
        for e in range(2):
            bm = _dot(bot, wbot_ref[0, e])
            hidden.append(_dot(top, wtop_ref[0, e]) + pltpu.roll(bm, nch - 1, axis=0))
    hid = jnp.concatenate(hidden, axis=1)
    out_t = _dot(_silu(hid).astype(BF16), w2_ref[0]).T
    out_t = jnp.where(t == 0, _head_norm_t(out_t, kg_ref[...]), out_t)
    t_out[0, 0] = out_t.astype(BF16)
    nat_out[0, 0] = out_t.T.astype(BF16)


def _compress(craw, cmp_k_pos, cmp_k_w1, cmp_k_w2, cmp_v_pos, cmp_v_w1, cmp_v_w2, k_gain):
    b, _, n_pairs, s, _ = craw.shape
    nch = s // CMP_STRIDE
    wide = CMP_STRIDE * PAIR

    def padded_w1(w1_half):
        w = w1_half.reshape(CMP_STRIDE, HEAD_DIM, CMP_HIDDEN)
        z = jnp.zeros_like(w)
        both = jnp.stack([jnp.concatenate([w, z], axis=1), jnp.concatenate([z, w], axis=1)])
        return both.reshape(2, wide, CMP_HIDDEN).astype(BF16)

    def pos_row(p_half):
        return jnp.broadcast_to(p_half[:, None, :], (CMP_STRIDE, 2, HEAD_DIM)).reshape(1, wide)

    def big_w2(w2):
        eye = jnp.eye(N_KV_HEADS, dtype=F32)
        return jnp.einsum('nd,hk->hnkd', w2, eye).reshape(N_KV_HEADS * CMP_HIDDEN, HKV).astype(BF16)

    half = CMP_STRIDE * HEAD_DIM
    wtop = jnp.stack([padded_w1(cmp_k_w1[:half]), padded_w1(cmp_v_w1[:half])])
    wbot = jnp.stack([padded_w1(cmp_k_w1[half:]), padded_w1(cmp_v_w1[half:])])
    ptop = jnp.stack([pos_row(cmp_k_pos[:CMP_STRIDE]), pos_row(cmp_v_pos[:CMP_STRIDE])])
    pbot = jnp.stack([pos_row(cmp_k_pos[CMP_STRIDE:]), pos_row(cmp_v_pos[CMP_STRIDE:])])
    w2 = jnp.stack([big_w2(cmp_k_w2), big_w2(cmp_v_w2)])
    kg = jnp.broadcast_to(k_gain.astype(F32)[:, None], (HEAD_DIM, nch))
    sel = lambda shape: pl.BlockSpec((1,) + shape, lambda t, b_: (t,) + (0,) * len(shape))
    return pl.pallas_call(
        _compress_kernel, name="compress", grid=(2, b),
        in_specs=[
            pl.BlockSpec((1, 1, n_pairs, s, PAIR), lambda t, b_: (b_, t, 0, 0, 0)),
            sel((1, wide)), sel((1, wide)),
            sel((2, wide, CMP_HIDDEN)), sel((2, wide, CMP_HIDDEN)),
            sel((N_KV_HEADS * CMP_HIDDEN, HKV)),
            pl.BlockSpec((HEAD_DIM, nch), lambda t, b_: (0, 0)),
        ],
        out_specs=[pl.BlockSpec((1, 1, nch, HKV), lambda t, b_: (t, b_, 0, 0)),
                   pl.BlockSpec((1, 1, HKV, nch), lambda t, b_: (t, b_, 0, 0))],
        out_shape=[jax.ShapeDtypeStruct((2, b, nch, HKV), BF16),
                   jax.ShapeDtypeStruct((2, b, HKV, nch), BF16)],
        compiler_params=pltpu.CompilerParams(
            dimension_semantics=("arbitrary", "arbitrary"), vmem_limit_bytes=VMEM_LIMIT),
    )(craw, ptop, pbot, wtop, wbot, w2, kg)


def kernel(x, rel_table, a_norm, a_w_in, a_q_gain, a_k_gain, a_sink, a_w_out, kv_norm, kv_w,
           kv_k_gain, cmp_k_pos, cmp_k_w1, cmp_k_w2, cmp_v_pos, cmp_v_w1, cmp_v_w2,
           b_norm, b_w_in, b_q_gain, b_w_out):
    b, s, _ = x.shape
    nq = s // TQ
    n_a = a_w_in.shape[0]
    n_b = b_w_in.shape[0]
    swa_tab, near, wfar, t31, cmpb = _make_tables(rel_table.astype(F32) * LOG2E, nq)
    ovt = _overlap_t(s)

    a_shapes, a_specs = _a_outs(b, s)
    b_shapes, b_specs = _b_outs(b, s)
    xt_shape, xt_spec = a_shapes[0], a_specs[0]

    ws, gains = _a_weights(a_w_in[0], a_q_gain[0], a_k_gain[0])
    consts = [_bcast(a_norm[0])] + ws + gains
    xt, qt, k, vt, gt = _proj_call(
        _first_kernel, "proj_first", b, s, [x] + consts,
        [_nat_spec(D_MODEL)] + _specs_for(consts), a_shapes, a_specs)

    for layer in range(n_a):
        sink_row = jnp.broadcast_to(
            (a_sink[layer].astype(F32) * LOG2E).reshape(N_KV_HEADS, 1, GROUP, 1), (N_KV_HEADS, 1, GROUP, TQ)
        ).reshape(1, N_KV_HEADS * GL)
        ot = _swa_attention(qt, k, vt, gt, swa_tab, sink_row)
        wo = a_w_out[layer].T.astype(BF16)
        if layer + 1 < n_a:
            ws, gains = _a_weights(a_w_in[layer + 1], a_q_gain[layer + 1], a_k_gain[layer + 1])
            consts = [wo, _bcast(a_norm[layer + 1])] + ws + gains
            xt, qt, k, vt, gt = _proj_call(
                _a2a_kernel, "proj_a2a", b, s, [xt, ot] + consts,
                [_tok_spec(D_MODEL), _tok_spec(HQ)] + _specs_for(consts), a_shapes, a_specs)
        else:
            wsb, gb = _b_weights(b_w_in[0], b_q_gain[0])
            consts = ([wo, _bcast(kv_norm), kv_w.T.astype(BF16), _bcast(kv_k_gain[1]), _bcast(kv_k_gain[2]),
                       _bcast(b_norm[0])] + wsb + gb)
            kv_shapes = [jax.ShapeDtypeStruct((b, 2, HKV // PAIR, s, PAIR), F32),
                         jax.ShapeDtypeStruct((b, s, HKV), BF16), jax.ShapeDtypeStruct((b, HKV, s), BF16),
                         jax.ShapeDtypeStruct((b, s, HKV), BF16), jax.ShapeDtypeStruct((b, HKV, s), BF16)]
            kv_specs = [pl.BlockSpec((1, 2, HKV // PAIR, TM, PAIR), lambda b_, t: (b_, 0, 0, t, 0)),
                        _nat_spec(HKV), _tok_spec(HKV), _nat_spec(HKV), _tok_spec(HKV)]
            xt, craw, kslc, vslc, kwin, vwin, qt, gt = _proj_call(
                _a2b_kernel, "proj_a2b", b, s, [xt, ot] + consts,
                [_tok_spec(D_MODEL), _tok_spec(HQ)] + _specs_for(consts),
                [xt_shape] + kv_shapes + b_shapes, [xt_spec] + kv_specs + b_specs)

    cmp_nat, cmp_t = _compress(craw, cmp_k_pos, cmp_k_w1, cmp_k_w2, cmp_v_pos, cmp_v_w1, cmp_v_w2,
                               kv_k_gain[0])
    tabs = (near, wfar, t31, cmpb)
    for layer in range(n_b):
        ot = _nsa_attention(qt, gt, cmp_nat, cmp_t, kslc, vslc, kwin, vwin, tabs, ovt)
        wo = b_w_out[layer].T.astype(BF16)
        if layer + 1 < n_b:
            wsb, gb = _b_weights(b_w_in[layer + 1], b_q_gain[layer + 1])
            consts = [wo, _bcast(b_norm[layer + 1])] + wsb + gb
            xt, qt, gt = _proj_call(
                _b2b_kernel, "proj_b2b", b, s, [xt, ot] + consts,
                [_tok_spec(D_MODEL), _tok_spec(HQ)] + _specs_for(consts),
                [xt_shape] + b_shapes, [xt_spec] + b_specs)
        else:
            out = _proj_call(
                _final_kernel, "proj_final", b, s, [xt, ot, wo],
                [_tok_spec(D_MODEL, TM_FINAL), _tok_spec(HQ, TM_FINAL), _const_spec(wo.shape)],
                jax.ShapeDtypeStruct((b, s, D_MODEL), F32), _nat_spec(D_MODEL, TM_FINAL), TM_FINAL)
    return out
```

```python
import functools
import math

import numpy as np
import jax
import jax.numpy as jnp
from jax import lax
from jax.experimental import pallas as pl
from jax.experimental.pallas import tpu as pltpu

D_MODEL = 1024
HEAD_DIM = 64
N_HEADS = 16
N_KV_HEADS = 4
GROUP = N_HEADS // N_KV_HEADS
HQ = N_HEADS * HEAD_DIM
HKV = N_KV_HEADS * HEAD_DIM
N_BRANCH = 3
SWA_WINDOW = 128
NSA_WINDOW = 512
CMP_LEN = 32
CMP_STRIDE = 16
CMP_HIDDEN = 256
SEL_BLOCK = 64
SEL_TOP = 8
SEL_FORCE_LOCAL = 2
NUM_BUCKETS = 32
MAX_DISTANCE = 128
EPS = 1e-6
NEG = -1e30
FORCE_BONUS = 1e6

TQ = 128
TM = 512
TM_FINAL = 1024
SWA_TILES = 16
GL = GROUP * TQ
PAIR = 2 * HEAD_DIM
NEAR = 2 * TQ
WIN_FAR = NSA_WINDOW - TQ
N_WIN_VARIANTS = NSA_WINDOW // TQ + 1
FAR = 4 * TQ
ONES_ROWS = 16
LOG2E = math.log2(math.e)
Q_SCALE = HEAD_DIM ** -0.5 * LOG2E
VMEM_LIMIT = 56 * 1024 * 1024

F32 = jnp.float32
BF16 = jnp.bfloat16


def _bucket_np(dist):
    d = np.maximum(dist, 0)
    max_exact = NUM_BUCKETS // 2
    ratio = (np.log(np.maximum(d, 1).astype(np.float32) / np.float32(max_exact))
             / np.float32(math.log(MAX_DISTANCE / max_exact))
             * np.float32(NUM_BUCKETS - max_exact))
    large = np.minimum(max_exact + ratio.astype(np.int32), NUM_BUCKETS - 1)
    return np.where(d < max_exact, d, large).astype(np.int32)


def _dist_vector(table, d_lo, length, hi_valid):
    d = d_lo + np.arange(length)
    v = table.T[:, _bucket_np(d)]
    return jnp.where(jnp.asarray((d >= 0) & (d < hi_valid))[None], v, NEG)


def _toeplitz_kernel(specs, *refs):
    n = len(specs)
    for (rows, step), v_ref, o_ref in zip(specs, refs[:n], refs[n:]):
        length = v_ref.shape[2]
        x = jnp.broadcast_to(v_ref[0], (rows, length))
        shift = (length - step * (rows - 1)) % length
        o_ref[...] = pltpu.roll(x, shift, 1, stride=step, stride_axis=0)[:, :TQ]


def _toeplitz_tables(table, requests):
    specs, vecs = [], []
    for n_rows, d_first, step, hi_valid in requests:
        rows = -(-n_rows // 8) * 8
        length = -(-(step * (rows - 1) + TQ) // TQ) * TQ
        d_lo = d_first - step * (rows - 1)
        vecs.append(_dist_vector(table, d_lo, length, hi_valid)[:, None, :])
        specs.append((rows, step))
    outs = pl.pallas_call(
        functools.partial(_toeplitz_kernel, tuple(specs)), name="bias_tables", grid=(N_HEADS,),
        in_specs=[pl.BlockSpec((1, 1, v.shape[2]), lambda h: (h, 0, 0)) for v in vecs],
        out_specs=[pl.BlockSpec((rows, TQ), lambda h: (0, h)) for rows, _ in specs],
        out_shape=[jax.ShapeDtypeStruct((rows, N_HEADS * TQ), F32) for rows, _ in specs],
    )(*vecs)
    return [o[:req[0]] for o, req in zip(outs, requests)]


def _mask_rows(t, n_masked):
    rows = np.arange(t.shape[0])[:, None] < n_masked
    return jnp.where(jnp.asarray(rows), NEG, t)


def _make_tables(table, nq):
    big = 1 << 30
    per = TQ // CMP_STRIDE
    ncp = nq * per
    off = per * (nq - 1)
    sat_c = -(-(MAX_DISTANCE + CMP_LEN - 1) // CMP_STRIDE)
    assert off >= sat_c
    wide = N_HEADS * TQ
    near, wcut, band = _toeplitz_tables(table, [
        (NEAR, TQ, 1, big),
        (TQ, NSA_WINDOW, 1, NSA_WINDOW),
        (sat_c + per - 1, CMP_STRIDE * (sat_c - 1) - (CMP_LEN - 1), CMP_STRIDE, big)])
    in_window = (TQ - np.arange(NEAR)[:, None] + np.arange(wide)[None, :] % TQ) < SWA_WINDOW
    swa = jnp.where(jnp.asarray(in_window), near, NEG)
    near = jnp.stack([_mask_rows(near, TQ), near])
    swa = jnp.stack([_mask_rows(swa, TQ), swa])
    t31 = jnp.broadcast_to(table[NUM_BUCKETS - 1][:, None], (N_HEADS, TQ)).reshape(1, wide)

    def saturated(rows):
        return jnp.broadcast_to(t31, (rows, wide))

    wfar = jnp.concatenate([wcut, saturated(WIN_FAR - TQ)], axis=0)
    wfar = jnp.stack([_mask_rows(wfar, min(WIN_FAR, (N_WIN_VARIANTS - 1 - v) * TQ))
                      for v in range(N_WIN_VARIANTS)])
    cmpb = jnp.concatenate([saturated(off - sat_c + 1), band, jnp.full((ncp - per, wide), NEG, F32)], axis=0)
    return swa, near, wfar, t31, cmpb


def _overlap_t(s):
    ncp = s // CMP_STRIDE
    nsel = s // SEL_BLOCK
    cs = np.arange(ncp)[None, :] * CMP_STRIDE
    ss = np.arange(nsel)[:, None] * SEL_BLOCK
    ov = (cs < ss + SEL_BLOCK) & (cs + CMP_LEN > ss) & (np.arange(ncp)[None, :] < ncp - 1)
    return jnp.asarray(ov.astype(np.float32), dtype=BF16)


def _lanes(gain_b, n):
    return jnp.concatenate([gain_b] * (n // gain_b.shape[1]), axis=1)


def _rms_t(xt, gain_b):
    ms = jnp.mean(xt * xt, axis=0, keepdims=True)
    return (xt * lax.rsqrt(ms + EPS) * _lanes(gain_b, xt.shape[1])).astype(BF16)


def _head_norm_t(a, gain_b):
    outs = []
    for h in range(a.shape[0] // HEAD_DIM):
        blk = a[h * HEAD_DIM:(h + 1) * HEAD_DIM]
        ms = jnp.mean(blk * blk, axis=0, keepdims=True)
        outs.append(blk * lax.rsqrt(ms + EPS) * _lanes(gain_b, a.shape[1]))
    return jnp.concatenate(outs, axis=0)


def _sigmoid_h(hz):
    return 0.5 + 0.5 * jnp.tanh(hz)


def _silu_h(hz):
    return hz + hz * jnp.tanh(hz)


def _silu(z):
    return _silu_h(0.5 * z)


def _dot(a, b):
    return jnp.dot(a, b, preferred_element_type=F32)


def _proj_a(xn, w, qg, kg, q_out, k_out, v_out, g_out):
    pq = _dot(w[:HQ, :], xn)
    pkv = _dot(w[HQ:HQ + 2 * HKV, :], xn)
    q_out[0] = _head_norm_t(pq, qg[...]).astype(BF16)
    pz = _dot(w[HQ + 2 * HKV:, :], xn)
    k_out[0] = _head_norm_t(pkv[:HKV], kg[...]).T.astype(BF16)
    v_out[0] = pkv[HKV:].astype(BF16)
    g_out[0] = _silu_h(pz).astype(BF16)


def _proj_b(xn, w, qg, q_out, g_out):
    n_gate = N_BRANCH * N_HEADS
    pq = _dot(w[:HQ, :], xn)
    sg = _sigmoid_h(_dot(w[HQ:HQ + n_gate, :], xn))

    def gate_rows(pz, c):
        for hd in range(N_HEADS):
            r = c * N_HEADS + hd
            z = pz[hd * HEAD_DIM:(hd + 1) * HEAD_DIM]
            g_out[0, r * HEAD_DIM:(r + 1) * HEAD_DIM, :] = (_silu_h(z) * sg[r:r + 1]).astype(BF16)

    z0 = HQ + n_gate
    pz = _dot(w[z0:z0 + HQ, :], xn)
    q_out[0] = _head_norm_t(pq, qg[...]).astype(BF16)
    for c in range(1, N_BRANCH):
        nxt = _dot(w[z0 + c * HQ:z0 + (c + 1) * HQ, :], xn)
        gate_rows(pz, c - 1)
        pz = nxt
    gate_rows(pz, N_BRANCH - 1)


def _proj_kv(xn, wkv, kg1, kg2, craw_out, kslc_out, vslc_out, kwin_out, vwin_out):
    p = _dot(wkv[...], xn)

    def rows(n):
        return p[n * HKV:(n + 1) * HKV]
    for t in range(2):
        nat = rows(t).T
        for pr in range(HKV // PAIR):
            craw_out[0, t, pr] = nat[:, _pair_cols(pr)]
    kslc_out[0] = _head_norm_t(rows(2), kg1[...]).T.astype(BF16)
    vslc_out[0] = rows(3).astype(BF16)
    kwin_out[0] = _head_norm_t(rows(4), kg2[...]).T.astype(BF16)
    vwin_out[0] = rows(5).astype(BF16)


def _first_kernel(x_ref, ng, w, qg, kg, xt_out, q_out, k_out, v_out, g_out):
    sub = TM // 2
    for j in range(TM // sub):
        c = slice(j * sub, (j + 1) * sub)
        xt = x_ref[0, c, :].T
        xt_out[0, :, c] = xt
        _proj_a(_rms_t(xt, ng[...]), w, qg, kg, q_out.at[:, :, c], k_out.at[:, c, :], v_out.at[:, :, c],
                g_out.at[:, :, c])


def _a2a_kernel(x_ref, o_ref, wo, ng, w, qg, kg, xt_out, q_out, k_out, v_out, g_out):
    xt = x_ref[0] + _dot(wo[...], o_ref[0])
    xt_out[0] = xt
    _proj_a(_rms_t(xt, ng[...]), w, qg, kg, q_out, k_out, v_out, g_out)


def _a2b_kernel(x_ref, o_ref, wo, ngkv, wkv, kg1, kg2, ngb, w, qg,
                xt_out, craw_out, kslc_out, vslc_out, kwin_out, vwin_out, q_out, g_out):
    xt = x_ref[0] + _dot(wo[...], o_ref[0])
    xt_out[0] = xt
    _proj_kv(_rms_t(xt, ngkv[...]), wkv, kg1, kg2, craw_out, kslc_out, vslc_out, kwin_out, vwin_out)
    _proj_b(_rms_t(xt, ngb[...]), w, qg, q_out, g_out)


def _b2b_kernel(x_ref, o_ref, wo, ngb, w, qg, xt_out, q_out, g_out):
    xt = x_ref[0] + _dot(wo[...], o_ref[0])
    xt_out[0] = xt
    _proj_b(_rms_t(xt, ngb[...]), w, qg, q_out, g_out)


def _final_kernel(x_ref, o_ref, wo, x_out):
    x_out[0] = (x_ref[0] + _dot(wo[...], o_ref[0])).T


def _tok_spec(rows, tm=TM):
    return pl.BlockSpec((1, rows, tm), lambda b, t: (b, 0, t))


def _nat_spec(cols, tm=TM):
    return pl.BlockSpec((1, tm, cols), lambda b, t: (b, t, 0))


def _const_spec(shape):
    nd = len(shape)
    return pl.BlockSpec(shape, lambda b, t: (0,) * nd, pipeline_mode=pl.Buffered(1))


def _proj_call(body, name, b, s, ins, in_specs, out_shapes, out_specs, tm=TM):
    return pl.pallas_call(
        body, name=name, grid=(b, s // tm),
        in_specs=in_specs, out_specs=out_specs, out_shape=out_shapes,
        compiler_params=pltpu.CompilerParams(
            dimension_semantics=("parallel", "parallel"), vmem_limit_bytes=VMEM_LIMIT),
    )(*ins)


def _bcast(v, scale=1.0):
    return jnp.broadcast_to((v.astype(F32) * scale)[:, None], (v.shape[0], TQ))


def _a_weights(w_in, q_gain, k_gain):
    halve = np.where(np.arange(w_in.shape[1]) >= HQ + 2 * HKV, 0.5, 1.0).astype(np.float32)
    ws = [(w_in * halve).T.astype(BF16)]
    gains = [_bcast(q_gain, Q_SCALE), _bcast(k_gain)]
    return ws, gains


def _b_weights(w_in, q_gain):
    halve = np.where(np.arange(w_in.shape[1]) >= HQ, 0.5, 1.0).astype(np.float32)
    ws = [(w_in * halve).T.astype(BF16)]
    return ws, [_bcast(q_gain, Q_SCALE)]


def _a_outs(b, s):
    shapes = [jax.ShapeDtypeStruct((b, D_MODEL, s), F32), jax.ShapeDtypeStruct((b, HQ, s), BF16),
              jax.ShapeDtypeStruct((b, s, HKV), BF16), jax.ShapeDtypeStruct((b, HKV, s), BF16),
              jax.ShapeDtypeStruct((b, HQ, s), BF16)]
    specs = [_tok_spec(D_MODEL), _tok_spec(HQ), _nat_spec(HKV), _tok_spec(HKV), _tok_spec(HQ)]
    return shapes, specs


def _b_outs(b, s):
    shapes = [jax.ShapeDtypeStruct((b, HQ, s), BF16), jax.ShapeDtypeStruct((b, N_BRANCH * HQ, s), BF16)]
    specs = [_tok_spec(HQ), _tok_spec(N_BRANCH * HQ)]
    return shapes, specs


def _specs_for(arrs):
    return [_const_spec(a.shape) for a in arrs]


def _head_rows(h, g):
    return slice((h * GROUP + g) * HEAD_DIM, (h * GROUP + g + 1) * HEAD_DIM)


def _kv_rows(h):
    return slice(h * HEAD_DIM, (h + 1) * HEAD_DIM)


def _head_lanes(h):
    return slice(h * GL, (h + 1) * GL)


def _pair_lanes(p):
    return slice(2 * p * GL, 2 * (p + 1) * GL)


def _pair_cols(p):
    return slice(p * PAIR, (p + 1) * PAIR)


def _padded_q(q_ref, h, cols=slice(None)):
    qs = jnp.concatenate([q_ref[0, _head_rows(h, g), cols] for g in range(GROUP)], axis=1)
    zeros = jnp.zeros_like(qs)
    return jnp.concatenate([qs, zeros] if h % 2 == 0 else [zeros, qs], axis=0)


def _with_ones(v):
    return jnp.concatenate([v, jnp.ones((ONES_ROWS, v.shape[1]), BF16)], axis=0)


def _exp2_bf16(x):
    return jnp.exp2(x.astype(BF16))


def _swa_kernel(q_ref, kp_ref, kc_ref, vp_ref, vc_ref, g_ref, bias_ref, sink_ref, o_ref):
    i = pl.program_id(1)
    sink = sink_ref[...]
    for j in range(SWA_TILES):
        cur = slice(j * TQ, (j + 1) * TQ)
        prev = slice((j - 1) * TQ, j * TQ)
        k_prev = (lambda c: kp_ref[0, :, c]) if j == 0 else (lambda c: kc_ref[0, prev, c])
        v_prev = (lambda r: vp_ref[0, r, :]) if j == 0 else (lambda r: vc_ref[0, r, prev])
        bias = bias_ref[jnp.minimum(i, 1)] if j == 0 else bias_ref[1]
        pairs = []
        for p in range(N_KV_HEADS // 2):
            k = jnp.concatenate([k_prev(_pair_cols(p)), kc_ref[0, cur, _pair_cols(p)]], axis=0)
            qp = jnp.concatenate([_padded_q(q_ref, 2 * p, cur), _padded_q(q_ref, 2 * p + 1, cur)], axis=1)
            pairs.append(_dot(k, qp))
        s = jnp.concatenate(pairs, axis=1) + bias
        m = jnp.maximum(jnp.max(s, axis=0, keepdims=True), sink)
        e = _exp2_bf16(s - m)
        e_sink = jnp.exp2(sink - m)
        for h in range(N_KV_HEADS):
            v = jnp.concatenate([v_prev(_kv_rows(h)), vc_ref[0, _kv_rows(h), cur]], axis=1)
            acc = _dot(_with_ones(v), e[:, _head_lanes(h)])
            o = acc[:HEAD_DIM] / (acc[HEAD_DIM:HEAD_DIM + 1] + e_sink[:, _head_lanes(h)])
            for g in range(GROUP):
                rows = _head_rows(h, g)
                o_ref[0, rows, cur] = (o[:, g * TQ:(g + 1) * TQ] * g_ref[0, rows, cur].astype(F32)).astype(BF16)


def _swa_attention(qt, k, vt, gt, swa_tab, sink_row):
    b, _, s = qt.shape
    tw = SWA_TILES * TQ
    prev = lambda i: jnp.maximum(SWA_TILES * i - 1, 0)
    return pl.pallas_call(
        _swa_kernel, name="swa_attention", grid=(b, s // tw),
        in_specs=[
            pl.BlockSpec((1, HQ, tw), lambda b_, i: (b_, 0, i)),
            pl.BlockSpec((1, TQ, HKV), lambda b_, i: (b_, prev(i), 0)),
            pl.BlockSpec((1, tw, HKV), lambda b_, i: (b_, i, 0)),
            pl.BlockSpec((1, HKV, TQ), lambda b_, i: (b_, 0, prev(i))),
            pl.BlockSpec((1, HKV, tw), lambda b_, i: (b_, 0, i)),
            pl.BlockSpec((1, HQ, tw), lambda b_, i: (b_, 0, i)),
            pl.BlockSpec(memory_space=pltpu.VMEM), pl.BlockSpec(memory_space=pltpu.VMEM),
        ],
        out_specs=pl.BlockSpec((1, HQ, tw), lambda b_, i: (b_, 0, i)),
        out_shape=jax.ShapeDtypeStruct((b, HQ, s), BF16),
        compiler_params=pltpu.CompilerParams(
            dimension_semantics=("parallel", "arbitrary"), vmem_limit_bytes=VMEM_LIMIT),
    )(qt, k, k, vt, vt, gt, swa_tab, sink_row)


def _nsa_kernel(nq, q_ref, g_ref, kcmp_ref, vcmp_ref, kslc_ref, vslc_ref, kwin_ref, vwin_ref,
                cmpb_ref, near_ref, wfar_ref, t31_ref, ovt_ref, kauxf_ref, kauxn_ref, o_ref,
                qp_ref, m_ref, acc_ref, selb_ref, part_ref, sc0_ref, sc1_ref):
    i = pl.program_id(1)
    ncp = kcmp_ref.shape[2]
    nsel = ovt_ref.shape[0]
    n_pairs = N_KV_HEADS // 2
    wide = N_KV_HEADS * GL
    cmp_start = pl.multiple_of((TQ // CMP_STRIDE) * (nq - 1 - i), 8)
    prev_rows = pl.ds(pl.multiple_of(jnp.maximum(i - 1, 0) * TQ, TQ), TQ)
    cur_rows = pl.ds(pl.multiple_of(i * TQ, TQ), TQ)
    far_starts = [pl.multiple_of(jnp.maximum(i - back, 0) * TQ, TQ) for back in range(NSA_WINDOW // TQ, 1, -1)]
    n_far_blocks = (TQ // SEL_BLOCK) * (i - 1)

    def near_keys(k_ref, p):
        return jnp.concatenate([k_ref[0, prev_rows, _pair_cols(p)], k_ref[0, cur_rows, _pair_cols(p)]], axis=0)

    def near_values(v_ref, h):
        return jnp.concatenate([v_ref[0, _kv_rows(h), prev_rows], v_ref[0, _kv_rows(h), cur_rows]], axis=1)

    def scores(lhs_of_pair, rows):
        return jnp.concatenate([_dot(lhs_of_pair(p), qp_ref[rows, _pair_lanes(p)]) for p in range(n_pairs)], axis=1)

    for h in range(N_KV_HEADS):
        qp_ref[:PAIR, _head_lanes(h)] = _padded_q(q_ref, h)
    q_rows = slice(0, PAIR)
    near_tab = near_ref[0]
    bc = cmpb_ref[pl.ds(cmp_start, ncp), :]
    sc = scores(lambda p: kcmp_ref[0, 0, :, _pair_cols(p)], q_rows) + bc
    mc = jnp.max(sc, axis=0, keepdims=True)
    s_near = scores(lambda p: near_keys(kwin_ref, p), q_rows) + near_tab
    s_far = scores(lambda p: jnp.concatenate([kwin_ref[0, pl.ds(st, TQ), _pair_cols(p)] for st in far_starts], axis=0),
                   q_rows) + wfar_ref[0]
    mw = jnp.maximum(jnp.max(s_near, axis=0, keepdims=True), jnp.max(s_far, axis=0, keepdims=True))

    valid = bc > 0.5 * NEG
    ec = jnp.where(valid, jnp.exp2(sc - mc), 0.0)
    lc = jnp.sum(ec, axis=0, keepdims=True)
    pc = ec / jnp.where(lc > 0.0, lc, 1.0)
    pc_b = pc.astype(BF16)
    o_cmp = [_dot(vcmp_ref[0, 0, _kv_rows(h), :], pc_b[:, _head_lanes(h)]) for h in range(N_KV_HEADS)]

    psum = jnp.concatenate(
        [sum(pc[:, h * GL + g * TQ:h * GL + (g + 1) * TQ] for g in range(GROUP)) for h in range(N_KV_HEADS)], axis=1)
    ovt = ovt_ref[...]
    p1 = psum.astype(BF16)
    r1 = psum - p1.astype(F32)
    p2 = r1.astype(BF16)
    p3 = (r1 - p2.astype(F32)).astype(BF16)
    imp = _dot(ovt, p1) + _dot(ovt, p2) + _dot(ovt, p3)
    lane = lax.broadcasted_iota(jnp.int32, imp.shape, 1)
    pos = i * TQ + lane % TQ
    blk = lax.broadcasted_iota(jnp.int32, imp.shape, 0)
    causal = blk * SEL_BLOCK <= pos
    rel = pos // SEL_BLOCK - blk
    forced = (blk == 0) | ((rel >= 0) & (rel < SEL_FORCE_LOCAL))
    score = jnp.where(causal, imp + jnp.where(forced, FORCE_BONUS, 0.0), NEG)
    groups = [score[r:r + 8] for r in range(0, nsel, 8)]
    counts = [jnp.zeros((8, imp.shape[1]), jnp.int32) for _ in groups]
    row_in_group = lax.broadcasted_iota(jnp.int32, (8, imp.shape[1]), 0)
    for jp in range(nsel):
        row = score[jp:jp + 1, :]
        for gi, grp in enumerate(groups):
            if gi * 8 > jp:
                beats = row >= grp
            elif gi * 8 + 7 < jp:
                beats = row > grp
            else:
                beats = (row > grp) | ((row == grp) & (row_in_group > jp - gi * 8))
            counts[gi] = counts[gi] + beats.astype(jnp.int32)
    sel = jnp.concatenate(counts, axis=0) < min(SEL_TOP, nsel)

    def per_group(x):
        return jnp.concatenate([x[:, h * TQ:(h + 1) * TQ] for h in range(N_KV_HEADS) for _ in range(GROUP)], axis=1)

    selb_ref[...] = per_group(jnp.where(sel, 0.0, NEG))
    sel_far = per_group(jnp.where(sel & (blk < n_far_blocks), 0.0, NEG))
    t31 = t31_ref[...]
    hi = t31.astype(BF16).astype(F32)
    row8 = lax.broadcasted_iota(jnp.int32, (8, wide), 0)
    const_rows = jnp.where(row8 == 0, hi, jnp.where(row8 == 1, t31 - hi, 0.0))
    first_blk = (TQ // SEL_BLOCK) * (i - 1)
    near_rows = [selb_ref[pl.ds(jnp.maximum(first_blk + u, 0), 1), :] for u in range(NEAR // SEL_BLOCK)]
    aux = jnp.concatenate(
        [sel_far, const_rows] + near_rows + [jnp.zeros((PAIR - nsel - 8 - NEAR // SEL_BLOCK, wide), F32)], axis=0)
    qp_ref[PAIR:, :] = aux.astype(BF16)

    all_rows = slice(0, 2 * PAIR)
    s = scores(lambda p: jnp.concatenate([near_keys(kslc_ref, p), kauxn_ref[...]], axis=1), all_rows) + near_tab
    m = jnp.max(s, axis=0, keepdims=True)

    sc_refs = (sc0_ref, sc1_ref)

    def sweep_scores(step, dst_ref, cols, lanes):
        keys = slice(step * FAR, (step + 1) * FAR)
        lhs = jnp.concatenate([kslc_ref[0, keys, cols], kauxf_ref[keys, :]], axis=1)
        sc_new = _dot(lhs, qp_ref[:, lanes])
        dst_ref[:FAR, lanes] = sc_new
        dst_ref[FAR:FAR + 1, lanes] = jnp.max(sc_new, axis=0, keepdims=True)

    for p in range(n_pairs):
        sweep_scores(0, sc_refs[0], _pair_cols(p), _pair_lanes(p))

    m_ref[...] = m
    e = _exp2_bf16(s - m)
    for h in range(N_KV_HEADS):
        acc_ref[h] = _dot(_with_ones(near_values(vslc_ref, h)), e[:, _head_lanes(h)])

    e_near = _exp2_bf16(s_near - mw)
    e_far = _exp2_bf16(s_far - mw)
    for h in range(N_KV_HEADS):
        far_v = jnp.concatenate([vwin_ref[0, _kv_rows(h), pl.ds(st, TQ)] for st in far_starts], axis=1)
        acc_w = (_dot(_with_ones(near_values(vwin_ref, h)), e_near[:, _head_lanes(h)])
                 + _dot(_with_ones(far_v), e_far[:, _head_lanes(h)]))
        o_win = acc_w[:HEAD_DIM] / acc_w[HEAD_DIM:HEAD_DIM + 1]
        for g in range(GROUP):
            rows = _head_rows(h, g)
            lanes = slice(g * TQ, (g + 1) * TQ)
            part_ref[rows, :] = (o_cmp[h][:, lanes] * g_ref[0, 0, rows, :].astype(F32)
                                 + o_win[:, lanes] * g_ref[0, 2, rows, :].astype(F32))

    def sweep_consume(step, src_ref, lanes, heads, n_keys=FAR):
        s = src_ref[:n_keys, lanes]
        m_old = m_ref[:, lanes]
        m_new = jnp.maximum(m_old, src_ref[FAR:FAR + 1, lanes])
        alpha = jnp.exp2(m_old - m_new)
        e = _exp2_bf16(s - m_new)
        for j, h in enumerate(heads):
            sub = slice(j * GL, (j + 1) * GL)
            v = _with_ones(vslc_ref[0, _kv_rows(h), step * FAR:step * FAR + n_keys])
            acc_ref[h] = alpha[:, sub] * acc_ref[h] + _dot(v, e[:, sub])
        m_ref[:, lanes] = m_new

    n_far_chunks = jnp.maximum(i - 1, 0)
    n_steps = (n_far_chunks + FAR // TQ - 1) // (FAR // TQ)
    max_steps = kslc_ref.shape[1] // FAR
    for step in range(max_steps):
        if step + 1 < max_steps:
            @pl.when(step + 1 < n_steps)
            def _(step=step):
                for h in range(N_KV_HEADS):
                    sweep_consume(step, sc_refs[step % 2], _head_lanes(h), (h,))
                    sweep_scores(step + 1, sc_refs[(step + 1) % 2], _pair_cols(h // 2), _head_lanes(h))

        for chunks in range(1, FAR // TQ + 1):
            @pl.when(n_far_chunks == step * (FAR // TQ) + chunks)
            def _(step=step, chunks=chunks):
                sweep_consume(step, sc_refs[step % 2], slice(0, wide), tuple(range(N_KV_HEADS)), chunks * TQ)

    for h in range(N_KV_HEADS):
        o_slc = acc_ref[h, :HEAD_DIM, :] / acc_ref[h, HEAD_DIM:HEAD_DIM + 1, :]
        for g in range(GROUP):
            rows = _head_rows(h, g)
            o = part_ref[rows, :] + o_slc[:, g * TQ:(g + 1) * TQ] * g_ref[0, 1, rows, :].astype(F32)
            o_ref[0, rows, :] = o.astype(BF16)


def _nsa_attention(qt, gt, cmp_nat, cmp_t, kslc, vslc, kwin, vwin, tabs, ovt):
    b, _, s = qt.shape
    nq = s // TQ
    assert s % FAR == 0
    ncp = s // CMP_STRIDE
    nsel = s // SEL_BLOCK
    near, wfar, t31, cmpb = tabs
    g4 = gt.reshape(b, N_BRANCH, HQ, s)
    assert nsel + 8 + NEAR // SEL_BLOCK <= PAIR
    cols = np.arange(PAIR)[None, :]
    key_blk = np.arange(s)[:, None] // SEL_BLOCK
    kaux_far = jnp.asarray(((cols == key_blk) | (cols == nsel) | (cols == nsel + 1)).astype(np.float32), BF16)
    near_blk = np.arange(NEAR)[:, None] // SEL_BLOCK
    kaux_near = jnp.asarray((cols == nsel + 8 + near_blk).astype(np.float32), BF16)
    resident = pl.BlockSpec(memory_space=pltpu.VMEM)
    return pl.pallas_call(
        functools.partial(_nsa_kernel, nq), name="nsa_attention", grid=(b, nq),
        in_specs=[
            pl.BlockSpec((1, HQ, TQ), lambda b_, i: (b_, 0, i)),
            pl.BlockSpec((1, N_BRANCH, HQ, TQ), lambda b_, i: (b_, 0, 0, i)),
            pl.BlockSpec((1, 1, ncp, HKV), lambda b_, i: (0, b_, 0, 0)),
            pl.BlockSpec((1, 1, HKV, ncp), lambda b_, i: (1, b_, 0, 0)),
            pl.BlockSpec((1, s, HKV), lambda b_, i: (b_, 0, 0)),
            pl.BlockSpec((1, HKV, s), lambda b_, i: (b_, 0, 0)),
            pl.BlockSpec((1, s, HKV), lambda b_, i: (b_, 0, 0)),
            pl.BlockSpec((1, HKV, s), lambda b_, i: (b_, 0, 0)),
            resident,
            pl.BlockSpec((1, NEAR, N_KV_HEADS * GL), lambda b_, i: (jnp.minimum(i, 1), 0, 0)),
            pl.BlockSpec((1, WIN_FAR, N_KV_HEADS * GL),
                         lambda b_, i: (jnp.minimum(i, N_WIN_VARIANTS - 1), 0, 0)),
            resident, resident, resident, resident,
        ],
        out_specs=pl.BlockSpec((1, HQ, TQ), lambda b_, i: (b_, 0, i)),
        out_shape=jax.ShapeDtypeStruct((b, HQ, s), BF16),
        scratch_shapes=[pltpu.VMEM((2 * PAIR, N_KV_HEADS * GL), BF16),
                        pltpu.VMEM((1, N_KV_HEADS * GL), F32),
                        pltpu.VMEM((N_KV_HEADS, HEAD_DIM + ONES_ROWS, GL), F32),
                        pltpu.VMEM((nsel, N_KV_HEADS * GL), F32),
                        pltpu.VMEM((HQ, TQ), F32),
                        pltpu.VMEM((FAR + 8, N_KV_HEADS * GL), F32), pltpu.VMEM((FAR + 8, N_KV_HEADS * GL), F32)],
        compiler_params=pltpu.CompilerParams(
            dimension_semantics=("parallel", "arbitrary"), vmem_limit_bytes=VMEM_LIMIT),
    )(qt, g4, cmp_nat, cmp_t, kslc, vslc, kwin, vwin, cmpb, near, wfar, t31, ovt, kaux_far, kaux_near)


def _compress_kernel(u_ref, ptop_ref, pbot_ref, wtop_ref, wbot_ref, w2_ref, kg_ref, nat_out, t_out):
    t = pl.program_id(0)
    nch = u_ref.shape[3] // CMP_STRIDE
    hidden = []
    for p in range(HKV // PAIR):
        u = jnp.concatenate([u_ref[0, 0, p, pl.ds(j, nch, stride=CMP_STRIDE), :] for j in range(CMP_STRIDE)], axis=1)
        top = (u + ptop_ref[0]).astype(BF16)
        bot = (u + pbot_ref[0]).astype(BF16)
        for e in range(2):
            bm = _dot(bot, wbot_ref[0, e])
            hidden.append(_dot(top, wtop_ref[0, e]) + pltpu.roll(bm, nch - 1, axis=0))
    hid = jnp.concatenate(hidden, axis=1)
    out_t = _dot(_silu(hid).astype(BF16), w2_ref[0]).T
    out_t = jnp.where(t == 0, _head_norm_t(out_t, kg_ref[...]), out_t)
    t_out[0, 0] = out_t.astype(BF16)
    nat_out[0, 0] = out_t.T.astype(BF16)


def _compress(craw, cmp_k_pos, cmp_k_w1, cmp_k_w2, cmp_v_pos, cmp_v_w1, cmp_v_w2, k_gain):
    b, _, n_pairs, s, _ = craw.shape
    nch = s // CMP_STRIDE
    wide = CMP_STRIDE * PAIR

    def padded_w1(w1_half):
        w = w1_half.reshape(CMP_STRIDE, HEAD_DIM, CMP_HIDDEN)
        z = jnp.zeros_like(w)
        both = jnp.stack([jnp.concatenate([w, z], axis=1), jnp.concatenate([z, w], axis=1)])
        return both.reshape(2, wide, CMP_HIDDEN).astype(BF16)

    def pos_row(p_half):
        return jnp.broadcast_to(p_half[:, None, :], (CMP_STRIDE, 2, HEAD_DIM)).reshape(1, wide)

    def big_w2(w2):
        eye = jnp.eye(N_KV_HEADS, dtype=F32)
        return jnp.einsum('nd,hk->hnkd', w2, eye).reshape(N_KV_HEADS * CMP_HIDDEN, HKV).astype(BF16)

    half = CMP_STRIDE * HEAD_DIM
    wtop = jnp.stack([padded_w1(cmp_k_w1[:half]), padded_w1(cmp_v_w1[:half])])
    wbot = jnp.stack([padded_w1(cmp_k_w1[half:]), padded_w1(cmp_v_w1[half:])])
    ptop = jnp.stack([pos_row(cmp_k_pos[:CMP_STRIDE]), pos_row(cmp_v_pos[:CMP_STRIDE])])
    pbot = jnp.stack([pos_row(cmp_k_pos[CMP_STRIDE:]), pos_row(cmp_v_pos[CMP_STRIDE:])])
    w2 = jnp.stack([big_w2(cmp_k_w2), big_w2(cmp_v_w2)])
    kg = jnp.broadcast_to(k_gain.astype(F32)[:, None], (HEAD_DIM, nch))
    sel = lambda shape: pl.BlockSpec((1,) + shape, lambda t, b_: (t,) + (0,) * len(shape))
    return pl.pallas_call(
        _compress_kernel, name="compress", grid=(2, b),
        in_specs=[
            pl.BlockSpec((1, 1, n_pairs, s, PAIR), lambda t, b_: (b_, t, 0, 0, 0)),
            sel((1, wide)), sel((1, wide)),
            sel((2, wide, CMP_HIDDEN)), sel((2, wide, CMP_HIDDEN)),
            sel((N_KV_HEADS * CMP_HIDDEN, HKV)),
            pl.BlockSpec((HEAD_DIM, nch), lambda t, b_: (0, 0)),
        ],
        out_specs=[pl.BlockSpec((1, 1, nch, HKV), lambda t, b_: (t, b_, 0, 0)),
                   pl.BlockSpec((1, 1, HKV, nch), lambda t, b_: (t, b_, 0, 0))],
        out_shape=[jax.ShapeDtypeStruct((2, b, nch, HKV), BF16),
                   jax.ShapeDtypeStruct((2, b, HKV, nch), BF16)],
        compiler_params=pltpu.CompilerParams(
            dimension_semantics=("arbitrary", "arbitrary"), vmem_limit_bytes=VMEM_LIMIT),
    )(craw, ptop, pbot, wtop, wbot, w2, kg)


def kernel(x, rel_table, a_norm, a_w_in, a_q_gain, a_k_gain, a_sink, a_w_out, kv_norm, kv_w,
           kv_k_gain, cmp_k_pos, cmp_k_w1, cmp_k_w2, cmp_v_pos, cmp_v_w1, cmp_v_w2,
           b_norm, b_w_in, b_q_gain, b_w_out):
    b, s, _ = x.shape
    nq = s // TQ
    n_a = a_w_in.shape[0]
    n_b = b_w_in.shape[0]
    swa_tab, near, wfar, t31, cmpb = _make_tables(rel_table.astype(F32) * LOG2E, nq)
    ovt = _overlap_t(s)

    a_shapes, a_specs = _a_outs(b, s)
    b_shapes, b_specs = _b_outs(b, s)
    xt_shape, xt_spec = a_shapes[0], a_specs[0]

    ws, gains = _a_weights(a_w_in[0], a_q_gain[0], a_k_gain[0])
    consts = [_bcast(a_norm[0])] + ws + gains
    xt, qt, k, vt, gt = _proj_call(
        _first_kernel, "proj_first", b, s, [x] + consts,
        [_nat_spec(D_MODEL)] + _specs_for(consts), a_shapes, a_specs)

    for layer in range(n_a):
        sink_row = jnp.broadcast_to(
            (a_sink[layer].astype(F32) * LOG2E).reshape(N_KV_HEADS, 1, GROUP, 1), (N_KV_HEADS, 1, GROUP, TQ)
        ).reshape(1, N_KV_HEADS * GL)
        ot = _swa_attention(qt, k, vt, gt, swa_tab, sink_row)
        wo = a_w_out[layer].T.astype(BF16)
        if layer + 1 < n_a:
            ws, gains = _a_weights(a_w_in[layer + 1], a_q_gain[layer + 1], a_k_gain[layer + 1])
            consts = [wo, _bcast(a_norm[layer + 1])] + ws + gains
            xt, qt, k, vt, gt = _proj_call(
                _a2a_kernel, "proj_a2a", b, s, [xt, ot] + consts,
                [_tok_spec(D_MODEL), _tok_spec(HQ)] + _specs_for(consts), a_shapes, a_specs)
        else:
            wsb, gb = _b_weights(b_w_in[0], b_q_gain[0])
            consts = ([wo, _bcast(kv_norm), kv_w.T.astype(BF16), _bcast(kv_k_gain[1]), _bcast(kv_k_gain[2]),
                       _bcast(b_norm[0])] + wsb + gb)
            kv_shapes = [jax.ShapeDtypeStruct((b, 2, HKV // PAIR, s, PAIR), F32),
                         jax.ShapeDtypeStruct((b, s, HKV), BF16), jax.ShapeDtypeStruct((b, HKV, s), BF16),
                         jax.ShapeDtypeStruct((b, s, HKV), BF16), jax.ShapeDtypeStruct((b, HKV, s), BF16)]
            kv_specs = [pl.BlockSpec((1, 2, HKV // PAIR, TM, PAIR), lambda b_, t: (b_, 0, 0, t, 0)),
                        _nat_spec(HKV), _tok_spec(HKV), _nat_spec(HKV), _tok_spec(HKV)]
            xt, craw, kslc, vslc, kwin, vwin, qt, gt = _proj_call(
                _a2b_kernel, "proj_a2b", b, s, [xt, ot] + consts,
                [_tok_spec(D_MODEL), _tok_spec(HQ)] + _specs_for(consts),
                [xt_shape] + kv_shapes + b_shapes, [xt_spec] + kv_specs + b_specs)

    cmp_nat, cmp_t = _compress(craw, cmp_k_pos, cmp_k_w1, cmp_k_w2, cmp_v_pos, cmp_v_w1, cmp_v_w2,
                               kv_k_gain[0])
    tabs = (near, wfar, t31, cmpb)
    for layer in range(n_b):
        ot = _nsa_attention(qt, gt, cmp_nat, cmp_t, kslc, vslc, kwin, vwin, tabs, ovt)
        wo = b_w_out[layer].T.astype(BF16)
        if layer + 1 < n_b:
            wsb, gb = _b_weights(b_w_in[layer + 1], b_q_gain[layer + 1])
            consts = [wo, _bcast(b_norm[layer + 1])] + wsb + gb
            xt, qt, gt = _proj_call(
                _b2b_kernel, "proj_b2b", b, s, [xt, ot] + consts,
                [_tok_spec(D_MODEL), _tok_spec(HQ)] + _specs_for(consts),
                [xt_shape] + b_shapes, [xt_spec] + b_specs)
        else:
            out = _proj_call(
                _final_kernel, "proj_final", b, s, [xt, ot, wo],
                [_tok_spec(D_MODEL, TM_FINAL), _tok_spec(HQ, TM_FINAL), _const_spec(wo.shape)],
                jax.ShapeDtypeStruct((b, s, D_MODEL), F32), _nat_spec(D_MODEL, TM_FINAL), TM_FINAL)
    return out
```
